```python
import math
import jax, jax.numpy as jnp
from jax import lax
import numpy as np

D_MODEL = 1024
BATCH = 8
SEQ = 4096
DEPTH = 1

HEAD_DIM = 64
DIFF_HEADS = 4
DIFF_VDIM = 2 * HEAD_DIM
DIFF_WIDTH = DIFF_HEADS * DIFF_VDIM
DIL_HEADS = 4
DIL_WIDTH = DIL_HEADS * HEAD_DIM
DIL_WINDOWS = (128, 512, 2048)
DIL_DILATIONS = (1, 4, 16)
DIL_NKEYS = DIL_WINDOWS[0] // DIL_DILATIONS[0] + 1
MEM_HEADS = 4
MEM_WIDTH = MEM_HEADS * HEAD_DIM
N_MEM = 256
D_MIX = DIFF_WIDTH + DIL_WIDTH + MEM_WIDTH
C_DIFF_Q = 0
C_DIFF_K = C_DIFF_Q + DIFF_WIDTH
C_DIFF_V = C_DIFF_K + DIFF_WIDTH
C_DIL_Q = C_DIFF_V + DIFF_WIDTH
C_DIL_K = C_DIL_Q + DIL_WIDTH
C_DIL_V = C_DIL_K + DIL_WIDTH
C_MEM_Q = C_DIL_V + DIL_WIDTH
D_IN = C_MEM_Q + MEM_WIDTH
Q_BLOCK = 128
DIL_Q_BLOCK = 64
N_EXPERTS = 32
TOP_K = 4
D_FF = D_MODEL
SWIGLU_ALPHA = 1.702
SWIGLU_LIMIT = 7.0
MOE_BLOCK = 256
NORM_EPS = 1e-5

kernel_name = "hymba_diff_dilated_mem_moe_block"


def rms_norm(x, g):
    xf = x.astype(jnp.float32)
    y = xf * lax.rsqrt(jnp.mean(xf * xf, axis=-1, keepdims=True) + NORM_EPS)
    return (y * g.astype(jnp.float32)).astype(x.dtype)


def alibi_slopes():
    n = DIFF_HEADS + DIL_HEADS
    s = 2.0 ** (-8.0 * (jnp.arange(n, dtype=jnp.float32) + 1.0) / n)
    return s[0::2], s[1::2]


def dilated_distances():
    r = jnp.array(DIL_DILATIONS, dtype=jnp.int32)
    return jnp.arange(DIL_NKEYS, dtype=jnp.int32)[None, :] * r[:, None]


def split_heads(t, n_heads, d):
    b, s, _ = t.shape
    return t.reshape(b, s, n_heads, d).transpose(0, 2, 1, 3)


def merge_heads(t):
    b, h, s, d = t.shape
    return t.transpose(0, 2, 1, 3).reshape(b, s, h * d)


def diff_attention(q1, q2, k1, k2, v, slopes, lam):
    b, h, s, dh = q1.shape
    scale = dh ** -0.5
    kpos = jnp.arange(s)

    def block(i):
        start = i * Q_BLOCK
        qb1 = lax.dynamic_slice_in_dim(q1, start, Q_BLOCK, axis=2)
        qb2 = lax.dynamic_slice_in_dim(q2, start, Q_BLOCK, axis=2)
        qpos = start + jnp.arange(Q_BLOCK)
        dist = qpos[:, None] - kpos[None, :]
        causal = dist >= 0
        bias = -slopes[:, None, None] * dist.astype(jnp.float32)

        def probs(qb, k):
            sc = jnp.einsum('bhqd,bhkd->bhqk', qb, k).astype(jnp.float32) * scale + bias
            return jax.nn.softmax(jnp.where(causal, sc, -jnp.inf), axis=-1)

        a = probs(qb1, k1) - lam * probs(qb2, k2)
        return jnp.einsum('bhqk,bhkd->bhqd', a.astype(v.dtype), v)

    o = lax.map(block, jnp.arange(s // Q_BLOCK))
    return o.transpose(1, 2, 0, 3, 4).reshape(b, h, s, v.shape[-1])


def dilated_attention(q, k, v, slopes):
    b, h, s, dh = q.shape
    scale = dh ** -0.5
    dist = dilated_distances()
    bias = -slopes[:, None, None, None] * dist.astype(jnp.float32)

    def block(i):
        start = i * DIL_Q_BLOCK
        qb = lax.dynamic_slice_in_dim(q, start, DIL_Q_BLOCK, axis=2)
        qpos = start + jnp.arange(DIL_Q_BLOCK)
        kidx = qpos[:, None, None] - dist[None]
        valid = kidx >= 0
        kidx = jnp.maximum(kidx, 0)
        kg = jnp.take(k, kidx, axis=2, mode='clip')
        vg = jnp.take(v, kidx, axis=2, mode='clip')
        sc = jnp.einsum('bhqd,bhqnkd->bhqnk', qb, kg).astype(jnp.float32) * scale + bias
        sc = jnp.where(valid, sc, -jnp.inf)
        m = jnp.max(sc, axis=-1, keepdims=True)
        p = jnp.exp(sc - m)
        denom = jnp.sum(p, axis=-1)
        o = jnp.einsum('bhqnk,bhqnkd->bhqnd', p.astype(v.dtype), vg).astype(jnp.float32) / denom[..., None]
        w = jax.nn.softmax(m[..., 0] + jnp.log(denom), axis=-1)
        return jnp.einsum('bhqn,bhqnd->bhqd', w, o)

    o = lax.map(block, jnp.arange(s // DIL_Q_BLOCK))
    return o.transpose(1, 2, 0, 3, 4).reshape(b, h, s, dh)


def memory_attention(q, mk, mv):
    scale = q.shape[-1] ** -0.5
    sc = jnp.einsum('bhsd,bhmd->bhsm', q, mk).astype(jnp.float32) * scale
    a = jax.nn.softmax(sc, axis=-1)
    return jnp.einsum('bhsm,bhmd->bhsd', a.astype(mv.dtype), mv)


def moe_ffn(h, w_router, b_router, w_gu, b_gu, w_dn, b_dn):
    b, s, d = h.shape
    t = b * s
    xf = h.reshape(t, d)
    logits = (xf @ w_router + b_router).astype(jnp.float32)
    top_v, top_e = lax.top_k(logits, TOP_K)
    gates = jax.nn.softmax(top_v, axis=-1)
    n_assign = t * TOP_K
    e_flat = top_e.reshape(-1)
    g_flat = gates.reshape(-1)
    tok_flat = jnp.arange(n_assign, dtype=jnp.int32) // TOP_K
    order = jnp.argsort(e_flat, stable=True)
    e_sorted = e_flat[order]
    counts = jnp.bincount(e_flat, length=N_EXPERTS)
    padded = (counts + MOE_BLOCK - 1) // MOE_BLOCK * MOE_BLOCK
    pad_end = jnp.cumsum(padded)
    pad_start = pad_end - padded
    start = jnp.cumsum(counts) - counts
    rank = jnp.arange(n_assign, dtype=jnp.int32) - start[e_sorted]
    dest = pad_start[e_sorted] + rank
    n_blocks = -(-n_assign // MOE_BLOCK) + N_EXPERTS
    n_rows = n_blocks * MOE_BLOCK
    row_tok = jnp.zeros((n_rows,), jnp.int32).at[dest].set(tok_flat[order])
    row_gate = jnp.zeros((n_rows,), jnp.float32).at[dest].set(g_flat[order])
    block_expert = jnp.minimum(
        jnp.searchsorted(pad_end, jnp.arange(n_blocks) * MOE_BLOCK, side='right'), N_EXPERTS - 1)
    xb = xf[row_tok].reshape(n_blocks, MOE_BLOCK, d)

    def expert_block(args):
        xe, e = args
        gu = xe @ w_gu[e] + b_gu[e]
        gate = jnp.minimum(gu[..., 0::2], SWIGLU_LIMIT)
        up = jnp.clip(gu[..., 1::2], -SWIGLU_LIMIT, SWIGLU_LIMIT)
        act = (up + 1.0) * (gate * jax.nn.sigmoid(gate * SWIGLU_ALPHA))
        return act @ w_dn[e] + b_dn[e]

    yb = lax.map(expert_block, (xb, block_expert))
    out = jnp.zeros((t, d), jnp.float32).at[row_tok].add(
        yb.reshape(n_rows, d).astype(jnp.float32) * row_gate[:, None])
    return out.reshape(b, s, d).astype(h.dtype)


def setup_inputs(seed: int = 0) -> dict:
    key = jax.random.key(seed)
    ks = jax.random.split(key, 18)
    f32 = jnp.float32

    def nrm(k, shape, scale):
        return jax.random.normal(k, shape, f32) * scale

    return {
        "x": nrm(ks[0], (BATCH, SEQ, D_MODEL), 1.0),
        "mem": nrm(ks[1], (BATCH, N_MEM, D_MODEL), 1.0),
        "attn_norm_g": 1.0 + nrm(ks[2], (DEPTH, D_MODEL), 0.02),
        "mem_norm_g": 1.0 + nrm(ks[3], (DEPTH, D_MODEL), 0.02),
        "w_in": nrm(ks[4], (DEPTH, D_MODEL, D_IN), D_MODEL ** -0.5),
        "diff_lambda_qk": nrm(ks[5], (DEPTH, 4, HEAD_DIM), 0.1),
        "diff_subln_g": 1.0 + nrm(ks[6], (DEPTH, DIFF_VDIM), 0.02),
        "w_mem_kv": nrm(ks[7], (DEPTH, D_MODEL, 2 * MEM_WIDTH), D_MODEL ** -0.5),
        "w_out": nrm(ks[8], (DEPTH, D_MIX, D_MODEL), D_MIX ** -0.5),
        "ffn_norm_g": 1.0 + nrm(ks[9], (DEPTH, D_MODEL), 0.02),
        "w_router": nrm(ks[10], (DEPTH, D_MODEL, N_EXPERTS), D_MODEL ** -0.5),
        "b_router": nrm(ks[11], (DEPTH, N_EXPERTS), 0.01),
        "w_gate_up": nrm(ks[12], (DEPTH, N_EXPERTS, D_MODEL, 2 * D_FF), D_MODEL ** -0.5),
        "b_gate_up": nrm(ks[13], (DEPTH, N_EXPERTS, 2 * D_FF), 0.01),
        "w_down": nrm(ks[14], (DEPTH, N_EXPERTS, D_FF, D_MODEL), D_FF ** -0.5),
        "b_down": nrm(ks[15], (DEPTH, N_EXPERTS, D_MODEL), 0.01),
        "final_norm_g": 1.0 + nrm(ks[16], (D_MODEL,), 0.02),
    }


def reference(x, mem, attn_norm_g, mem_norm_g, w_in, diff_lambda_qk, diff_subln_g, w_mem_kv,
              w_out, ffn_norm_g, w_router, b_router, w_gate_up, b_gate_up, w_down, b_down,
              final_norm_g):
    b, s, _ = x.shape
    diff_slopes, dil_slopes = alibi_slopes()
    for l in range(DEPTH):
        h = rms_norm(x, attn_norm_g[l])
        p = h @ w_in[l]

        lam_init = 0.8 - 0.6 * math.exp(-0.3 * l)
        lv = diff_lambda_qk[l].astype(jnp.float32)
        lam = jnp.exp(jnp.sum(lv[0] * lv[1])) - jnp.exp(jnp.sum(lv[2] * lv[3])) + lam_init
        dq = p[..., C_DIFF_Q:C_DIFF_K].reshape(b, s, DIFF_HEADS, 2, HEAD_DIM).transpose(0, 2, 3, 1, 4)
        dk = p[..., C_DIFF_K:C_DIFF_V].reshape(b, s, DIFF_HEADS, 2, HEAD_DIM).transpose(0, 2, 3, 1, 4)
        dv = split_heads(p[..., C_DIFF_V:C_DIL_Q], DIFF_HEADS, DIFF_VDIM)
        o_diff = diff_attention(dq[:, :, 0], dq[:, :, 1], dk[:, :, 0], dk[:, :, 1], dv, diff_slopes, lam)
        o_diff = rms_norm(o_diff, diff_subln_g[l]) * (1.0 - lam_init)
        o_diff = merge_heads(o_diff.astype(x.dtype))

        lq = split_heads(p[..., C_DIL_Q:C_DIL_K], DIL_HEADS, HEAD_DIM)
        lk = split_heads(p[..., C_DIL_K:C_DIL_V], DIL_HEADS, HEAD_DIM)
        lvv = split_heads(p[..., C_DIL_V:C_MEM_Q], DIL_HEADS, HEAD_DIM)
        o_dil = merge_heads(dilated_attention(lq, lk, lvv, dil_slopes).astype(x.dtype))

        mq = split_heads(p[..., C_MEM_Q:D_IN], MEM_HEADS, HEAD_DIM)
        mkv = rms_norm(mem, mem_norm_g[l]) @ w_mem_kv[l]
        mk = split_heads(mkv[..., :MEM_WIDTH], MEM_HEADS, HEAD_DIM)
        mv = split_heads(mkv[..., MEM_WIDTH:], MEM_HEADS, HEAD_DIM)
        o_mem = merge_heads(memory_attention(mq, mk, mv).astype(x.dtype))

        mix = jnp.concatenate([o_diff, o_dil, o_mem], axis=-1)
        x = x + mix @ w_out[l]

        x = x + moe_ffn(rms_norm(x, ffn_norm_g[l]), w_router[l], b_router[l],
                        w_gate_up[l], b_gate_up[l], w_down[l], b_down[l])
    return rms_norm(x, final_norm_g)
```

```python
import functools
import math

import jax
import jax.numpy as jnp
from jax import lax
from jax.experimental import pallas as pl
from jax.experimental.pallas import tpu as pltpu

F32 = jnp.float32
BF16 = jnp.bfloat16
I32 = jnp.int32

NORM_EPS = 1e-5
HEAD_DIM = 64
N_HEADS = 4
DIFF_VDIM = 2 * HEAD_DIM
DIFF_WIDTH = N_HEADS * DIFF_VDIM
GROUP_WIDTH = N_HEADS * HEAD_DIM
DIL_DILATIONS = (1, 4, 16)
DIL_WINDOW_KEYS = 128
TOP_K = 4
SWIGLU_ALPHA = 1.702
SWIGLU_LIMIT = 7.0
NEG = -1e30

V7X_LANES = 128
V7X_VMEM_LIMIT = 56 * 1024 * 1024

TOK_TILE = 512
ATT_TILE = 256
DIL_TILE = DIL_WINDOW_KEYS
ROW_TILE = 512
DMA_TILE = 256


def _cparams(*sem):
    return pltpu.CompilerParams(dimension_semantics=tuple(sem), vmem_limit_bytes=V7X_VMEM_LIMIT)


def _rms(x, g):
    return x * lax.rsqrt(jnp.mean(x * x, axis=-1, keepdims=True) + NORM_EPS) * g


def _dot_nt(a, b):
    return lax.dot_general(a, b, (((1,), (1,)), ((), ())), preferred_element_type=F32)


def _dot(a, b):
    return jnp.dot(a, b, preferred_element_type=F32)


def _memkv_kernel(mem_ref, g_ref, w_ref, mk_ref, mv_ref):
    h = _rms(mem_ref[0], g_ref[...]).astype(BF16)
    kv = _dot(h, w_ref[...])
    mk_ref[0] = kv[:, :GROUP_WIDTH].astype(BF16)
    mv_ref[0] = kv[:, GROUP_WIDTH:].astype(BF16)


def _mem_kv(mem, g, w):
    b, m, d = mem.shape
    out = jax.ShapeDtypeStruct((b, m, GROUP_WIDTH), BF16)
    return pl.pallas_call(
        _memkv_kernel,
        grid=(b,),
        in_specs=[pl.BlockSpec((1, m, d), lambda i: (i, 0, 0)),
                  pl.BlockSpec((1, d), lambda i: (0, 0)),
                  pl.BlockSpec((d, 2 * GROUP_WIDTH), lambda i: (0, 0))],
        out_specs=[pl.BlockSpec((1, m, GROUP_WIDTH), lambda i: (i, 0, 0))] * 2,
        out_shape=[out, out],
        compiler_params=_cparams("arbitrary"),
        name="mem_kv",
    )(mem, g, w)


_IN_SEGMENTS = ((DIFF_WIDTH, HEAD_DIM ** -0.5), (DIFF_WIDTH, 1.0), (DIFF_WIDTH, 1.0),
                (GROUP_WIDTH, HEAD_DIM ** -0.5), (GROUP_WIDTH, 1.0), (GROUP_WIDTH, 1.0),
                (GROUP_WIDTH, HEAD_DIM ** -0.5))


def _inproj_kernel(x_ref, g_ref, w_ref, *out_refs):
    h = _rms(x_ref[...], g_ref[...]).astype(BF16)
    c0 = 0
    for ref, (n, scale) in zip(out_refs, _IN_SEGMENTS):
        p = _dot(h, w_ref[:, c0:c0 + n])
        if scale != 1.0:
            p = p * scale
        ref[...] = p.astype(BF16)
        c0 += n


def _in_proj(x2d, g, w):
    t, d = x2d.shape
    tm = min(TOK_TILE, t)
    return pl.pallas_call(
        _inproj_kernel,
        grid=(t // tm,),
        in_specs=[pl.BlockSpec((tm, d), lambda i: (i, 0)),
                  pl.BlockSpec((1, d), lambda i: (0, 0)),
                  pl.BlockSpec(w.shape, lambda i: (0, 0))],
        out_specs=[pl.BlockSpec((tm, n), lambda i: (i, 0)) for n, _ in _IN_SEGMENTS],
        out_shape=[jax.ShapeDtypeStruct((t, n), BF16) for n, _ in _IN_SEGMENTS],
        compiler_params=_cparams("arbitrary"),
        name="in_proj",
    )(x2d, g, w)


def _diff_kernel(slopes_ref, lam_ref, g_ref, q_ref, k_ref, v_ref, o_ref,
                 m1, l1, a1, m2, l2, a2, *, lam_init, tile):
    h = pl.program_id(1)
    i = pl.program_id(2)
    slope = slopes_ref[h]

    q = q_ref[0]
    lane = lax.broadcasted_iota(I32, q.shape, 1)
    zero = jnp.zeros_like(q)
    q1 = jnp.where(lane < HEAD_DIM, q, zero)
    q2 = jnp.where(lane >= HEAD_DIM, q, zero)

    for m, l, a in ((m1, l1, a1), (m2, l2, a2)):
        m[...] = jnp.full(m.shape, NEG, F32)
        l[...] = jnp.zeros(l.shape, F32)
        a[...] = jnp.zeros(a.shape, F32)

    row = lax.broadcasted_iota(I32, (tile, tile), 0)
    col = lax.broadcasted_iota(I32, (tile, tile), 1)

    def body(j, carry):
        start = pl.multiple_of(j * tile, tile)
        k = k_ref[0, pl.ds(start, tile), :]
        v = v_ref[0, pl.ds(start, tile), :]
        rel = (col - row) + (j - i) * tile
        bias = jnp.where(rel <= 0, slope * rel.astype(F32), NEG)
        for qq, m, l, a in ((q1, m1, l1, a1), (q2, m2, l2, a2)):
            s = _dot_nt(qq, k) + bias
            m_old = m[...]
            m_new = jnp.maximum(m_old, jnp.max(s, axis=1, keepdims=True))
            alpha = jnp.exp(m_old - m_new)
            p = jnp.exp(s - m_new)
            l[...] = alpha * l[...] + jnp.sum(p, axis=1, keepdims=True)
            a[...] = alpha * a[...] + _dot(p.astype(BF16), v)
            m[...] = m_new
        return carry

    lax.fori_loop(0, i + 1, body, 0)

    lv = lam_ref[...]
    lam = (jnp.exp(jnp.sum(lv[0:1] * lv[1:2], axis=1, keepdims=True))
           - jnp.exp(jnp.sum(lv[2:3] * lv[3:4], axis=1, keepdims=True)) + lam_init)
    o = a1[...] * (1.0 / l1[...]) - lam * (a2[...] * (1.0 / l2[...]))
    o = _rms(o, g_ref[...]) * (1.0 - lam_init)
    o_ref[0] = o.astype(o_ref.dtype)


def _diff_attention(qd, kd, vd, slopes, lam_qk, subln_g, lam_init):
    b, s, _ = qd.shape
    tile = min(ATT_TILE, s)
    vec = lambda shape: pltpu.VMEM(shape, F32)
    return pl.pallas_call(
        functools.partial(_diff_kernel, lam_init=lam_init, tile=tile),
        grid=(b, N_HEADS, s // tile),
        in_specs=[pl.BlockSpec(memory_space=pltpu.SMEM),
                  pl.BlockSpec(lam_qk.shape, lambda bb, hh, ii: (0, 0)),
                  pl.BlockSpec((1, DIFF_VDIM), lambda bb, hh, ii: (0, 0)),
                  pl.BlockSpec((1, tile, DIFF_VDIM), lambda bb, hh, ii: (bb, ii, hh)),
                  pl.BlockSpec((1, s, DIFF_VDIM), lambda bb, hh, ii: (bb, 0, hh)),
                  pl.BlockSpec((1, s, DIFF_VDIM), lambda bb, hh, ii: (bb, 0, hh))],
        out_specs=pl.BlockSpec((1, tile, DIFF_VDIM), lambda bb, hh, ii: (bb, ii, hh)),
        out_shape=jax.ShapeDtypeStruct((b, s, DIFF_WIDTH), BF16),
        scratch_shapes=[vec((tile, 1)), vec((tile, 1)), vec((tile, DIFF_VDIM)),
                        vec((tile, 1)), vec((tile, 1)), vec((tile, DIFF_VDIM))],
        compiler_params=_cparams("arbitrary", "arbitrary", "arbitrary"),
        name="diff_attn",
    )(slopes, lam_qk, subln_g, qd, kd, vd)


def _head_mask(lane, h):
    return (lane >= h * HEAD_DIM) & (lane < (h + 1) * HEAD_DIM)


def _dil_kernel(slopes_ref, q_ref, kp_ref, kc_ref, vp_ref, vc_ref, o_ref, lse_ref, *, dilation, tile):
    i = pl.program_id(2)
    q = q_ref[0]
    kp, kc, vp, vc = kp_ref[0], kc_ref[0], vp_ref[0], vc_ref[0]
    lane = lax.broadcasted_iota(I32, q.shape, 1)
    row = lax.broadcasted_iota(I32, (tile, tile), 0)
    col = lax.broadcasted_iota(I32, (tile, tile), 1)
    d_cur = row - col
    d_prev = d_cur + tile
    ok_cur = d_cur >= 0
    ok_prev = (d_prev <= DIL_WINDOW_KEYS) & (i > 0)
    d_cur_f = d_cur.astype(F32)
    d_prev_f = d_prev.astype(F32)

    out = jnp.zeros(q.shape, F32)
    lse = jnp.zeros(q.shape, F32)
    for h in range(N_HEADS):
        sl = slopes_ref[h] * float(dilation)
        hm = _head_mask(lane, h)
        qh = jnp.where(hm, q, jnp.zeros_like(q))
        s_c = _dot_nt(qh, kc) + jnp.where(ok_cur, -sl * d_cur_f, NEG)
        s_p = _dot_nt(qh, kp) + jnp.where(ok_prev, -sl * d_prev_f, NEG)
        m = jnp.maximum(jnp.max(s_c, axis=1, keepdims=True), jnp.max(s_p, axis=1, keepdims=True))
        p_c = jnp.exp(s_c - m)
        p_p = jnp.exp(s_p - m)
        l = jnp.sum(p_c, axis=1, keepdims=True) + jnp.sum(p_p, axis=1, keepdims=True)
        acc = _dot(p_c.astype(BF16), vc) + _dot(p_p.astype(BF16), vp)
        out = jnp.where(hm, acc * (1.0 / l), out)
        lse = jnp.where(hm, m + jnp.log(l), lse)
    o_ref[0] = out
    lse_ref[0] = lse


def _dil_branch(ql, kl, vl, slopes, dilation):
    b, s, w = ql.shape
    r = dilation
    sr = s // r
    tile = min(DIL_TILE, sr)
    view = lambda t: t.reshape(b, sr, r * w)
    cur = pl.BlockSpec((1, tile, w), lambda bb, rr, ii: (bb, ii, rr))
    prev = pl.BlockSpec((1, tile, w), lambda bb, rr, ii: (bb, jnp.maximum(ii - 1, 0), rr))
    out = jax.ShapeDtypeStruct((b, sr, r * w), F32)
    o, lse = pl.pallas_call(
        functools.partial(_dil_kernel, dilation=r, tile=tile),
        grid=(b, r, sr // tile),
        in_specs=[pl.BlockSpec(memory_space=pltpu.SMEM), cur, prev, cur, prev, cur],
        out_specs=[cur, cur],
        out_shape=[out, out],
        compiler_params=_cparams("arbitrary", "arbitrary", "arbitrary"),
        name=f"dil_attn_r{r}",
    )(slopes, view(ql), view(kl), view(kl), view(vl), view(vl))
    return o.reshape(b * s, w), lse.reshape(b * s, w)


def _memattn_kernel(q_ref, mk_ref, mv_ref, o_ref):
    q = q_ref[0]
    mk, mv = mk_ref[0], mv_ref[0]
    lane = lax.broadcasted_iota(I32, q.shape, 1)
    out = jnp.zeros(q.shape, F32)
    for h in range(N_HEADS):
        hm = _head_mask(lane, h)
        s = _dot_nt(jnp.where(hm, q, jnp.zeros_like(q)), mk)
        m = jnp.max(s, axis=1, keepdims=True)
        p = jnp.exp(s - m)
        l = jnp.sum(p, axis=1, keepdims=True)
        out = jnp.where(hm, _dot(p.astype(BF16), mv) * (1.0 / l), out)
    o_ref[0] = out.astype(o_ref.dtype)


def _mem_attention(mq, mk, mv):
    b, s, w = mq.shape
    m = mk.shape[1]
    tile = min(TOK_TILE, s)
    return pl.pallas_call(
        _memattn_kernel,
        grid=(b, s // tile),
        in_specs=[pl.BlockSpec((1, tile, w), lambda bb, ii: (bb, ii, 0)),
                  pl.BlockSpec((1, m, w), lambda bb, ii: (bb, 0, 0)),
                  pl.BlockSpec((1, m, w), lambda bb, ii: (bb, 0, 0))],
        out_specs=pl.BlockSpec((1, tile, w), lambda bb, ii: (bb, ii, 0)),
        out_shape=jax.ShapeDtypeStruct((b, s, w), BF16),
        compiler_params=_cparams("arbitrary", "arbitrary"),
        name="mem_attn",
    )(mq, mk, mv)


def _outrouter_kernel(x_ref, od_ref, o1, s1, o2, s2, o3, s3, om_ref, wo_ref, g_ref, wr_ref, br_ref,
                      x2_ref, h2_ref, eidx_ref, rank_ref, gate_ref, cnt_ref, run_ref, *, tm):
    step = pl.program_id(0)

    @pl.when(step == 0)
    def _():
        run_ref[...] = jnp.zeros(run_ref.shape, F32)

    lse = (s1[...], s2[...], s3[...])
    big = jnp.maximum(jnp.maximum(lse[0], lse[1]), lse[2])
    wts = [jnp.exp(z - big) for z in lse]
    o_dil = (wts[0] * o1[...] + wts[1] * o2[...] + wts[2] * o3[...]) * (1.0 / (wts[0] + wts[1] + wts[2]))

    c1, c2 = DIFF_WIDTH, DIFF_WIDTH + GROUP_WIDTH
    mix = (_dot(od_ref[...], wo_ref[0:c1, :]) + _dot(o_dil.astype(BF16), wo_ref[c1:c2, :])
           + _dot(om_ref[...], wo_ref[c2:, :]))
    x2 = x_ref[...] + mix
    x2_ref[...] = x2
    h2 = _rms(x2, g_ref[...])
    h2_ref[...] = h2

    logits = jnp.dot(h2, wr_ref[...], precision=lax.Precision.HIGHEST, preferred_element_type=F32) + br_ref[...]
    lane = lax.broadcasted_iota(I32, logits.shape, 1)
    work = logits
    sel = jnp.zeros(logits.shape, F32)
    vals, idxs = [], []
    for _ in range(TOP_K):
        mx = jnp.max(work, axis=1, keepdims=True)
        idx = jnp.min(jnp.where(work == mx, lane, V7X_LANES), axis=1, keepdims=True)
        hit = lane == idx
        sel = jnp.where(hit, 1.0, sel)
        work = jnp.where(hit, -jnp.inf, work)
        vals.append(mx)
        idxs.append(idx)

    r_io = lax.broadcasted_iota(I32, (tm, tm), 0)
    c_io = lax.broadcasted_iota(I32, (tm, tm), 1)
    tri = jnp.where(c_io < r_io, 1.0, 0.0).astype(BF16)
    before = _dot(tri, sel.astype(BF16)) + run_ref[...]
    run_ref[...] = run_ref[...] + jnp.sum(sel, axis=0, keepdims=True)
    cnt_ref[...] = run_ref[...].astype(I32)

    exps = [jnp.exp(v - vals[0]) for v in vals]
    inv = 1.0 / (exps[0] + exps[1] + exps[2] + exps[3])
    eidx = jnp.zeros(logits.shape, I32)
    rank = jnp.zeros(logits.shape, I32)
    gate = jnp.zeros(logits.shape, F32)
    for k in range(TOP_K):
        rk = jnp.sum(jnp.where(lane == idxs[k], before, 0.0), axis=1, keepdims=True)
        eidx = jnp.where(lane == k, idxs[k], eidx)
        rank = jnp.where(lane == k, rk.astype(I32), rank)
        gate = jnp.where(lane == k, exps[k] * inv, gate)
    eidx_ref[...] = eidx
    rank_ref[...] = rank
    gate_ref[...] = gate


def _out_router(x2d, o_diff, dil, o_mem, w_out, g, w_router_p, b_router_p):
    t, d = x2d.shape
    tm = min(TOK_TILE, t)
    row = lambda n: pl.BlockSpec((tm, n), lambda i: (i, 0))
    full = lambda a: pl.BlockSpec(a.shape, lambda i: (0, 0))
    (o1, s1), (o2, s2), (o3, s3) = dil
    return pl.pallas_call(
        functools.partial(_outrouter_kernel, tm=tm),
        grid=(t // tm,),
        in_specs=[row(d), row(DIFF_WIDTH)] + [row(GROUP_WIDTH)] * 7
                 + [full(w_out), full(g), full(w_router_p), full(b_router_p)],
        out_specs=[row(d), row(d), row(V7X_LANES), row(V7X_LANES), row(V7X_LANES),
                   pl.BlockSpec((1, V7X_LANES), lambda i: (0, 0))],
        out_shape=[jax.ShapeDtypeStruct((t, d), F32), jax.ShapeDtypeStruct((t, d), F32),
                   jax.ShapeDtypeStruct((t, V7X_LANES), I32), jax.ShapeDtypeStruct((t, V7X_LANES), I32),
                   jax.ShapeDtypeStruct((t, V7X_LANES), F32), jax.ShapeDtypeStruct((1, V7X_LANES), I32)],
        scratch_shapes=[pltpu.VMEM((1, V7X_LANES), F32)],
        compiler_params=_cparams("arbitrary"),
        name="out_router",
    )(x2d, o_diff, o1, s1, o2, s2, o3, s3, o_mem, w_out, g, w_router_p, b_router_p)


def _row_copy(src, dst, sem):
    return pltpu.make_async_copy(src, dst, sem)


def _dispatch_kernel(dest_ref, h_ref, xs_hbm, sem, *, tm):
    def issue(t, carry):
        for k in range(TOP_K):
            d = dest_ref[0, 0, t * TOP_K + k]
            _row_copy(h_ref.at[pl.ds(t, 1), :], xs_hbm.at[pl.ds(d, 1), :], sem).start()
        return carry

    lax.fori_loop(0, tm, issue, 0)
    for _ in range(TOP_K):
        _row_copy(h_ref, xs_hbm.at[pl.ds(0, tm), :], sem).wait()


def _dispatch(h2, dest3):
    t, d = h2.shape
    tm = min(DMA_TILE, t)
    return pl.pallas_call(
        functools.partial(_dispatch_kernel, tm=tm),
        grid=(t // tm,),
        in_specs=[pl.BlockSpec((1, 1, tm * TOP_K), lambda i: (i, 0, 0), memory_space=pltpu.SMEM),
                  pl.BlockSpec((tm, d), lambda i: (i, 0))],
        out_specs=pl.BlockSpec(memory_space=pl.ANY),
        out_shape=jax.ShapeDtypeStruct((t * TOP_K, d), h2.dtype),
        scratch_shapes=[pltpu.SemaphoreType.DMA],
        compiler_params=_cparams("arbitrary"),
        name="dispatch",
    )(dest3, h2)


def _experts_kernel(tile_ref, exp_ref, lo_ref, hi_ref, first_ref, act_ref,
                    xs_ref, wg_ref, wu_ref, bg_ref, bu_ref, wd_ref, bd_ref, y_ref, *, tm):
    w = pl.program_id(0)

    @pl.when(act_ref[w] == 1)
    def _():
        x = xs_ref[...].astype(BF16)
        gate = jnp.minimum(_dot(x, wg_ref[0]) + bg_ref[0], SWIGLU_LIMIT)
        up = jnp.clip(_dot(x, wu_ref[0]) + bu_ref[0], -SWIGLU_LIMIT, SWIGLU_LIMIT)
        act = (up + 1.0) * (gate * jax.nn.sigmoid(gate * SWIGLU_ALPHA))
        y = _dot(act.astype(BF16), wd_ref[0]) + bd_ref[0]
        row = lax.broadcasted_iota(I32, y.shape, 0)
        mine = (row >= lo_ref[w]) & (row < hi_ref[w])

        @pl.when(first_ref[w] == 1)
        def _():
            y_ref[...] = jnp.where(mine, y, 0.0)

        @pl.when(first_ref[w] == 0)
        def _():
            y_ref[...] = jnp.where(mine, y, y_ref[...])


def _experts(xs, meta, wg, wu, bg, bu, wd, bd):
    n, d = xs.shape
    e, _, f = wg.shape
    tm = min(ROW_TILE, n)
    n_items = meta[0].shape[0]
    xmap = lambda w, tile, ex, lo, hi, fi, ac: (tile[w], 0)
    wmap = lambda w, tile, ex, lo, hi, fi, ac: (ex[w], 0, 0)
    grid_spec = pltpu.PrefetchScalarGridSpec(
        num_scalar_prefetch=6,
        grid=(n_items,),
        in_specs=[pl.BlockSpec((tm, d), xmap),
                  pl.BlockSpec((1, d, f), wmap), pl.BlockSpec((1, d, f), wmap),
                  pl.BlockSpec((1, 1, f), wmap), pl.BlockSpec((1, 1, f), wmap),
                  pl.BlockSpec((1, f, d), wmap), pl.BlockSpec((1, 1, d), wmap)],
        out_specs=pl.BlockSpec((tm, d), xmap),
    )
    return pl.pallas_call(
        functools.partial(_experts_kernel, tm=tm),
        grid_spec=grid_spec,
        out_shape=jax.ShapeDtypeStruct((n, d), F32),
        compiler_params=_cparams("arbitrary"),
        name="experts",
    )(*meta, xs, wg, wu, bg, bu, wd, bd)


def _expert_work_items(counts, n_rows, tm):
    e = counts.shape[0]
    n_tiles = n_rows // tm
    n_items = n_tiles + e - 1
    ends = jnp.cumsum(counts)
    starts = ends - counts
    first_tile = starts // tm
    n_t = jnp.where(counts > 0, (ends - 1) // tm - first_tile + 1, 0)
    item_end = jnp.cumsum(n_t)
    item_start = item_end - n_t
    total = item_end[-1]
    w = jnp.arange(n_items, dtype=I32)
    active = w < total
    w_eff = jnp.minimum(w, total - 1)
    ex = jnp.minimum(jnp.sum((item_end[None, :] <= w_eff[:, None]).astype(I32), axis=1), e - 1)
    tile = (first_tile[ex] + (w_eff - item_start[ex])).astype(I32)
    lo = jnp.clip(starts[ex] - tile * tm, 0, tm).astype(I32)
    hi = jnp.clip(ends[ex] - tile * tm, 0, tm).astype(I32)
    first = jnp.concatenate([jnp.ones((1,), I32), (tile[1:] != tile[:-1]).astype(I32)])
    return tile, ex, lo, hi, first, active.astype(I32)


def _combine_kernel(dest_ref, x2_ref, gate_ref, g_ref, y_hbm, o_ref, ybuf, sem, *, tm):
    def issue(t, carry):
        for k in range(TOP_K):
            d = dest_ref[0, 0, t * TOP_K + k]
            _row_copy(y_hbm.at[pl.ds(d, 1), :], ybuf.at[k, pl.ds(t, 1), :], sem).start()
        return carry

    lax.fori_loop(0, tm, issue, 0)
    for k in range(TOP_K):
        _row_copy(y_hbm.at[pl.ds(0, tm), :], ybuf.at[k], sem).wait()

    gates = gate_ref[...]
    acc = x2_ref[...]
    for k in range(TOP_K):
        acc = acc + ybuf[k] * gates[:, k:k + 1]
    o_ref[...] = _rms(acc, g_ref[...])


def _combine(y, dest3, x2, gates, g_final):
    t, d = x2.shape
    tm = min(DMA_TILE, t)
    return pl.pallas_call(
        functools.partial(_combine_kernel, tm=tm),
        grid=(t // tm,),
        in_specs=[pl.BlockSpec((1, 1, tm * TOP_K), lambda i: (i, 0, 0), memory_space=pltpu.SMEM),
                  pl.BlockSpec((tm, d), lambda i: (i, 0)),
                  pl.BlockSpec((tm, V7X_LANES), lambda i: (i, 0)),
                  pl.BlockSpec((1, d), lambda i: (0, 0)),
                  pl.BlockSpec(memory_space=pl.ANY)],
        out_specs=pl.BlockSpec((tm, d), lambda i: (i, 0)),
        out_shape=jax.ShapeDtypeStruct((t, d), F32),
        scratch_shapes=[pltpu.VMEM((TOP_K, tm, d), F32), pltpu.SemaphoreType.DMA],
        compiler_params=_cparams("arbitrary"),
        name="combine",
    )(dest3, x2, gates, g_final, y)


def _alibi_slopes():
    n = 2 * N_HEADS
    s = 2.0 ** (-8.0 * (jnp.arange(n, dtype=F32) + 1.0) / n)
    return s[0::2], s[1::2]


def kernel(x, mem, attn_norm_g, mem_norm_g, w_in, diff_lambda_qk, diff_subln_g, w_mem_kv, w_out,
           ffn_norm_g, w_router, b_router, w_gate_up, b_gate_up, w_down, b_down, final_norm_g):
    b, s, d = x.shape
    t = b * s
    assert w_in.shape[0] == 1, "the final norm is fused into the layer's combine step: one layer only"
    assert s % (DIL_DILATIONS[-1] * DIL_TILE) == 0 and t % TOK_TILE == 0
    n_exp = w_router.shape[-1]
    diff_slopes, dil_slopes = _alibi_slopes()
    x2d = x.reshape(t, d)

    for l in range(1):
        lam_init = 0.8 - 0.6 * math.exp(-0.3 * l)
        mk, mv = _mem_kv(mem, mem_norm_g[l][None], w_mem_kv[l].astype(BF16))
        qd, kd, vd, ql, kl, vl, mq = _in_proj(x2d, attn_norm_g[l][None], w_in[l].astype(BF16))
        sh = lambda a: a.reshape(b, s, a.shape[-1])
        o_diff = _diff_attention(sh(qd), sh(kd), sh(vd), diff_slopes, diff_lambda_qk[l],
                                 diff_subln_g[l][None], lam_init)
        dil = [_dil_branch(sh(ql), sh(kl), sh(vl), dil_slopes, r) for r in DIL_DILATIONS]
        o_mem = _mem_attention(sh(mq), mk, mv)

        w_router_p = jnp.zeros((d, V7X_LANES), F32).at[:, :n_exp].set(w_router[l])
        b_router_p = jnp.full((1, V7X_LANES), NEG, F32).at[0, :n_exp].set(b_router[l])
        x2, h2, eidx, rank, gates, cnt = _out_router(
            x2d, o_diff.reshape(t, DIFF_WIDTH), dil, o_mem.reshape(t, GROUP_WIDTH),
            w_out[l].astype(BF16), ffn_norm_g[l][None], w_router_p, b_router_p)

        counts = cnt[0, :n_exp]
        starts = jnp.cumsum(counts) - counts
        e_sel = eidx[:, :TOP_K]
        onehot = e_sel[:, :, None] == jnp.arange(n_exp, dtype=I32)[None, None, :]
        dest = rank[:, :TOP_K] + jnp.sum(jnp.where(onehot, starts[None, None, :], 0), axis=-1)
        tm_dma = min(DMA_TILE, t)
        dest3 = dest.astype(I32).reshape(t // tm_dma, 1, tm_dma * TOP_K)

        xs = _dispatch(h2, dest3)

        f = w_down.shape[2]
        wgu = w_gate_up[l].reshape(n_exp, d, f, 2)
        bgu = b_gate_up[l].reshape(n_exp, 1, f, 2)
        meta = _expert_work_items(counts, t * TOP_K, min(ROW_TILE, t * TOP_K))
        y = _experts(xs, meta, wgu[..., 0].astype(BF16), wgu[..., 1].astype(BF16),
                     bgu[..., 0], bgu[..., 1], w_down[l].astype(BF16), b_down[l][:, None, :])

        out = _combine(y, dest3, x2, gates, final_norm_g[None])
    return out.reshape(b, s, d)
```

```python
import functools
import math

import jax
import jax.numpy as jnp
from jax import lax
from jax.experimental import pallas as pl
from jax.experimental.pallas import tpu as pltpu

F32 = jnp.float32
BF16 = jnp.bfloat16
I32 = jnp.int32

NORM_EPS = 1e-5
HEAD_DIM = 64
N_HEADS = 4
DIFF_VDIM = 2 * HEAD_DIM
DIFF_WIDTH = N_HEADS * DIFF_VDIM
GROUP_WIDTH = N_HEADS * HEAD_DIM
DIL_DILATIONS = (1, 4, 16)
DIL_WINDOW_KEYS = 128
TOP_K = 4
SWIGLU_ALPHA = 1.702
SWIGLU_LIMIT = 7.0
NEG = -1e30

V7X_LANES = 128
V7X_VMEM_LIMIT = 56 * 1024 * 1024

TOK_TILE = 512
ATT_TILE = 512
POS_SPLIT = 64
DIL_TILE = DIL_WINDOW_KEYS
ROW_TILE = 512
DMA_TILE = 256


def _cparams(*sem):
    return pltpu.CompilerParams(dimension_semantics=tuple(sem), vmem_limit_bytes=V7X_VMEM_LIMIT)


def _rms(x, g):
    return x * lax.rsqrt(jnp.mean(x * x, axis=-1, keepdims=True) + NORM_EPS) * g


def _dot_nt(a, b):
    return lax.dot_general(a, b, (((1,), (1,)), ((), ())), preferred_element_type=F32)


def _dot(a, b):
    return jnp.dot(a, b, preferred_element_type=F32)


def _memkv_kernel(mem_ref, g_ref, w_ref, mk_ref, mv_ref):
    h = _rms(mem_ref[0], g_ref[...]).astype(BF16)
    kv = _dot(h, w_ref[...])
    mk_ref[0] = kv[:, :GROUP_WIDTH].astype(BF16)
    mv_ref[0] = kv[:, GROUP_WIDTH:].astype(BF16)


def _mem_kv(mem, g, w):
    b, m, d = mem.shape
    out = jax.ShapeDtypeStruct((b, m, GROUP_WIDTH), BF16)
    return pl.pallas_call(
        _memkv_kernel,
        grid=(b,),
        in_specs=[pl.BlockSpec((1, m, d), lambda i: (i, 0, 0)),
                  pl.BlockSpec((1, d), lambda i: (0, 0)),
                  pl.BlockSpec((d, 2 * GROUP_WIDTH), lambda i: (0, 0))],
        out_specs=[pl.BlockSpec((1, m, GROUP_WIDTH), lambda i: (i, 0, 0))] * 2,
        out_shape=[out, out],
        compiler_params=_cparams("arbitrary"),
        name="mem_kv",
    )(mem, g, w)


_IN_SEGMENTS = ((DIFF_WIDTH, HEAD_DIM ** -0.5), (DIFF_WIDTH, 1.0), (DIFF_WIDTH, 1.0),
                (GROUP_WIDTH, HEAD_DIM ** -0.5), (GROUP_WIDTH, 1.0), (GROUP_WIDTH, 1.0),
                (GROUP_WIDTH, HEAD_DIM ** -0.5))


def _inproj_kernel(x_ref, g_ref, w_ref, *out_refs):
    h = _rms(x_ref[...], g_ref[...]).astype(BF16)
    c0 = 0
    for ref, (n, scale) in zip(out_refs, _IN_SEGMENTS):
        p = _dot(h, w_ref[:, c0:c0 + n])
        if scale != 1.0:
            p = p * scale
        ref[...] = p.astype(BF16)
        c0 += n


def _in_proj(x2d, g, w):
    t, d = x2d.shape
    tm = min(TOK_TILE, t)
    return pl.pallas_call(
        _inproj_kernel,
        grid=(t // tm,),
        in_specs=[pl.BlockSpec((tm, d), lambda i: (i, 0)),
                  pl.BlockSpec((1, d), lambda i: (0, 0)),
                  pl.BlockSpec(w.shape, lambda i: (0, 0))],
        out_specs=[pl.BlockSpec((tm, n), lambda i: (i, 0)) for n, _ in _IN_SEGMENTS],
        out_shape=[jax.ShapeDtypeStruct((t, n), BF16) for n, _ in _IN_SEGMENTS],
        compiler_params=_cparams("arbitrary"),
        name="in_proj",
    )(x2d, g, w)


def _diff_kernel(slopes_ref, lam_ref, g_ref, q_ref, k_ref, v_ref, kpos_ref, ones_ref, o_ref,
                 m1, a1, m2, a2, *, lam_init, tile):
    h = pl.program_id(1)
    i = pl.program_id(2)
    slope = slopes_ref[h]

    q = q_ref[0]
    lane = lax.broadcasted_iota(I32, q.shape, 1)
    zero = jnp.zeros_like(q)
    qpos = jnp.where(lane == 0, slope * float(POS_SPLIT), jnp.where(lane == 1, slope, 0.0)).astype(BF16)
    q1 = jnp.concatenate([jnp.where(lane < HEAD_DIM, q, zero), qpos], axis=1)
    q2 = jnp.concatenate([jnp.where(lane >= HEAD_DIM, q, zero), qpos], axis=1)

    for m, a in ((m1, a1), (m2, a2)):
        m[...] = jnp.full(m.shape, NEG, F32)
        a[...] = jnp.zeros(a.shape, F32)

    def step(j, diagonal):
        start = pl.multiple_of(j * tile, tile)
        kk = jnp.concatenate([k_ref[0, pl.ds(start, tile), :], kpos_ref[pl.ds(start, tile), :]], axis=1)
        vv = jnp.concatenate([v_ref[0, pl.ds(start, tile), :], ones_ref[...]], axis=1)
        for qq, m, a in ((q1, m1, a1), (q2, m2, a2)):
            s = _dot_nt(qq, kk)
            if diagonal:
                row = lax.broadcasted_iota(I32, s.shape, 0)
                col = lax.broadcasted_iota(I32, s.shape, 1)
                s = jnp.where(col <= row, s, NEG)
            m_old = m[...]
            m_new = jnp.maximum(m_old, jnp.max(s, axis=1, keepdims=True))
            p = jnp.exp(s - m_new)
            a[...] = jnp.exp(m_old - m_new) * a[...] + _dot(p.astype(BF16), vv)
            m[...] = m_new

    def body(j, carry):
        step(j, False)
        return carry

    lax.fori_loop(0, i, body, 0)
    step(i, True)

    lv = lam_ref[...]
    lam = (jnp.exp(jnp.sum(lv[0:1] * lv[1:2], axis=1, keepdims=True))
           - jnp.exp(jnp.sum(lv[2:3] * lv[3:4], axis=1, keepdims=True)) + lam_init)
    n = DIFF_VDIM
    o = (a1[:, :n] * (1.0 / a1[:, n:n + 1])) - lam * (a2[:, :n] * (1.0 / a2[:, n:n + 1]))
    o = _rms(o, g_ref[...]) * (1.0 - lam_init)
    o_ref[0] = o.astype(o_ref.dtype)


def _diff_attention(qd, kd, vd, slopes, lam_qk, subln_g, lam_init):
    b, s, _ = qd.shape
    tile = min(ATT_TILE, s)
    pos = jnp.arange(s, dtype=I32)[:, None]
    lane = jnp.arange(V7X_LANES, dtype=I32)[None, :]
    kpos = jnp.where(lane == 0, pos // POS_SPLIT, jnp.where(lane == 1, pos % POS_SPLIT, 0)).astype(BF16)
    ones = jnp.broadcast_to(lane == 0, (tile, V7X_LANES)).astype(BF16)
    vec = lambda shape: pltpu.VMEM(shape, F32)
    const = lambda a: pl.BlockSpec(a.shape, lambda bb, hh, ii: (0, 0))
    return pl.pallas_call(
        functools.partial(_diff_kernel, lam_init=lam_init, tile=tile),
        grid=(b, N_HEADS, s // tile),
        in_specs=[pl.BlockSpec(memory_space=pltpu.SMEM), const(lam_qk), const(subln_g),
                  pl.BlockSpec((1, tile, DIFF_VDIM), lambda bb, hh, ii: (bb, ii, hh)),
                  pl.BlockSpec((1, s, DIFF_VDIM), lambda bb, hh, ii: (bb, 0, hh)),
                  pl.BlockSpec((1, s, DIFF_VDIM), lambda bb, hh, ii: (bb, 0, hh)),
                  const(kpos), const(ones)],
        out_specs=pl.BlockSpec((1, tile, DIFF_VDIM), lambda bb, hh, ii: (bb, ii, hh)),
        out_shape=jax.ShapeDtypeStruct((b, s, DIFF_WIDTH), BF16),
        scratch_shapes=[vec((tile, 1)), vec((tile, 2 * DIFF_VDIM)), vec((tile, 1)), vec((tile, 2 * DIFF_VDIM))],
        compiler_params=_cparams("arbitrary", "arbitrary", "arbitrary"),
        name="diff_attn",
    )(slopes, lam_qk, subln_g, qd, kd, vd, kpos, ones)


def _head_mask(lane, h):
    return (lane >= h * HEAD_DIM) & (lane < (h + 1) * HEAD_DIM)


def _dil_kernel(slopes_ref, q_ref, kp_ref, kc_ref, vp_ref, vc_ref, o_ref, lse_ref, *, dilation, tile):
    i = pl.program_id(2)
    q = q_ref[0]
    kp, kc, vp, vc = kp_ref[0], kc_ref[0], vp_ref[0], vc_ref[0]
    lane = lax.broadcasted_iota(I32, q.shape, 1)
    row = lax.broadcasted_iota(I32, (tile, tile), 0)
    col = lax.broadcasted_iota(I32, (tile, tile), 1)
    d_cur = row - col
    d_prev = d_cur + tile
    ok_cur = d_cur >= 0
    ok_prev = (d_prev <= DIL_WINDOW_KEYS) & (i > 0)
    d_cur_f = d_cur.astype(F32)
    d_prev_f = d_prev.astype(F32)

    out = jnp.zeros(q.shape, F32)
    lse = jnp.zeros(q.shape, F32)
    for h in range(N_HEADS):
        sl = slopes_ref[h] * float(dilation)
        hm = _head_mask(lane, h)
        qh = jnp.where(hm, q, jnp.zeros_like(q))
        s_c = _dot_nt(qh, kc) + jnp.where(ok_cur, -sl * d_cur_f, NEG)
        s_p = _dot_nt(qh, kp) + jnp.where(ok_prev, -sl * d_prev_f, NEG)
        m = jnp.maximum(jnp.max(s_c, axis=1, keepdims=True), jnp.max(s_p, axis=1, keepdims=True))
        p_c = jnp.exp(s_c - m)
        p_p = jnp.exp(s_p - m)
        l = jnp.sum(p_c, axis=1, keepdims=True) + jnp.sum(p_p, axis=1, keepdims=True)
        acc = _dot(p_c.astype(BF16), vc) + _dot(p_p.astype(BF16), vp)
        out = jnp.where(hm, acc * (1.0 / l), out)
        lse = jnp.where(hm, m + jnp.log(l), lse)
    o_ref[0] = out
    lse_ref[0] = lse


def _dil_branch(ql, kl, vl, slopes, dilation):
    b, s, w = ql.shape
    r = dilation
    sr = s // r
    tile = min(DIL_TILE, sr)
    view = lambda t: t.reshape(b, sr, r * w)
    cur = pl.BlockSpec((1, tile, w), lambda bb, rr, ii: (bb, ii, rr))
    prev = pl.BlockSpec((1, tile, w), lambda bb, rr, ii: (bb, jnp.maximum(ii - 1, 0), rr))
    out = jax.ShapeDtypeStruct((b, sr, r * w), F32)
    o, lse = pl.pallas_call(
        functools.partial(_dil_kernel, dilation=r, tile=tile),
        grid=(b, r, sr // tile),
        in_specs=[pl.BlockSpec(memory_space=pltpu.SMEM), cur, prev, cur, prev, cur],
        out_specs=[cur, cur],
        out_shape=[out, out],
        compiler_params=_cparams("arbitrary", "arbitrary", "arbitrary"),
        name=f"dil_attn_r{r}",
    )(slopes, view(ql), view(kl), view(kl), view(vl), view(vl))
    return o.reshape(b * s, w), lse.reshape(b * s, w)


def _memattn_kernel(q_ref, mk_ref, mv_ref, o_ref):
    q = q_ref[0]
    mk, mv = mk_ref[0], mv_ref[0]
    lane = lax.broadcasted_iota(I32, q.shape, 1)
    out = jnp.zeros(q.shape, F32)
    for h in range(N_HEADS):
        hm = _head_mask(lane, h)
        s = _dot_nt(jnp.where(hm, q, jnp.zeros_like(q)), mk)
        m = jnp.max(s, axis=1, keepdims=True)
        p = jnp.exp(s - m)
        l = jnp.sum(p, axis=1, keepdims=True)
        out = jnp.where(hm, _dot(p.astype(BF16), mv) * (1.0 / l), out)
    o_ref[0] = out.astype(o_ref.dtype)


def _mem_attention(mq, mk, mv):
    b, s, w = mq.shape
    m = mk.shape[1]
    tile = min(TOK_TILE, s)
    return pl.pallas_call(
        _memattn_kernel,
        grid=(b, s // tile),
        in_specs=[pl.BlockSpec((1, tile, w), lambda bb, ii: (bb, ii, 0)),
                  pl.BlockSpec((1, m, w), lambda bb, ii: (bb, 0, 0)),
                  pl.BlockSpec((1, m, w), lambda bb, ii: (bb, 0, 0))],
        out_specs=pl.BlockSpec((1, tile, w), lambda bb, ii: (bb, ii, 0)),
        out_shape=jax.ShapeDtypeStruct((b, s, w), BF16),
        compiler_params=_cparams("arbitrary", "arbitrary"),
        name="mem_attn",
    )(mq, mk, mv)


def _outrouter_kernel(x_ref, od_ref, o1, s1, o2, s2, o3, s3, om_ref, wo_ref, g_ref, wr_ref, br_ref,
                      x2_ref, h2_ref, eidx_ref, rank_ref, gate_ref, cnt_ref, run_ref, *, tm):
    step = pl.program_id(0)

    @pl.when(step == 0)
    def _():
        run_ref[...] = jnp.zeros(run_ref.shape, F32)

    lse = (s1[...], s2[...], s3[...])
    big = jnp.maximum(jnp.maximum(lse[0], lse[1]), lse[2])
    wts = [jnp.exp(z - big) for z in lse]
    o_dil = (wts[0] * o1[...] + wts[1] * o2[...] + wts[2] * o3[...]) * (1.0 / (wts[0] + wts[1] + wts[2]))

    c1, c2 = DIFF_WIDTH, DIFF_WIDTH + GROUP_WIDTH
    mix = (_dot(od_ref[...], wo_ref[0:c1, :]) + _dot(o_dil.astype(BF16), wo_ref[c1:c2, :])
           + _dot(om_ref[...], wo_ref[c2:, :]))
    x2 = x_ref[...] + mix
    x2_ref[...] = x2
    h2 = _rms(x2, g_ref[...])
    h2_ref[...] = h2

    h_hi = h2.astype(BF16)
    h_lo = (h2 - h_hi.astype(F32)).astype(BF16)
    logits = (_dot(h_hi, wr_ref[0]) + (_dot(h_hi, wr_ref[1]) + _dot(h_lo, wr_ref[0]))) + br_ref[...]
    lane = lax.broadcasted_iota(I32, logits.shape, 1)
    work = logits
    sel = jnp.zeros(logits.shape, F32)
    vals, idxs = [], []
    for _ in range(TOP_K):
        mx = jnp.max(work, axis=1, keepdims=True)
        idx = jnp.min(jnp.where(work == mx, lane, V7X_LANES), axis=1, keepdims=True)
        hit = lane == idx
        sel = jnp.where(hit, 1.0, sel)
        work = jnp.where(hit, -jnp.inf, work)
        vals.append(mx)
        idxs.append(idx)

    r_io = lax.broadcasted_iota(I32, (tm, tm), 0)
    c_io = lax.broadcasted_iota(I32, (tm, tm), 1)
    tri = jnp.where(c_io < r_io, 1.0, 0.0).astype(BF16)
    before = _dot(tri, sel.astype(BF16)) + run_ref[...]
    run_ref[...] = run_ref[...] + jnp.sum(sel, axis=0, keepdims=True)
    cnt_ref[...] = run_ref[...].astype(I32)

    exps = [jnp.exp(v - vals[0]) for v in vals]
    inv = 1.0 / (exps[0] + exps[1] + exps[2] + exps[3])
    eidx = jnp.zeros(logits.shape, I32)
    rank = jnp.zeros(logits.shape, I32)
    gate = jnp.zeros(logits.shape, F32)
    for k in range(TOP_K):
        rk = jnp.sum(jnp.where(lane == idxs[k], before, 0.0), axis=1, keepdims=True)
        eidx = jnp.where(lane == k, idxs[k], eidx)
        rank = jnp.where(lane == k, rk.astype(I32), rank)
        gate = jnp.where(lane == k, exps[k] * inv, gate)
    eidx_ref[...] = eidx
    rank_ref[...] = rank
    gate_ref[...] = gate


def _out_router(x2d, o_diff, dil, o_mem, w_out, g, w_router_p, b_router_p):
    t, d = x2d.shape
    tm = min(TOK_TILE, t)
    row = lambda n: pl.BlockSpec((tm, n), lambda i: (i, 0))
    full = lambda a: pl.BlockSpec(a.shape, lambda i: (0,) * a.ndim)
    (o1, s1), (o2, s2), (o3, s3) = dil
    return pl.pallas_call(
        functools.partial(_outrouter_kernel, tm=tm),
        grid=(t // tm,),
        in_specs=[row(d), row(DIFF_WIDTH)] + [row(GROUP_WIDTH)] * 7
                 + [full(w_out), full(g), full(w_router_p), full(b_router_p)],
        out_specs=[row(d), row(d), row(V7X_LANES), row(V7X_LANES), row(V7X_LANES),
                   pl.BlockSpec((1, V7X_LANES), lambda i: (0, 0))],
        out_shape=[jax.ShapeDtypeStruct((t, d), F32), jax.ShapeDtypeStruct((t, d), F32),
                   jax.ShapeDtypeStruct((t, V7X_LANES), I32), jax.ShapeDtypeStruct((t, V7X_LANES), I32),
                   jax.ShapeDtypeStruct((t, V7X_LANES), F32), jax.ShapeDtypeStruct((1, V7X_LANES), I32)],
        scratch_shapes=[pltpu.VMEM((1, V7X_LANES), F32)],
        compiler_params=_cparams("arbitrary"),
        name="out_router",
    )(x2d, o_diff, o1, s1, o2, s2, o3, s3, o_mem, w_out, g, w_router_p, b_router_p)


def _row_copy(src, dst, sem):
    return pltpu.make_async_copy(src, dst, sem)


def _dispatch_kernel(dest_ref, h_ref, xs_hbm, sem, *, tm):
    def issue(t, carry):
        for k in range(TOP_K):
            d = dest_ref[0, 0, t * TOP_K + k]
            _row_copy(h_ref.at[pl.ds(t, 1), :], xs_hbm.at[pl.ds(d, 1), :], sem).start(priority=k % 2)
        return carry

    lax.fori_loop(0, tm, issue, 0)
    for _ in range(TOP_K):
        _row_copy(h_ref, xs_hbm.at[pl.ds(0, tm), :], sem).wait()


def _dispatch(h2, dest3):
    t, d = h2.shape
    tm = min(DMA_TILE, t)
    return pl.pallas_call(
        functools.partial(_dispatch_kernel, tm=tm),
        grid=(t // tm,),
        in_specs=[pl.BlockSpec((1, 1, tm * TOP_K), lambda i: (i, 0, 0), memory_space=pltpu.SMEM),
                  pl.BlockSpec((tm, d), lambda i: (i, 0))],
        out_specs=pl.BlockSpec(memory_space=pl.ANY),
        out_shape=jax.ShapeDtypeStruct((t * TOP_K, d), h2.dtype),
        scratch_shapes=[pltpu.SemaphoreType.DMA],
        compiler_params=_cparams("arbitrary"),
        name="dispatch",
    )(dest3, h2)


def _experts_kernel(tile_ref, exp_ref, lo_ref, hi_ref, first_ref, act_ref,
                    xs_ref, wg_ref, wu_ref, bg_ref, bu_ref, wd_ref, bd_ref, y_ref, *, tm):
    w = pl.program_id(0)

    @pl.when(act_ref[w] == 1)
    def _():
        x = xs_ref[...].astype(BF16)
        gate = jnp.minimum(_dot(x, wg_ref[0]) + bg_ref[0], SWIGLU_LIMIT)
        up = jnp.clip(_dot(x, wu_ref[0]) + bu_ref[0], -SWIGLU_LIMIT, SWIGLU_LIMIT)
        act = (up + 1.0) * (gate * jax.nn.sigmoid(gate * SWIGLU_ALPHA))
        y = _dot(act.astype(BF16), wd_ref[0]) + bd_ref[0]
        row = lax.broadcasted_iota(I32, y.shape, 0)
        mine = (row >= lo_ref[w]) & (row < hi_ref[w])

        @pl.when(first_ref[w] == 1)
        def _():
            y_ref[...] = jnp.where(mine, y, 0.0)

        @pl.when(first_ref[w] == 0)
        def _():
            y_ref[...] = jnp.where(mine, y, y_ref[...])


def _experts(xs, meta, wg, wu, bg, bu, wd, bd):
    n, d = xs.shape
    e, _, f = wg.shape
    tm = min(ROW_TILE, n)
    n_items = meta[0].shape[0]
    xmap = lambda w, tile, ex, lo, hi, fi, ac: (tile[w], 0)
    wmap = lambda w, tile, ex, lo, hi, fi, ac: (ex[w], 0, 0)
    grid_spec = pltpu.PrefetchScalarGridSpec(
        num_scalar_prefetch=6,
        grid=(n_items,),
        in_specs=[pl.BlockSpec((tm, d), xmap),
                  pl.BlockSpec((1, d, f), wmap), pl.BlockSpec((1, d, f), wmap),
                  pl.BlockSpec((1, 1, f), wmap), pl.BlockSpec((1, 1, f), wmap),
                  pl.BlockSpec((1, f, d), wmap), pl.BlockSpec((1, 1, d), wmap)],
        out_specs=pl.BlockSpec((tm, d), xmap),
    )
    return pl.pallas_call(
        functools.partial(_experts_kernel, tm=tm),
        grid_spec=grid_spec,
        out_shape=jax.ShapeDtypeStruct((n, d), F32),
        compiler_params=_cparams("arbitrary"),
        name="experts",
    )(*meta, xs, wg, wu, bg, bu, wd, bd)


def _expert_work_items(counts, n_rows, tm):
    e = counts.shape[0]
    n_tiles = n_rows // tm
    n_items = n_tiles + e - 1
    ends = jnp.cumsum(counts)
    starts = ends - counts
    first_tile = starts // tm
    n_t = jnp.where(counts > 0, (ends - 1) // tm - first_tile + 1, 0)
    item_end = jnp.cumsum(n_t)
    item_start = item_end - n_t
    total = item_end[-1]
    w = jnp.arange(n_items, dtype=I32)
    active = w < total
    w_eff = jnp.minimum(w, total - 1)
    ex = jnp.minimum(jnp.sum((item_end[None, :] <= w_eff[:, None]).astype(I32), axis=1), e - 1)
    tile = (first_tile[ex] + (w_eff - item_start[ex])).astype(I32)
    lo = jnp.clip(starts[ex] - tile * tm, 0, tm).astype(I32)
    hi = jnp.clip(ends[ex] - tile * tm, 0, tm).astype(I32)
    first = jnp.concatenate([jnp.ones((1,), I32), (tile[1:] != tile[:-1]).astype(I32)])
    return tile, ex, lo, hi, first, active.astype(I32)


def _combine_kernel(dest_ref, x2_ref, gate_ref, g_ref, y_hbm, o_ref, ybuf, sem, *, tm):
    def issue(t, carry):
        for k in range(TOP_K):
            d = dest_ref[0, 0, t * TOP_K + k]
            _row_copy(y_hbm.at[pl.ds(d, 1), :], ybuf.at[k, pl.ds(t, 1), :], sem).start(priority=k % 2)
        return carry

    lax.fori_loop(0, tm, issue, 0)
    for k in range(TOP_K):
        _row_copy(y_hbm.at[pl.ds(0, tm), :], ybuf.at[k], sem).wait()

    gates = gate_ref[...]
    acc = x2_ref[...]
    for k in range(TOP_K):
        acc = acc + ybuf[k] * gates[:, k:k + 1]
    o_ref[...] = _rms(acc, g_ref[...])


def _combine(y, dest3, x2, gates, g_final):
    t, d = x2.shape
    tm = min(DMA_TILE, t)
    return pl.pallas_call(
        functools.partial(_combine_kernel, tm=tm),
        grid=(t // tm,),
        in_specs=[pl.BlockSpec((1, 1, tm * TOP_K), lambda i: (i, 0, 0), memory_space=pltpu.SMEM),
                  pl.BlockSpec((tm, d), lambda i: (i, 0)),
                  pl.BlockSpec((tm, V7X_LANES), lambda i: (i, 0)),
                  pl.BlockSpec((1, d), lambda i: (0, 0)),
                  pl.BlockSpec(memory_space=pl.ANY)],
        out_specs=pl.BlockSpec((tm, d), lambda i: (i, 0)),
        out_shape=jax.ShapeDtypeStruct((t, d), F32),
        scratch_shapes=[pltpu.VMEM((TOP_K, tm, d), F32), pltpu.SemaphoreType.DMA],
        compiler_params=_cparams("arbitrary"),
        name="combine",
    )(dest3, x2, gates, g_final, y)


def _alibi_slopes():
    n = 2 * N_HEADS
    s = 2.0 ** (-8.0 * (jnp.arange(n, dtype=F32) + 1.0) / n)
    return s[0::2], s[1::2]


def kernel(x, mem, attn_norm_g, mem_norm_g, w_in, diff_lambda_qk, diff_subln_g, w_mem_kv, w_out,
           ffn_norm_g, w_router, b_router, w_gate_up, b_gate_up, w_down, b_down, final_norm_g):
    b, s, d = x.shape
    t = b * s
    assert w_in.shape[0] == 1, "the final norm is fused into the layer's combine step: one layer only"
    assert s % (DIL_DILATIONS[-1] * DIL_TILE) == 0 and t % TOK_TILE == 0
    n_exp = w_router.shape[-1]
    diff_slopes, dil_slopes = _alibi_slopes()
    x2d = x.reshape(t, d)

    for l in range(1):
        lam_init = 0.8 - 0.6 * math.exp(-0.3 * l)
        mk, mv = _mem_kv(mem, mem_norm_g[l][None], w_mem_kv[l].astype(BF16))
        qd, kd, vd, ql, kl, vl, mq = _in_proj(x2d, attn_norm_g[l][None], w_in[l].astype(BF16))
        sh = lambda a: a.reshape(b, s, a.shape[-1])
        o_diff = _diff_attention(sh(qd), sh(kd), sh(vd), diff_slopes, diff_lambda_qk[l],
                                 diff_subln_g[l][None], lam_init)
        dil = [_dil_branch(sh(ql), sh(kl), sh(vl), dil_slopes, r) for r in DIL_DILATIONS]
        o_mem = _mem_attention(sh(mq), mk, mv)

        w_router_f = jnp.zeros((d, V7X_LANES), F32).at[:, :n_exp].set(w_router[l])
        w_router_hi = w_router_f.astype(BF16)
        w_router_p = jnp.stack([w_router_hi, (w_router_f - w_router_hi.astype(F32)).astype(BF16)])
        b_router_p = jnp.full((1, V7X_LANES), NEG, F32).at[0, :n_exp].set(b_router[l])
        x2, h2, eidx, rank, gates, cnt = _out_router(
            x2d, o_diff.reshape(t, DIFF_WIDTH), dil, o_mem.reshape(t, GROUP_WIDTH),
            w_out[l].astype(BF16), ffn_norm_g[l][None], w_router_p, b_router_p)

        counts = cnt[0, :n_exp]
        starts = jnp.cumsum(counts) - counts
        e_sel = eidx[:, :TOP_K]
        onehot = e_sel[:, :, None] == jnp.arange(n_exp, dtype=I32)[None, None, :]
        dest = rank[:, :TOP_K] + jnp.sum(jnp.where(onehot, starts[None, None, :], 0), axis=-1)
        tm_dma = min(DMA_TILE, t)
        dest3 = dest.astype(I32).reshape(t // tm_dma, 1, tm_dma * TOP_K)

        xs = _dispatch(h2, dest3)

        f = w_down.shape[2]
        wgu = w_gate_up[l].reshape(n_exp, d, f, 2)
        bgu = b_gate_up[l].reshape(n_exp, 1, f, 2)
        meta = _expert_work_items(counts, t * TOP_K, min(ROW_TILE, t * TOP_K))
        y = _experts(xs, meta, wgu[..., 0].astype(BF16), wgu[..., 1].astype(BF16),
                     bgu[..., 0], bgu[..., 1], w_down[l].astype(BF16), b_down[l][:, None, :])

        out = _combine(y, dest3, x2, gates, final_norm_g[None])
    return out.reshape(b, s, d)
```

```python
import functools
import math

import jax
import jax.numpy as jnp
from jax import lax
from jax.experimental import pallas as pl
from jax.experimental.pallas import tpu as pltpu

F32 = jnp.float32
BF16 = jnp.bfloat16
I32 = jnp.int32

NORM_EPS = 1e-5
HEAD_DIM = 64
N_HEADS = 4
DIFF_VDIM = 2 * HEAD_DIM
DIFF_WIDTH = N_HEADS * DIFF_VDIM
GROUP_WIDTH = N_HEADS * HEAD_DIM
DIL_DILATIONS = (1, 4, 16)
DIL_WINDOW_KEYS = 128
TOP_K = 4
SWIGLU_ALPHA = 1.702
SWIGLU_LIMIT = 7.0
NEG = -1e30

V7X_LANES = 128
V7X_VMEM_LIMIT = 56 * 1024 * 1024

TOK_TILE = 512
ATT_TILE = 512
POS_SPLIT = 64
DIL_TILE = DIL_WINDOW_KEYS
ROW_TILE = 512
DMA_TILE = 256
DMA_UNROLL = 4


def _cparams(*sem):
    return pltpu.CompilerParams(dimension_semantics=tuple(sem), vmem_limit_bytes=V7X_VMEM_LIMIT)


def _rms(x, g):
    return x * lax.rsqrt(jnp.mean(x * x, axis=-1, keepdims=True) + NORM_EPS) * g


def _dot_nt(a, b):
    return lax.dot_general(a, b, (((1,), (1,)), ((), ())), preferred_element_type=F32)


def _dot(a, b):
    return jnp.dot(a, b, preferred_element_type=F32)


def _memkv_kernel(mem_ref, g_ref, w_ref, mk_ref, mv_ref):
    h = _rms(mem_ref[0], g_ref[...]).astype(BF16)
    kv = _dot(h, w_ref[...])
    mk_ref[0] = kv[:, :GROUP_WIDTH].astype(BF16)
    mv_ref[0] = kv[:, GROUP_WIDTH:].astype(BF16)


def _mem_kv(mem, g, w):
    b, m, d = mem.shape
    out = jax.ShapeDtypeStruct((b, m, GROUP_WIDTH), BF16)
    return pl.pallas_call(
        _memkv_kernel,
        grid=(b,),
        in_specs=[pl.BlockSpec((1, m, d), lambda i: (i, 0, 0)),
                  pl.BlockSpec((1, d), lambda i: (0, 0)),
                  pl.BlockSpec((d, 2 * GROUP_WIDTH), lambda i: (0, 0))],
        out_specs=[pl.BlockSpec((1, m, GROUP_WIDTH), lambda i: (i, 0, 0))] * 2,
        out_shape=[out, out],
        compiler_params=_cparams("arbitrary"),
        name="mem_kv",
    )(mem, g, w)


_IN_SEGMENTS = ((DIFF_WIDTH, HEAD_DIM ** -0.5), (DIFF_WIDTH, 1.0), (DIFF_WIDTH, 1.0),
                (GROUP_WIDTH, HEAD_DIM ** -0.5), (GROUP_WIDTH, 1.0), (GROUP_WIDTH, 1.0),
                (GROUP_WIDTH, HEAD_DIM ** -0.5))


def _inproj_kernel(x_ref, g_ref, w_ref, *out_refs):
    h = _rms(x_ref[...], g_ref[...]).astype(BF16)
    c0 = 0
    for ref, (n, scale) in zip(out_refs, _IN_SEGMENTS):
        p = _dot(h, w_ref[:, c0:c0 + n])
        if scale != 1.0:
            p = p * scale
        ref[...] = p.astype(BF16)
        c0 += n


def _in_proj(x2d, g, w):
    t, d = x2d.shape
    tm = min(TOK_TILE, t)
    return pl.pallas_call(
        _inproj_kernel,
        grid=(t // tm,),
        in_specs=[pl.BlockSpec((tm, d), lambda i: (i, 0)),
                  pl.BlockSpec((1, d), lambda i: (0, 0)),
                  pl.BlockSpec(w.shape, lambda i: (0, 0))],
        out_specs=[pl.BlockSpec((tm, n), lambda i: (i, 0)) for n, _ in _IN_SEGMENTS],
        out_shape=[jax.ShapeDtypeStruct((t, n), BF16) for n, _ in _IN_SEGMENTS],
        compiler_params=_cparams("arbitrary"),
        name="in_proj",
    )(x2d, g, w)


def _diff_kernel(slopes_ref, lam_ref, g_ref, q_ref, k_ref, v_ref, kpos_ref, ones_ref, o_ref,
                 m1, a1, s1, m2, a2, s2, *, lam_init, tile):
    h = pl.program_id(1)
    i = pl.program_id(2)
    slope = slopes_ref[h]

    q = q_ref[0]
    lane = lax.broadcasted_iota(I32, q.shape, 1)
    zero = jnp.zeros_like(q)
    qpos = jnp.where(lane == 0, slope * float(POS_SPLIT), jnp.where(lane == 1, slope, 0.0)).astype(BF16)
    q1 = jnp.concatenate([jnp.where(lane < HEAD_DIM, q, zero), qpos], axis=1)
    q2 = jnp.concatenate([jnp.where(lane >= HEAD_DIM, q, zero), qpos], axis=1)

    for m, a in ((m1, a1), (m2, a2)):
        m[...] = jnp.full(m.shape, NEG, F32)
        a[...] = jnp.zeros(a.shape, F32)

    def scores(j, slot):
        start = pl.multiple_of(j * tile, tile)
        kk = jnp.concatenate([k_ref[0, pl.ds(start, tile), :], kpos_ref[pl.ds(start, tile), :]], axis=1)
        s1[slot] = _dot_nt(q1, kk)
        s2[slot] = _dot_nt(q2, kk)

    def consume(j, slot, diagonal):
        start = pl.multiple_of(j * tile, tile)
        vv = jnp.concatenate([v_ref[0, pl.ds(start, tile), :], ones_ref[...]], axis=1)
        for sbuf, m, a in ((s1, m1, a1), (s2, m2, a2)):
            s = sbuf[slot]
            if diagonal:
                row = lax.broadcasted_iota(I32, s.shape, 0)
                col = lax.broadcasted_iota(I32, s.shape, 1)
                s = jnp.where(col <= row, s, NEG)
            m_old = m[...]
            m_new = jnp.maximum(m_old, jnp.max(s, axis=1, keepdims=True))
            p = jnp.exp(s - m_new)
            a[...] = jnp.exp(m_old - m_new) * a[...] + _dot(p.astype(BF16), vv)
            m[...] = m_new

    scores(0, 0)

    def pair(p, carry):
        j = 2 * p
        scores(j + 1, 1)
        consume(j, 0, False)
        scores(j + 2, 0)
        consume(j + 1, 1, False)
        return carry

    lax.fori_loop(0, i // 2, pair, 0)

    @pl.when(i % 2 == 0)
    def _():
        consume(i, 0, True)

    @pl.when(i % 2 == 1)
    def _():
        scores(i, 1)
        consume(i - 1, 0, False)
        consume(i, 1, True)

    lv = lam_ref[...]
    lam = (jnp.exp(jnp.sum(lv[0:1] * lv[1:2], axis=1, keepdims=True))
           - jnp.exp(jnp.sum(lv[2:3] * lv[3:4], axis=1, keepdims=True)) + lam_init)
    n = DIFF_VDIM
    o = (a1[:, :n] * (1.0 / a1[:, n:n + 1])) - lam * (a2[:, :n] * (1.0 / a2[:, n:n + 1]))
    o = _rms(o, g_ref[...]) * (1.0 - lam_init)
    o_ref[0] = o.astype(o_ref.dtype)


def _diff_attention(qd, kd, vd, slopes, lam_qk, subln_g, lam_init):
    b, s, _ = qd.shape
    tile = min(ATT_TILE, s)
    pos = jnp.arange(s, dtype=I32)[:, None]
    lane = jnp.arange(V7X_LANES, dtype=I32)[None, :]
    kpos = jnp.where(lane == 0, pos // POS_SPLIT, jnp.where(lane == 1, pos % POS_SPLIT, 0)).astype(BF16)
    ones = jnp.broadcast_to(lane == 0, (tile, V7X_LANES)).astype(BF16)
    vec = lambda shape: pltpu.VMEM(shape, F32)
    const = lambda a: pl.BlockSpec(a.shape, lambda bb, hh, ii: (0, 0))
    return pl.pallas_call(
        functools.partial(_diff_kernel, lam_init=lam_init, tile=tile),
        grid=(b, N_HEADS, s // tile),
        in_specs=[pl.BlockSpec(memory_space=pltpu.SMEM), const(lam_qk), const(subln_g),
                  pl.BlockSpec((1, tile, DIFF_VDIM), lambda bb, hh, ii: (bb, ii, hh)),
                  pl.BlockSpec((1, s, DIFF_VDIM), lambda bb, hh, ii: (bb, 0, hh)),
                  pl.BlockSpec((1, s, DIFF_VDIM), lambda bb, hh, ii: (bb, 0, hh)),
                  const(kpos), const(ones)],
        out_specs=pl.BlockSpec((1, tile, DIFF_VDIM), lambda bb, hh, ii: (bb, ii, hh)),
        out_shape=jax.ShapeDtypeStruct((b, s, DIFF_WIDTH), BF16),
        scratch_shapes=[vec((tile, 1)), vec((tile, 2 * DIFF_VDIM)), vec((2, tile, tile))] * 2,
        compiler_params=_cparams("arbitrary", "arbitrary", "arbitrary"),
        name="diff_attn",
    )(slopes, lam_qk, subln_g, qd, kd, vd, kpos, ones)


def _head_mask(lane, h):
    return (lane >= h * HEAD_DIM) & (lane < (h + 1) * HEAD_DIM)


def _dil_kernel(slopes_ref, q_ref, kp_ref, kc_ref, vp_ref, vc_ref, o_ref, lse_ref, *, dilation, tile):
    i = pl.program_id(2)
    q = q_ref[0]
    kp, kc, vp, vc = kp_ref[0], kc_ref[0], vp_ref[0], vc_ref[0]
    lane = lax.broadcasted_iota(I32, q.shape, 1)
    row = lax.broadcasted_iota(I32, (tile, tile), 0)
    col = lax.broadcasted_iota(I32, (tile, tile), 1)
    d_cur = row - col
    d_prev = d_cur + tile
    ok_cur = d_cur >= 0
    ok_prev = (d_prev <= DIL_WINDOW_KEYS) & (i > 0)
    d_cur_f = d_cur.astype(F32)
    d_prev_f = d_prev.astype(F32)

    out = jnp.zeros(q.shape, F32)
    lse = jnp.zeros(q.shape, F32)
    for h in range(N_HEADS):
        sl = slopes_ref[h] * float(dilation)
        hm = _head_mask(lane, h)
        qh = jnp.where(hm, q, jnp.zeros_like(q))
        s_c = _dot_nt(qh, kc) + jnp.where(ok_cur, -sl * d_cur_f, NEG)
        s_p = _dot_nt(qh, kp) + jnp.where(ok_prev, -sl * d_prev_f, NEG)
        m = jnp.maximum(jnp.max(s_c, axis=1, keepdims=True), jnp.max(s_p, axis=1, keepdims=True))
        p_c = jnp.exp(s_c - m)
        p_p = jnp.exp(s_p - m)
        l = jnp.sum(p_c, axis=1, keepdims=True) + jnp.sum(p_p, axis=1, keepdims=True)
        acc = _dot(p_c.astype(BF16), vc) + _dot(p_p.astype(BF16), vp)
        out = jnp.where(hm, acc * (1.0 / l), out)
        lse = jnp.where(hm, m + jnp.log(l), lse)
    o_ref[0] = out
    lse_ref[0] = lse


def _dil_branch(ql, kl, vl, slopes, dilation):
    b, s, w = ql.shape
    r = dilation
    sr = s // r
    tile = min(DIL_TILE, sr)
    view = lambda t: t.reshape(b, sr, r * w)
    cur = pl.BlockSpec((1, tile, w), lambda bb, rr, ii: (bb, ii, rr))
    prev = pl.BlockSpec((1, tile, w), lambda bb, rr, ii: (bb, jnp.maximum(ii - 1, 0), rr))
    out = jax.ShapeDtypeStruct((b, sr, r * w), F32)
    o, lse = pl.pallas_call(
        functools.partial(_dil_kernel, dilation=r, tile=tile),
        grid=(b, r, sr // tile),
        in_specs=[pl.BlockSpec(memory_space=pltpu.SMEM), cur, prev, cur, prev, cur],
        out_specs=[cur, cur],
        out_shape=[out, out],
        compiler_params=_cparams("arbitrary", "arbitrary", "arbitrary"),
        name=f"dil_attn_r{r}",
    )(slopes, view(ql), view(kl), view(kl), view(vl), view(vl))
    return o.reshape(b * s, w), lse.reshape(b * s, w)


def _memattn_kernel(q_ref, mk_ref, mv_ref, o_ref):
    q = q_ref[0]
    mk, mv = mk_ref[0], mv_ref[0]
    lane = lax.broadcasted_iota(I32, q.shape, 1)
    out = jnp.zeros(q.shape, F32)
    for h in range(N_HEADS):
        hm = _head_mask(lane, h)
        s = _dot_nt(jnp.where(hm, q, jnp.zeros_like(q)), mk)
        m = jnp.max(s, axis=1, keepdims=True)
        p = jnp.exp(s - m)
        l = jnp.sum(p, axis=1, keepdims=True)
        out = jnp.where(hm, _dot(p.astype(BF16), mv) * (1.0 / l), out)
    o_ref[0] = out.astype(o_ref.dtype)


def _mem_attention(mq, mk, mv):
    b, s, w = mq.shape
    m = mk.shape[1]
    tile = min(TOK_TILE, s)
    return pl.pallas_call(
        _memattn_kernel,
        grid=(b, s // tile),
        in_specs=[pl.BlockSpec((1, tile, w), lambda bb, ii: (bb, ii, 0)),
                  pl.BlockSpec((1, m, w), lambda bb, ii: (bb, 0, 0)),
                  pl.BlockSpec((1, m, w), lambda bb, ii: (bb, 0, 0))],
        out_specs=pl.BlockSpec((1, tile, w), lambda bb, ii: (bb, ii, 0)),
        out_shape=jax.ShapeDtypeStruct((b, s, w), BF16),
        compiler_params=_cparams("arbitrary", "arbitrary"),
        name="mem_attn",
    )(mq, mk, mv)


def _outrouter_kernel(x_ref, od_ref, o1, s1, o2, s2, o3, s3, om_ref, wo_ref, g_ref, wr_ref, br_ref,
                      x2_ref, h2_ref, eidx_ref, rank_ref, gate_ref, cnt_ref, run_ref, *, tm):
    step = pl.program_id(0)

    @pl.when(step == 0)
    def _():
        run_ref[...] = jnp.zeros(run_ref.shape, F32)

    lse = (s1[...], s2[...], s3[...])
    big = jnp.maximum(jnp.maximum(lse[0], lse[1]), lse[2])
    wts = [jnp.exp(z - big) for z in lse]
    o_dil = (wts[0] * o1[...] + wts[1] * o2[...] + wts[2] * o3[...]) * (1.0 / (wts[0] + wts[1] + wts[2]))

    c1, c2 = DIFF_WIDTH, DIFF_WIDTH + GROUP_WIDTH
    mix = (_dot(od_ref[...], wo_ref[0:c1, :]) + _dot(o_dil.astype(BF16), wo_ref[c1:c2, :])
           + _dot(om_ref[...], wo_ref[c2:, :]))
    x2 = x_ref[...] + mix
    x2_ref[...] = x2
    h2 = _rms(x2, g_ref[...])
    h2_ref[...] = h2

    h_hi = h2.astype(BF16)
    h_lo = (h2 - h_hi.astype(F32)).astype(BF16)
    logits = (_dot(h_hi, wr_ref[0]) + (_dot(h_hi, wr_ref[1]) + _dot(h_lo, wr_ref[0]))) + br_ref[...]
    lane = lax.broadcasted_iota(I32, logits.shape, 1)
    work = logits
    sel = jnp.zeros(logits.shape, F32)
    vals, idxs = [], []
    for _ in range(TOP_K):
        mx = jnp.max(work, axis=1, keepdims=True)
        idx = jnp.min(jnp.where(work == mx, lane, V7X_LANES), axis=1, keepdims=True)
        hit = lane == idx
        sel = jnp.where(hit, 1.0, sel)
        work = jnp.where(hit, -jnp.inf, work)
        vals.append(mx)
        idxs.append(idx)

    r_io = lax.broadcasted_iota(I32, (tm, tm), 0)
    c_io = lax.broadcasted_iota(I32, (tm, tm), 1)
    tri = jnp.where(c_io < r_io, 1.0, 0.0).astype(BF16)
    before = _dot(tri, sel.astype(BF16)) + run_ref[...]
    run_ref[...] = run_ref[...] + jnp.sum(sel, axis=0, keepdims=True)
    cnt_ref[...] = run_ref[...].astype(I32)

    exps = [jnp.exp(v - vals[0]) for v in vals]
    inv = 1.0 / (exps[0] + exps[1] + exps[2] + exps[3])
    eidx = jnp.zeros(logits.shape, I32)
    rank = jnp.zeros(logits.shape, I32)
    gate = jnp.zeros(logits.shape, F32)
    for k in range(TOP_K):
        rk = jnp.sum(jnp.where(lane == idxs[k], before, 0.0), axis=1, keepdims=True)
        eidx = jnp.where(lane == k, idxs[k], eidx)
        rank = jnp.where(lane == k, rk.astype(I32), rank)
        gate = jnp.where(lane == k, exps[k] * inv, gate)
    eidx_ref[...] = eidx
    rank_ref[...] = rank
    gate_ref[...] = gate


def _out_router(x2d, o_diff, dil, o_mem, w_out, g, w_router_p, b_router_p):
    t, d = x2d.shape
    tm = min(TOK_TILE, t)
    row = lambda n: pl.BlockSpec((tm, n), lambda i: (i, 0))
    full = lambda a: pl.BlockSpec(a.shape, lambda i: (0,) * a.ndim)
    (o1, s1), (o2, s2), (o3, s3) = dil
    return pl.pallas_call(
        functools.partial(_outrouter_kernel, tm=tm),
        grid=(t // tm,),
        in_specs=[row(d), row(DIFF_WIDTH)] + [row(GROUP_WIDTH)] * 7
                 + [full(w_out), full(g), full(w_router_p), full(b_router_p)],
        out_specs=[row(d), row(d), row(V7X_LANES), row(V7X_LANES), row(V7X_LANES),
                   pl.BlockSpec((1, V7X_LANES), lambda i: (0, 0))],
        out_shape=[jax.ShapeDtypeStruct((t, d), F32), jax.ShapeDtypeStruct((t, d), F32),
                   jax.ShapeDtypeStruct((t, V7X_LANES), I32), jax.ShapeDtypeStruct((t, V7X_LANES), I32),
                   jax.ShapeDtypeStruct((t, V7X_LANES), F32), jax.ShapeDtypeStruct((1, V7X_LANES), I32)],
        scratch_shapes=[pltpu.VMEM((1, V7X_LANES), F32)],
        compiler_params=_cparams("arbitrary"),
        name="out_router",
    )(x2d, o_diff, o1, s1, o2, s2, o3, s3, o_mem, w_out, g, w_router_p, b_router_p)


def _row_copy(src, dst, sem):
    return pltpu.make_async_copy(src, dst, sem)


def _dispatch_kernel(dest_ref, h_ref, xs_hbm, sem, *, tm):
    def issue(t, carry):
        for k in range(TOP_K):
            d = dest_ref[0, 0, t * TOP_K + k]
            _row_copy(h_ref.at[pl.ds(t, 1), :], xs_hbm.at[pl.ds(d, 1), :], sem).start()
        return carry

    lax.fori_loop(0, tm, issue, 0, unroll=DMA_UNROLL)
    for _ in range(TOP_K):
        _row_copy(h_ref, xs_hbm.at[pl.ds(0, tm), :], sem).wait()


def _dispatch(h2, dest3):
    t, d = h2.shape
    tm = min(DMA_TILE, t)
    return pl.pallas_call(
        functools.partial(_dispatch_kernel, tm=tm),
        grid=(t // tm,),
        in_specs=[pl.BlockSpec((1, 1, tm * TOP_K), lambda i: (i, 0, 0), memory_space=pltpu.SMEM),
                  pl.BlockSpec((tm, d), lambda i: (i, 0))],
        out_specs=pl.BlockSpec(memory_space=pl.ANY),
        out_shape=jax.ShapeDtypeStruct((t * TOP_K, d), h2.dtype),
        scratch_shapes=[pltpu.SemaphoreType.DMA],
        compiler_params=_cparams("arbitrary"),
        name="dispatch",
    )(dest3, h2)


def _experts_kernel(tile_ref, exp_ref, lo_ref, hi_ref, first_ref, act_ref,
                    xs_ref, wg_ref, wu_ref, bg_ref, bu_ref, wd_ref, bd_ref, y_ref, *, tm):
    w = pl.program_id(0)

    @pl.when(act_ref[w] == 1)
    def _():
        x = xs_ref[...].astype(BF16)
        gate = jnp.minimum(_dot(x, wg_ref[0]) + bg_ref[0], SWIGLU_LIMIT)
        up = jnp.clip(_dot(x, wu_ref[0]) + bu_ref[0], -SWIGLU_LIMIT, SWIGLU_LIMIT)
        act = (up + 1.0) * (gate * jax.nn.sigmoid(gate * SWIGLU_ALPHA))
        y = _dot(act.astype(BF16), wd_ref[0]) + bd_ref[0]
        row = lax.broadcasted_iota(I32, y.shape, 0)
        mine = (row >= lo_ref[w]) & (row < hi_ref[w])

        @pl.when(first_ref[w] == 1)
        def _():
            y_ref[...] = jnp.where(mine, y, 0.0)

        @pl.when(first_ref[w] == 0)
        def _():
            y_ref[...] = jnp.where(mine, y, y_ref[...])


def _experts(xs, meta, wg, wu, bg, bu, wd, bd):
    n, d = xs.shape
    e, _, f = wg.shape
    tm = min(ROW_TILE, n)
    n_items = meta[0].shape[0]
    xmap = lambda w, tile, ex, lo, hi, fi, ac: (tile[w], 0)
    wmap = lambda w, tile, ex, lo, hi, fi, ac: (ex[w], 0, 0)
    grid_spec = pltpu.PrefetchScalarGridSpec(
        num_scalar_prefetch=6,
        grid=(n_items,),
        in_specs=[pl.BlockSpec((tm, d), xmap),
                  pl.BlockSpec((1, d, f), wmap), pl.BlockSpec((1, d, f), wmap),
                  pl.BlockSpec((1, 1, f), wmap), pl.BlockSpec((1, 1, f), wmap),
                  pl.BlockSpec((1, f, d), wmap), pl.BlockSpec((1, 1, d), wmap)],
        out_specs=pl.BlockSpec((tm, d), xmap),
    )
    return pl.pallas_call(
        functools.partial(_experts_kernel, tm=tm),
        grid_spec=grid_spec,
        out_shape=jax.ShapeDtypeStruct((n, d), F32),
        compiler_params=_cparams("arbitrary"),
        name="experts",
    )(*meta, xs, wg, wu, bg, bu, wd, bd)


def _expert_work_items(counts, n_rows, tm):
    e = counts.shape[0]
    n_tiles = n_rows // tm
    n_items = n_tiles + e - 1
    ends = jnp.cumsum(counts)
    starts = ends - counts
    first_tile = starts // tm
    n_t = jnp.where(counts > 0, (ends - 1) // tm - first_tile + 1, 0)
    item_end = jnp.cumsum(n_t)
    item_start = item_end - n_t
    total = item_end[-1]
    w = jnp.arange(n_items, dtype=I32)
    active = w < total
    w_eff = jnp.minimum(w, total - 1)
    ex = jnp.minimum(jnp.sum((item_end[None, :] <= w_eff[:, None]).astype(I32), axis=1), e - 1)
    tile = (first_tile[ex] + (w_eff - item_start[ex])).astype(I32)
    lo = jnp.clip(starts[ex] - tile * tm, 0, tm).astype(I32)
    hi = jnp.clip(ends[ex] - tile * tm, 0, tm).astype(I32)
    first = jnp.concatenate([jnp.ones((1,), I32), (tile[1:] != tile[:-1]).astype(I32)])
    return tile, ex, lo, hi, first, active.astype(I32)


def _combine_kernel(dest_ref, dest_next_ref, x2_ref, gate_ref, g_ref, y_hbm, o_ref, ybuf, sems, *, tm, n_steps):
    i = pl.program_id(0)
    slot = i % 2

    def gather(dref, into):
        def issue(t, carry):
            for k in range(TOP_K):
                d = dref[0, 0, t * TOP_K + k]
                _row_copy(y_hbm.at[pl.ds(d, 1), :], ybuf.at[into, k, pl.ds(t, 1), :], sems.at[into]).start()
            return carry
        lax.fori_loop(0, tm, issue, 0, unroll=DMA_UNROLL)

    @pl.when(i == 0)
    def _():
        gather(dest_ref, 0)

    @pl.when(i + 1 < n_steps)
    def _():
        gather(dest_next_ref, 1 - slot)

    for k in range(TOP_K):
        _row_copy(y_hbm.at[pl.ds(0, tm), :], ybuf.at[slot, k], sems.at[slot]).wait()

    gates = gate_ref[...]
    acc = x2_ref[...]
    for k in range(TOP_K):
        acc = acc + ybuf[slot, k] * gates[:, k:k + 1]
    o_ref[...] = _rms(acc, g_ref[...])


def _combine(y, dest3, x2, gates, g_final):
    t, d = x2.shape
    tm = min(DMA_TILE, t)
    n_steps = t // tm
    dest_spec = lambda f: pl.BlockSpec((1, 1, tm * TOP_K), f, memory_space=pltpu.SMEM)
    return pl.pallas_call(
        functools.partial(_combine_kernel, tm=tm, n_steps=n_steps),
        grid=(n_steps,),
        in_specs=[dest_spec(lambda i: (i, 0, 0)),
                  dest_spec(lambda i: (jnp.minimum(i + 1, n_steps - 1), 0, 0)),
                  pl.BlockSpec((tm, d), lambda i: (i, 0)),
                  pl.BlockSpec((tm, V7X_LANES), lambda i: (i, 0)),
                  pl.BlockSpec((1, d), lambda i: (0, 0)),
                  pl.BlockSpec(memory_space=pl.ANY)],
        out_specs=pl.BlockSpec((tm, d), lambda i: (i, 0)),
        out_shape=jax.ShapeDtypeStruct((t, d), F32),
        scratch_shapes=[pltpu.VMEM((2, TOP_K, tm, d), F32), pltpu.SemaphoreType.DMA((2,))],
        compiler_params=_cparams("arbitrary"),
        name="combine",
    )(dest3, dest3, x2, gates, g_final, y)


def _alibi_slopes():
    n = 2 * N_HEADS
    s = 2.0 ** (-8.0 * (jnp.arange(n, dtype=F32) + 1.0) / n)
    return s[0::2], s[1::2]


def kernel(x, mem, attn_norm_g, mem_norm_g, w_in, diff_lambda_qk, diff_subln_g, w_mem_kv, w_out,
           ffn_norm_g, w_router, b_router, w_gate_up, b_gate_up, w_down, b_down, final_norm_g):
    b, s, d = x.shape
    t = b * s
    assert w_in.shape[0] == 1, "the final norm is fused into the layer's combine step: one layer only"
    assert s % (DIL_DILATIONS[-1] * DIL_TILE) == 0 and t % TOK_TILE == 0
    n_exp = w_router.shape[-1]
    diff_slopes, dil_slopes = _alibi_slopes()
    x2d = x.reshape(t, d)

    for l in range(1):
        lam_init = 0.8 - 0.6 * math.exp(-0.3 * l)
        mk, mv = _mem_kv(mem, mem_norm_g[l][None], w_mem_kv[l].astype(BF16))
        qd, kd, vd, ql, kl, vl, mq = _in_proj(x2d, attn_norm_g[l][None], w_in[l].astype(BF16))
        sh = lambda a: a.reshape(b, s, a.shape[-1])
        o_diff = _diff_attention(sh(qd), sh(kd), sh(vd), diff_slopes, diff_lambda_qk[l],
                                 diff_subln_g[l][None], lam_init)
        dil = [_dil_branch(sh(ql), sh(kl), sh(vl), dil_slopes, r) for r in DIL_DILATIONS]
        o_mem = _mem_attention(sh(mq), mk, mv)

        w_router_f = jnp.zeros((d, V7X_LANES), F32).at[:, :n_exp].set(w_router[l])
        w_router_hi = w_router_f.astype(BF16)
        w_router_p = jnp.stack([w_router_hi, (w_router_f - w_router_hi.astype(F32)).astype(BF16)])
        b_router_p = jnp.full((1, V7X_LANES), NEG, F32).at[0, :n_exp].set(b_router[l])
        x2, h2, eidx, rank, gates, cnt = _out_router(
            x2d, o_diff.reshape(t, DIFF_WIDTH), dil, o_mem.reshape(t, GROUP_WIDTH),
            w_out[l].astype(BF16), ffn_norm_g[l][None], w_router_p, b_router_p)

        counts = cnt[0, :n_exp]
        starts = jnp.cumsum(counts) - counts
        e_sel = eidx[:, :TOP_K]
        onehot = e_sel[:, :, None] == jnp.arange(n_exp, dtype=I32)[None, None, :]
        dest = rank[:, :TOP_K] + jnp.sum(jnp.where(onehot, starts[None, None, :], 0), axis=-1)
        tm_dma = min(DMA_TILE, t)
        dest3 = dest.astype(I32).reshape(t // tm_dma, 1, tm_dma * TOP_K)

        xs = _dispatch(h2, dest3)

        f = w_down.shape[2]
        wgu = w_gate_up[l].reshape(n_exp, d, f, 2)
        bgu = b_gate_up[l].reshape(n_exp, 1, f, 2)
        meta = _expert_work_items(counts, t * TOP_K, min(ROW_TILE, t * TOP_K))
        y = _experts(xs, meta, wgu[..., 0].astype(BF16), wgu[..., 1].astype(BF16),
                     bgu[..., 0], bgu[..., 1], w_down[l].astype(BF16), b_down[l][:, None, :])

        out = _combine(y, dest3, x2, gates, final_norm_g[None])
    return out.reshape(b, s, d)
```

```python
import functools
import math

import jax
import jax.numpy as jnp
from jax import lax
from jax.experimental import pallas as pl
from jax.experimental.pallas import tpu as pltpu

F32 = jnp.float32
BF16 = jnp.bfloat16
I32 = jnp.int32

NORM_EPS = 1e-5
HEAD_DIM = 64
N_HEADS = 4
DIFF_VDIM = 2 * HEAD_DIM
DIFF_WIDTH = N_HEADS * DIFF_VDIM
GROUP_WIDTH = N_HEADS * HEAD_DIM
DIL_DILATIONS = (1, 4, 16)
DIL_WINDOW_KEYS = 128
TOP_K = 4
SWIGLU_ALPHA = 1.702
SWIGLU_LIMIT = 7.0
NEG = -1e30

V7X_LANES = 128
V7X_VMEM_LIMIT = 56 * 1024 * 1024

TOK_TILE = 512
ATT_TILE = 512
POS_SPLIT = 64
DIL_TILE = DIL_WINDOW_KEYS
DIL_CHUNK = 256
ROW_TILE = 512
DMA_TILE = 256
DMA_UNROLL = 4


def _cparams(*sem):
    return pltpu.CompilerParams(dimension_semantics=tuple(sem), vmem_limit_bytes=V7X_VMEM_LIMIT)


def _rms(x, g):
    return x * lax.rsqrt(jnp.mean(x * x, axis=-1, keepdims=True) + NORM_EPS) * g


def _dot_nt(a, b):
    return lax.dot_general(a, b, (((1,), (1,)), ((), ())), preferred_element_type=F32)


def _dot(a, b):
    return jnp.dot(a, b, preferred_element_type=F32)


def _memkv_kernel(mem_ref, g_ref, w_ref, mk_ref, mv_ref):
    h = _rms(mem_ref[0], g_ref[...]).astype(BF16)
    kv = _dot(h, w_ref[...])
    mk_ref[0] = kv[:, :GROUP_WIDTH].astype(BF16)
    mv_ref[0] = kv[:, GROUP_WIDTH:].astype(BF16)


def _mem_kv(mem, g, w):
    b, m, d = mem.shape
    out = jax.ShapeDtypeStruct((b, m, GROUP_WIDTH), BF16)
    return pl.pallas_call(
        _memkv_kernel,
        grid=(b,),
        in_specs=[pl.BlockSpec((1, m, d), lambda i: (i, 0, 0)),
                  pl.BlockSpec((1, d), lambda i: (0, 0)),
                  pl.BlockSpec((d, 2 * GROUP_WIDTH), lambda i: (0, 0))],
        out_specs=[pl.BlockSpec((1, m, GROUP_WIDTH), lambda i: (i, 0, 0))] * 2,
        out_shape=[out, out],
        compiler_params=_cparams("arbitrary"),
        name="mem_kv",
    )(mem, g, w)


_IN_SEGMENTS = ((DIFF_WIDTH, HEAD_DIM ** -0.5), (DIFF_WIDTH, 1.0), (DIFF_WIDTH, 1.0),
                (GROUP_WIDTH, HEAD_DIM ** -0.5), (GROUP_WIDTH, 1.0), (GROUP_WIDTH, 1.0),
                (GROUP_WIDTH, HEAD_DIM ** -0.5))


def _inproj_kernel(x_ref, g_ref, w_ref, *out_refs):
    h = _rms(x_ref[...], g_ref[...]).astype(BF16)
    c0 = 0
    for ref, (n, scale) in zip(out_refs, _IN_SEGMENTS):
        p = _dot(h, w_ref[:, c0:c0 + n])
        if scale != 1.0:
            p = p * scale
        ref[...] = p.astype(BF16)
        c0 += n


def _in_proj(x2d, g, w):
    t, d = x2d.shape
    tm = min(TOK_TILE, t)
    return pl.pallas_call(
        _inproj_kernel,
        grid=(t // tm,),
        in_specs=[pl.BlockSpec((tm, d), lambda i: (i, 0)),
                  pl.BlockSpec((1, d), lambda i: (0, 0)),
                  pl.BlockSpec(w.shape, lambda i: (0, 0))],
        out_specs=[pl.BlockSpec((tm, n), lambda i: (i, 0)) for n, _ in _IN_SEGMENTS],
        out_shape=[jax.ShapeDtypeStruct((t, n), BF16) for n, _ in _IN_SEGMENTS],
        compiler_params=_cparams("arbitrary"),
        name="in_proj",
    )(x2d, g, w)


def _diff_kernel(slopes_ref, lam_ref, g_ref, q_ref, k_ref, v_ref, kpos_ref, ones_ref, o_ref,
                 m1, a1, s1, m2, a2, s2, *, lam_init, tile):
    h = pl.program_id(1)
    i = pl.program_id(2)
    slope = slopes_ref[h]

    q = q_ref[0]
    lane = lax.broadcasted_iota(I32, q.shape, 1)
    zero = jnp.zeros_like(q)
    qpos = jnp.where(lane == 0, slope * float(POS_SPLIT), jnp.where(lane == 1, slope, 0.0)).astype(BF16)
    q1 = jnp.concatenate([jnp.where(lane < HEAD_DIM, q, zero), qpos], axis=1)
    q2 = jnp.concatenate([jnp.where(lane >= HEAD_DIM, q, zero), qpos], axis=1)

    for m, a in ((m1, a1), (m2, a2)):
        m[...] = jnp.full(m.shape, NEG, F32)
        a[...] = jnp.zeros(a.shape, F32)

    def scores(j, slot):
        start = pl.multiple_of(j * tile, tile)
        kk = jnp.concatenate([k_ref[0, pl.ds(start, tile), :], kpos_ref[pl.ds(start, tile), :]], axis=1)
        s1[slot] = _dot_nt(q1, kk)
        s2[slot] = _dot_nt(q2, kk)

    def consume(j, slot, diagonal):
        start = pl.multiple_of(j * tile, tile)
        vv = jnp.concatenate([v_ref[0, pl.ds(start, tile), :], ones_ref[...]], axis=1)
        for sbuf, m, a in ((s1, m1, a1), (s2, m2, a2)):
            s = sbuf[slot]
            if diagonal:
                row = lax.broadcasted_iota(I32, s.shape, 0)
                col = lax.broadcasted_iota(I32, s.shape, 1)
                s = jnp.where(col <= row, s, NEG)
            m_old = m[...]
            m_new = jnp.maximum(m_old, jnp.max(s, axis=1, keepdims=True))
            p = jnp.exp(s - m_new)
            a[...] = jnp.exp(m_old - m_new) * a[...] + _dot(p.astype(BF16), vv)
            m[...] = m_new

    scores(0, 0)

    def pair(p, carry):
        j = 2 * p
        scores(j + 1, 1)
        consume(j, 0, False)
        scores(j + 2, 0)
        consume(j + 1, 1, False)
        return carry

    lax.fori_loop(0, i // 2, pair, 0)

    @pl.when(i % 2 == 0)
    def _():
        consume(i, 0, True)

    @pl.when(i % 2 == 1)
    def _():
        scores(i, 1)
        consume(i - 1, 0, False)
        consume(i, 1, True)

    lv = lam_ref[...]
    lam = (jnp.exp(jnp.sum(lv[0:1] * lv[1:2], axis=1, keepdims=True))
           - jnp.exp(jnp.sum(lv[2:3] * lv[3:4], axis=1, keepdims=True)) + lam_init)
    n = DIFF_VDIM
    o = (a1[:, :n] * (1.0 / a1[:, n:n + 1])) - lam * (a2[:, :n] * (1.0 / a2[:, n:n + 1]))
    o = _rms(o, g_ref[...]) * (1.0 - lam_init)
    o_ref[0] = o.astype(o_ref.dtype)


def _diff_attention(qd, kd, vd, slopes, lam_qk, subln_g, lam_init):
    b, s, _ = qd.shape
    tile = min(ATT_TILE, s)
    pos = jnp.arange(s, dtype=I32)[:, None]
    lane = jnp.arange(V7X_LANES, dtype=I32)[None, :]
    kpos = jnp.where(lane == 0, pos // POS_SPLIT, jnp.where(lane == 1, pos % POS_SPLIT, 0)).astype(BF16)
    ones = jnp.broadcast_to(lane == 0, (tile, V7X_LANES)).astype(BF16)
    vec = lambda shape: pltpu.VMEM(shape, F32)
    const = lambda a: pl.BlockSpec(a.shape, lambda bb, hh, ii: (0, 0))
    return pl.pallas_call(
        functools.partial(_diff_kernel, lam_init=lam_init, tile=tile),
        grid=(b, N_HEADS, s // tile),
        in_specs=[pl.BlockSpec(memory_space=pltpu.SMEM), const(lam_qk), const(subln_g),
                  pl.BlockSpec((1, tile, DIFF_VDIM), lambda bb, hh, ii: (bb, ii, hh)),
                  pl.BlockSpec((1, s, DIFF_VDIM), lambda bb, hh, ii: (bb, 0, hh)),
                  pl.BlockSpec((1, s, DIFF_VDIM), lambda bb, hh, ii: (bb, 0, hh)),
                  const(kpos), const(ones)],
        out_specs=pl.BlockSpec((1, tile, DIFF_VDIM), lambda bb, hh, ii: (bb, ii, hh)),
        out_shape=jax.ShapeDtypeStruct((b, s, DIFF_WIDTH), BF16),
        scratch_shapes=[vec((tile, 1)), vec((tile, 2 * DIFF_VDIM)), vec((2, tile, tile))] * 2,
        compiler_params=_cparams("arbitrary", "arbitrary", "arbitrary"),
        name="diff_attn",
    )(slopes, lam_qk, subln_g, qd, kd, vd, kpos, ones)


def _head_mask(lane, h):
    return (lane >= h * HEAD_DIM) & (lane < (h + 1) * HEAD_DIM)


def _split3(x):
    hi = x.astype(BF16)
    rem = x - hi.astype(F32)
    mid = rem.astype(BF16)
    return hi, mid, (rem - mid.astype(F32)).astype(BF16)


def _dil_kernel(q_ref, k_ref, v_ref, perm_ref, bias_ref, spread_ref, o_ref, qc, kc, vc, res_o, res_l, *, seq):
    tile, chunk = DIL_TILE, DIL_CHUNK
    n_blocks, n_chunks = seq // tile, seq // chunk
    lane = lax.broadcasted_iota(I32, (tile, GROUP_WIDTH), 1)
    tail_lane = lax.broadcasted_iota(I32, (tile, V7X_LANES), 1)

    def to_class_order(bi, r):
        n, sr = chunk // r, seq // r

        def one_chunk(t, carry):
            src0 = pl.multiple_of(t * chunk, chunk)
            for src, dst in ((q_ref, qc), (k_ref, kc), (v_ref, vc)):
                y = _dot(perm_ref[bi - 1, 0], src[0, pl.ds(src0, chunk), :]).astype(BF16)
                for rho in range(r):
                    dst[pl.ds(pl.multiple_of(rho * sr + t * n, n), n), :] = y[rho * n:(rho + 1) * n, :]
            return carry

        lax.fori_loop(0, n_chunks, one_chunk, 0)

    def attend(bi, r, load):
        blocks_per_class = (seq // r) // tile

        def one_block(g, carry):
            cur = pl.multiple_of(g * tile, tile)
            prev = pl.multiple_of(jnp.maximum(g - 1, 0) * tile, tile)
            has_prev = (g % blocks_per_class) != 0
            q = load(0, cur)
            qs = jnp.concatenate([jnp.where(_head_mask(lane, h), q, jnp.zeros_like(q)) for h in range(N_HEADS)], axis=0)
            kk = jnp.concatenate([load(1, prev), load(1, cur)], axis=0)
            vv = jnp.concatenate([load(2, prev), load(2, cur)], axis=0)
            bias = jnp.where(has_prev, bias_ref[bi, 1], bias_ref[bi, 0])
            s = _dot_nt(qs, kk) + bias
            m = jnp.max(s, axis=1, keepdims=True)
            p = jnp.exp(s - m)
            l = jnp.sum(p, axis=1, keepdims=True)
            pv = _dot(p.astype(BF16), vv) * (1.0 / l)
            lse = m + jnp.log(l)
            out = jnp.zeros((tile, GROUP_WIDTH), F32)
            tail = jnp.zeros((tile, V7X_LANES), F32)
            for h in range(N_HEADS):
                rows = slice(h * tile, (h + 1) * tile)
                out = jnp.where(_head_mask(lane, h), pv[rows], out)
                tail = jnp.where(tail_lane == h, lse[rows], tail)
            res_o[bi, pl.ds(cur, tile), :] = out.astype(BF16)
            res_l[bi, pl.ds(cur, tile), :] = tail
            return carry

        lax.fori_loop(0, n_blocks, one_block, 0, unroll=4)

    refs = (q_ref, k_ref, v_ref)
    scratch = (qc, kc, vc)
    attend(0, 1, lambda which, start: refs[which][0, pl.ds(start, tile), :])
    for bi, r in enumerate(DIL_DILATIONS[1:], start=1):
        to_class_order(bi, r)
        attend(bi, r, lambda which, start: scratch[which][pl.ds(start, tile), :])

    def merge_chunk(t, carry):
        base = pl.multiple_of(t * chunk, chunk)
        def pack3(lse):
            hi, mid, lo = _split3(lse)
            return (hi.astype(F32) + pltpu.roll(mid.astype(F32), N_HEADS, 1)
                    + pltpu.roll(lo.astype(F32), 2 * N_HEADS, 1)).astype(BF16)

        outs, lses = [], []
        for bi, r in enumerate(DIL_DILATIONS):
            if r == 1:
                out = res_o[0, pl.ds(base, chunk), :].astype(F32)
                packed = pack3(res_l[0, pl.ds(base, chunk), :])
            else:
                n, sr = chunk // r, seq // r
                rows = [pl.ds(pl.multiple_of(rho * sr + t * n, n), n) for rho in range(r)]
                back = perm_ref[bi - 1, 1]
                out = _dot(back, jnp.concatenate([res_o[bi, rw, :] for rw in rows], axis=0))
                packed = _dot(back, pack3(jnp.concatenate([res_l[bi, rw, :] for rw in rows], axis=0))).astype(BF16)
            outs.append(out)
            lses.append(_dot(packed, spread_ref[...]))
        big = jnp.maximum(jnp.maximum(lses[0], lses[1]), lses[2])
        wts = [jnp.exp(z - big) for z in lses]
        o = (wts[0] * outs[0] + wts[1] * outs[1] + wts[2] * outs[2]) * (1.0 / (wts[0] + wts[1] + wts[2]))
        o_ref[0, pl.ds(base, chunk), :] = o.astype(o_ref.dtype)
        return carry

    lax.fori_loop(0, n_chunks, merge_chunk, 0)


def _dil_constants(slopes, seq):
    chunk, tile = DIL_CHUNK, DIL_TILE
    perms = []
    for r in DIL_DILATIONS[1:]:
        n = chunk // r
        dst = jnp.arange(chunk, dtype=I32)
        src = r * (dst % n) + dst // n
        fwd = (src[:, None] == jnp.arange(chunk, dtype=I32)[None, :]).astype(BF16)
        perms.append(jnp.stack([fwd, fwd.T]))
    m = jnp.arange(tile, dtype=I32)[:, None]
    j = jnp.arange(2 * tile, dtype=I32)[None, :]
    dist = jnp.where(j < tile, m - j + tile, m - (j - tile))
    ok_cur = (j >= tile) & (dist >= 0)
    ok_prev = (j < tile) & (dist <= DIL_WINDOW_KEYS)
    biases = []
    for r in DIL_DILATIONS:
        per_head = [-(slopes[h] * r) * dist.astype(F32) for h in range(N_HEADS)]
        with_prev = jnp.concatenate([jnp.where(ok_cur | ok_prev, b, NEG) for b in per_head], axis=0)
        no_prev = jnp.concatenate([jnp.where(ok_cur, b, NEG) for b in per_head], axis=0)
        biases.append(jnp.stack([no_prev, with_prev]))
    src_lane = jnp.arange(V7X_LANES, dtype=I32)[:, None]
    spread = ((src_lane < 3 * N_HEADS)
              & (src_lane % N_HEADS == jnp.arange(GROUP_WIDTH, dtype=I32)[None, :] // HEAD_DIM)).astype(BF16)
    return jnp.stack(perms), jnp.stack(biases), spread


def _dil_attention(ql, kl, vl, slopes):
    b, s, w = ql.shape
    perms, biases, spread = _dil_constants(slopes, s)
    row = pl.BlockSpec((1, s, w), lambda i: (i, 0, 0))
    const = lambda a: pl.BlockSpec(a.shape, lambda i: (0,) * a.ndim)
    return pl.pallas_call(
        functools.partial(_dil_kernel, seq=s),
        grid=(b,),
        in_specs=[row, row, row, const(perms), const(biases), const(spread)],
        out_specs=row,
        out_shape=jax.ShapeDtypeStruct((b, s, w), BF16),
        scratch_shapes=[pltpu.VMEM((s, w), BF16)] * 3
                       + [pltpu.VMEM((len(DIL_DILATIONS), s, w), BF16),
                          pltpu.VMEM((len(DIL_DILATIONS), s, V7X_LANES), F32)],
        compiler_params=_cparams("arbitrary"),
        name="dil_attn",
    )(ql, kl, vl, perms, biases, spread)


def _memattn_kernel(q_ref, mk_ref, mv_ref, o_ref):
    q = q_ref[0]
    mk, mv = mk_ref[0], mv_ref[0]
    lane = lax.broadcasted_iota(I32, q.shape, 1)
    out = jnp.zeros(q.shape, F32)
    for h in range(N_HEADS):
        hm = _head_mask(lane, h)
        s = _dot_nt(jnp.where(hm, q, jnp.zeros_like(q)), mk)
        m = jnp.max(s, axis=1, keepdims=True)
        p = jnp.exp(s - m)
        l = jnp.sum(p, axis=1, keepdims=True)
        out = jnp.where(hm, _dot(p.astype(BF16), mv) * (1.0 / l), out)
    o_ref[0] = out.astype(o_ref.dtype)


def _mem_attention(mq, mk, mv):
    b, s, w = mq.shape
    m = mk.shape[1]
    tile = min(TOK_TILE, s)
    return pl.pallas_call(
        _memattn_kernel,
        grid=(b, s // tile),
        in_specs=[pl.BlockSpec((1, tile, w), lambda bb, ii: (bb, ii, 0)),
                  pl.BlockSpec((1, m, w), lambda bb, ii: (bb, 0, 0)),
                  pl.BlockSpec((1, m, w), lambda bb, ii: (bb, 0, 0))],
        out_specs=pl.BlockSpec((1, tile, w), lambda bb, ii: (bb, ii, 0)),
        out_shape=jax.ShapeDtypeStruct((b, s, w), BF16),
        compiler_params=_cparams("arbitrary", "arbitrary"),
        name="mem_attn",
    )(mq, mk, mv)


def _outrouter_kernel(x_ref, od_ref, ol_ref, om_ref, wo_ref, g_ref, wr_ref, br_ref,
                      x2_ref, h2_ref, eidx_ref, rank_ref, gate_ref, cnt_ref, run_ref, *, tm):
    step = pl.program_id(0)

    @pl.when(step == 0)
    def _():
        run_ref[...] = jnp.zeros(run_ref.shape, F32)

    c1, c2 = DIFF_WIDTH, DIFF_WIDTH + GROUP_WIDTH
    mix = (_dot(od_ref[...], wo_ref[0:c1, :]) + _dot(ol_ref[...], wo_ref[c1:c2, :])
           + _dot(om_ref[...], wo_ref[c2:, :]))
    x2 = x_ref[...] + mix
    x2_ref[...] = x2
    h2 = _rms(x2, g_ref[...])
    h2_ref[...] = h2

    h_hi = h2.astype(BF16)
    h_lo = (h2 - h_hi.astype(F32)).astype(BF16)
    logits = (_dot(h_hi, wr_ref[0]) + (_dot(h_hi, wr_ref[1]) + _dot(h_lo, wr_ref[0]))) + br_ref[...]
    lane = lax.broadcasted_iota(I32, logits.shape, 1)
    work = logits
    sel = jnp.zeros(logits.shape, F32)
    vals, idxs = [], []
    for _ in range(TOP_K):
        mx = jnp.max(work, axis=1, keepdims=True)
        idx = jnp.min(jnp.where(work == mx, lane, V7X_LANES), axis=1, keepdims=True)
        hit = lane == idx
        sel = jnp.where(hit, 1.0, sel)
        work = jnp.where(hit, -jnp.inf, work)
        vals.append(mx)
        idxs.append(idx)

    r_io = lax.broadcasted_iota(I32, (tm, tm), 0)
    c_io = lax.broadcasted_iota(I32, (tm, tm), 1)
    tri = jnp.where(c_io < r_io, 1.0, 0.0).astype(BF16)
    before = _dot(tri, sel.astype(BF16)) + run_ref[...]
    run_ref[...] = run_ref[...] + jnp.sum(sel, axis=0, keepdims=True)
    cnt_ref[...] = run_ref[...].astype(I32)

    exps = [jnp.exp(v - vals[0]) for v in vals]
    inv = 1.0 / (exps[0] + exps[1] + exps[2] + exps[3])
    eidx = jnp.zeros(logits.shape, I32)
    rank = jnp.zeros(logits.shape, I32)
    gate = jnp.zeros(logits.shape, F32)
    for k in range(TOP_K):
        rk = jnp.sum(jnp.where(lane == idxs[k], before, 0.0), axis=1, keepdims=True)
        eidx = jnp.where(lane == k, idxs[k], eidx)
        rank = jnp.where(lane == k, rk.astype(I32), rank)
        gate = jnp.where(lane == k, exps[k] * inv, gate)
    eidx_ref[...] = eidx
    rank_ref[...] = rank
    gate_ref[...] = gate


def _out_router(x2d, o_diff, o_dil, o_mem, w_out, g, w_router_p, b_router_p):
    t, d = x2d.shape
    tm = min(TOK_TILE, t)
    row = lambda n: pl.BlockSpec((tm, n), lambda i: (i, 0))
    full = lambda a: pl.BlockSpec(a.shape, lambda i: (0,) * a.ndim)
    return pl.pallas_call(
        functools.partial(_outrouter_kernel, tm=tm),
        grid=(t // tm,),
        in_specs=[row(d), row(DIFF_WIDTH), row(GROUP_WIDTH), row(GROUP_WIDTH),
                  full(w_out), full(g), full(w_router_p), full(b_router_p)],
        out_specs=[row(d), row(d), row(V7X_LANES), row(V7X_LANES), row(V7X_LANES),
                   pl.BlockSpec((1, V7X_LANES), lambda i: (0, 0))],
        out_shape=[jax.ShapeDtypeStruct((t, d), F32), jax.ShapeDtypeStruct((t, d), F32),
                   jax.ShapeDtypeStruct((t, V7X_LANES), I32), jax.ShapeDtypeStruct((t, V7X_LANES), I32),
                   jax.ShapeDtypeStruct((t, V7X_LANES), F32), jax.ShapeDtypeStruct((1, V7X_LANES), I32)],
        scratch_shapes=[pltpu.VMEM((1, V7X_LANES), F32)],
        compiler_params=_cparams("arbitrary"),
        name="out_router",
    )(x2d, o_diff, o_dil, o_mem, w_out, g, w_router_p, b_router_p)


def _row_copy(src, dst, sem):
    return pltpu.make_async_copy(src, dst, sem)


def _dispatch_kernel(dest_ref, h_ref, xs_hbm, sem, *, tm):
    def issue(t, carry):
        for k in range(TOP_K):
            d = dest_ref[0, 0, t * TOP_K + k]
            _row_copy(h_ref.at[pl.ds(t, 1), :], xs_hbm.at[pl.ds(d, 1), :], sem).start()
        return carry

    lax.fori_loop(0, tm, issue, 0, unroll=DMA_UNROLL)
    for _ in range(TOP_K):
        _row_copy(h_ref, xs_hbm.at[pl.ds(0, tm), :], sem).wait()


def _dispatch(h2, dest3):
    t, d = h2.shape
    tm = min(DMA_TILE, t)
    return pl.pallas_call(
        functools.partial(_dispatch_kernel, tm=tm),
        grid=(t // tm,),
        in_specs=[pl.BlockSpec((1, 1, tm * TOP_K), lambda i: (i, 0, 0), memory_space=pltpu.SMEM),
                  pl.BlockSpec((tm, d), lambda i: (i, 0))],
        out_specs=pl.BlockSpec(memory_space=pl.ANY),
        out_shape=jax.ShapeDtypeStruct((t * TOP_K, d), h2.dtype),
        scratch_shapes=[pltpu.SemaphoreType.DMA],
        compiler_params=_cparams("arbitrary"),
        name="dispatch",
    )(dest3, h2)


def _experts_kernel(tile_ref, exp_ref, lo_ref, hi_ref, first_ref, act_ref,
                    xs_ref, wg_ref, wu_ref, bg_ref, bu_ref, wd_ref, bd_ref, y_ref, *, tm):
    w = pl.program_id(0)

    @pl.when(act_ref[w] == 1)
    def _():
        x = xs_ref[...].astype(BF16)
        gate = jnp.minimum(_dot(x, wg_ref[0]) + bg_ref[0], SWIGLU_LIMIT)
        up = jnp.clip(_dot(x, wu_ref[0]) + bu_ref[0], -SWIGLU_LIMIT, SWIGLU_LIMIT)
        act = (up + 1.0) * (gate * jax.nn.sigmoid(gate * SWIGLU_ALPHA))
        y = _dot(act.astype(BF16), wd_ref[0]) + bd_ref[0]
        row = lax.broadcasted_iota(I32, y.shape, 0)
        mine = (row >= lo_ref[w]) & (row < hi_ref[w])

        @pl.when(first_ref[w] == 1)
        def _():
            y_ref[...] = jnp.where(mine, y, 0.0)

        @pl.when(first_ref[w] == 0)
        def _():
            y_ref[...] = jnp.where(mine, y, y_ref[...])


def _experts(xs, meta, wg, wu, bg, bu, wd, bd):
    n, d = xs.shape
    e, _, f = wg.shape
    tm = min(ROW_TILE, n)
    n_items = meta[0].shape[0]
    xmap = lambda w, tile, ex, lo, hi, fi, ac: (tile[w], 0)
    wmap = lambda w, tile, ex, lo, hi, fi, ac: (ex[w], 0, 0)
    grid_spec = pltpu.PrefetchScalarGridSpec(
        num_scalar_prefetch=6,
        grid=(n_items,),
        in_specs=[pl.BlockSpec((tm, d), xmap),
                  pl.BlockSpec((1, d, f), wmap), pl.BlockSpec((1, d, f), wmap),
                  pl.BlockSpec((1, 1, f), wmap), pl.BlockSpec((1, 1, f), wmap),
                  pl.BlockSpec((1, f, d), wmap), pl.BlockSpec((1, 1, d), wmap)],
        out_specs=pl.BlockSpec((tm, d), xmap),
    )
    return pl.pallas_call(
        functools.partial(_experts_kernel, tm=tm),
        grid_spec=grid_spec,
        out_shape=jax.ShapeDtypeStruct((n, d), F32),
        compiler_params=_cparams("arbitrary"),
        name="experts",
    )(*meta, xs, wg, wu, bg, bu, wd, bd)


def _expert_work_items(counts, n_rows, tm):
    e = counts.shape[0]
    n_tiles = n_rows // tm
    n_items = n_tiles + e - 1
    ends = jnp.cumsum(counts)
    starts = ends - counts
    first_tile = starts // tm
    n_t = jnp.where(counts > 0, (ends - 1) // tm - first_tile + 1, 0)
    item_end = jnp.cumsum(n_t)
    item_start = item_end - n_t
    total = item_end[-1]
    w = jnp.arange(n_items, dtype=I32)
    active = w < total
    w_eff = jnp.minimum(w, total - 1)
    ex = jnp.minimum(jnp.sum((item_end[None, :] <= w_eff[:, None]).astype(I32), axis=1), e - 1)
    tile = (first_tile[ex] + (w_eff - item_start[ex])).astype(I32)
    lo = jnp.clip(starts[ex] - tile * tm, 0, tm).astype(I32)
    hi = jnp.clip(ends[ex] - tile * tm, 0, tm).astype(I32)
    first = jnp.concatenate([jnp.ones((1,), I32), (tile[1:] != tile[:-1]).astype(I32)])
    return tile, ex, lo, hi, first, active.astype(I32)


def _combine_kernel(dest_ref, dest_next_ref, x2_ref, gate_ref, g_ref, y_hbm, o_ref, ybuf, sems, *, tm, n_steps):
    i = pl.program_id(0)
    slot = i % 2

    def gather(dref, into):
        def issue(t, carry):
            for k in range(TOP_K):
                d = dref[0, 0, t * TOP_K + k]
                _row_copy(y_hbm.at[pl.ds(d, 1), :], ybuf.at[into, k, pl.ds(t, 1), :], sems.at[into]).start()
            return carry
        lax.fori_loop(0, tm, issue, 0, unroll=DMA_UNROLL)

    @pl.when(i == 0)
    def _():
        gather(dest_ref, 0)

    @pl.when(i + 1 < n_steps)
    def _():
        gather(dest_next_ref, 1 - slot)

    for k in range(TOP_K):
        _row_copy(y_hbm.at[pl.ds(0, tm), :], ybuf.at[slot, k], sems.at[slot]).wait()

    gates = gate_ref[...]
    acc = x2_ref[...]
    for k in range(TOP_K):
        acc = acc + ybuf[slot, k] * gates[:, k:k + 1]
    o_ref[...] = _rms(acc, g_ref[...])


def _combine(y, dest3, x2, gates, g_final):
    t, d = x2.shape
    tm = min(DMA_TILE, t)
    n_steps = t // tm
    dest_spec = lambda f: pl.BlockSpec((1, 1, tm * TOP_K), f, memory_space=pltpu.SMEM)
    return pl.pallas_call(
        functools.partial(_combine_kernel, tm=tm, n_steps=n_steps),
        grid=(n_steps,),
        in_specs=[dest_spec(lambda i: (i, 0, 0)),
                  dest_spec(lambda i: (jnp.minimum(i + 1, n_steps - 1), 0, 0)),
                  pl.BlockSpec((tm, d), lambda i: (i, 0)),
                  pl.BlockSpec((tm, V7X_LANES), lambda i: (i, 0)),
                  pl.BlockSpec((1, d), lambda i: (0, 0)),
                  pl.BlockSpec(memory_space=pl.ANY)],
        out_specs=pl.BlockSpec((tm, d), lambda i: (i, 0)),
        out_shape=jax.ShapeDtypeStruct((t, d), F32),
        scratch_shapes=[pltpu.VMEM((2, TOP_K, tm, d), F32), pltpu.SemaphoreType.DMA((2,))],
        compiler_params=_cparams("arbitrary"),
        name="combine",
    )(dest3, dest3, x2, gates, g_final, y)


def _alibi_slopes():
    n = 2 * N_HEADS
    s = 2.0 ** (-8.0 * (jnp.arange(n, dtype=F32) + 1.0) / n)
    return s[0::2], s[1::2]


def kernel(x, mem, attn_norm_g, mem_norm_g, w_in, diff_lambda_qk, diff_subln_g, w_mem_kv, w_out,
           ffn_norm_g, w_router, b_router, w_gate_up, b_gate_up, w_down, b_down, final_norm_g):
    b, s, d = x.shape
    t = b * s
    assert w_in.shape[0] == 1, "the final norm is fused into the layer's combine step: one layer only"
    assert s % (DIL_DILATIONS[-1] * DIL_TILE) == 0 and t % TOK_TILE == 0
    n_exp = w_router.shape[-1]
    diff_slopes, dil_slopes = _alibi_slopes()
    x2d = x.reshape(t, d)

    for l in range(1):
        lam_init = 0.8 - 0.6 * math.exp(-0.3 * l)
        mk, mv = _mem_kv(mem, mem_norm_g[l][None], w_mem_kv[l].astype(BF16))
        qd, kd, vd, ql, kl, vl, mq = _in_proj(x2d, attn_norm_g[l][None], w_in[l].astype(BF16))
        sh = lambda a: a.reshape(b, s, a.shape[-1])
        o_diff = _diff_attention(sh(qd), sh(kd), sh(vd), diff_slopes, diff_lambda_qk[l],
                                 diff_subln_g[l][None], lam_init)
        o_dil = _dil_attention(sh(ql), sh(kl), sh(vl), dil_slopes)
        o_mem = _mem_attention(sh(mq), mk, mv)

        w_router_f = jnp.zeros((d, V7X_LANES), F32).at[:, :n_exp].set(w_router[l])
        w_router_hi = w_router_f.astype(BF16)
        w_router_p = jnp.stack([w_router_hi, (w_router_f - w_router_hi.astype(F32)).astype(BF16)])
        b_router_p = jnp.full((1, V7X_LANES), NEG, F32).at[0, :n_exp].set(b_router[l])
        x2, h2, eidx, rank, gates, cnt = _out_router(
            x2d, o_diff.reshape(t, DIFF_WIDTH), o_dil.reshape(t, GROUP_WIDTH), o_mem.reshape(t, GROUP_WIDTH),
            w_out[l].astype(BF16), ffn_norm_g[l][None], w_router_p, b_router_p)

        counts = cnt[0, :n_exp]
        starts = jnp.cumsum(counts) - counts
        e_sel = eidx[:, :TOP_K]
        onehot = e_sel[:, :, None] == jnp.arange(n_exp, dtype=I32)[None, None, :]
        dest = rank[:, :TOP_K] + jnp.sum(jnp.where(onehot, starts[None, None, :], 0), axis=-1)
        tm_dma = min(DMA_TILE, t)
        dest3 = dest.astype(I32).reshape(t // tm_dma, 1, tm_dma * TOP_K)

        xs = _dispatch(h2, dest3)

        f = w_down.shape[2]
        wgu = w_gate_up[l].reshape(n_exp, d, f, 2)
        bgu = b_gate_up[l].reshape(n_exp, 1, f, 2)
        meta = _expert_work_items(counts, t * TOP_K, min(ROW_TILE, t * TOP_K))
        y = _experts(xs, meta, wgu[..., 0].astype(BF16), wgu[..., 1].astype(BF16),
                     bgu[..., 0], bgu[..., 1], w_down[l].astype(BF16), b_down[l][:, None, :])

        out = _combine(y, dest3, x2, gates, final_norm_g[None])
    return out.reshape(b, s, d)
```

```python
import functools
import math

import jax
import jax.numpy as jnp
from jax import lax
from jax.experimental import pallas as pl
from jax.experimental.pallas import tpu as pltpu

F32 = jnp.float32
BF16 = jnp.bfloat16
I32 = jnp.int32

NORM_EPS = 1e-5
HEAD_DIM = 64
N_HEADS = 4
DIFF_VDIM = 2 * HEAD_DIM
DIFF_WIDTH = N_HEADS * DIFF_VDIM
GROUP_WIDTH = N_HEADS * HEAD_DIM
DIL_DILATIONS = (1, 4, 16)
DIL_WINDOW_KEYS = 128
TOP_K = 4
SWIGLU_ALPHA = 1.702
SWIGLU_LIMIT = 7.0
NEG = -1e30

V7X_LANES = 128
V7X_MXU_DIM = 256
V7X_VMEM_LIMIT = 56 * 1024 * 1024

TOK_TILE = 512
ATT_TILE = 512
POS_SPLIT = 64
DIL_TILE = DIL_WINDOW_KEYS
DIL_CHUNK = 256
ROW_TILE = 512
DMA_TILE = 256
DMA_UNROLL = 4


def _cparams(*sem):
    return pltpu.CompilerParams(dimension_semantics=tuple(sem), vmem_limit_bytes=V7X_VMEM_LIMIT)


def _rms(x, g):
    return x * lax.rsqrt(jnp.mean(x * x, axis=-1, keepdims=True) + NORM_EPS) * g


def _dot_nt(a, b):
    return lax.dot_general(a, b, (((1,), (1,)), ((), ())), preferred_element_type=F32)


def _dot(a, b):
    return jnp.dot(a, b, preferred_element_type=F32)


def _memkv_kernel(mem_ref, g_ref, w_ref, mk_ref, mv_ref):
    h = _rms(mem_ref[0], g_ref[...]).astype(BF16)
    kv = _dot(h, w_ref[...])
    mk_ref[0] = kv[:, :GROUP_WIDTH].astype(BF16)
    mv_ref[0] = kv[:, GROUP_WIDTH:].astype(BF16)


def _mem_kv(mem, g, w):
    b, m, d = mem.shape
    out = jax.ShapeDtypeStruct((b, m, GROUP_WIDTH), BF16)
    return pl.pallas_call(
        _memkv_kernel,
        grid=(b,),
        in_specs=[pl.BlockSpec((1, m, d), lambda i: (i, 0, 0)),
                  pl.BlockSpec((1, d), lambda i: (0, 0)),
                  pl.BlockSpec((d, 2 * GROUP_WIDTH), lambda i: (0, 0))],
        out_specs=[pl.BlockSpec((1, m, GROUP_WIDTH), lambda i: (i, 0, 0))] * 2,
        out_shape=[out, out],
        compiler_params=_cparams("arbitrary"),
        name="mem_kv",
    )(mem, g, w)


_IN_SEGMENTS = ((DIFF_WIDTH, HEAD_DIM ** -0.5), (DIFF_WIDTH, 1.0), (DIFF_WIDTH, 1.0),
                (GROUP_WIDTH, HEAD_DIM ** -0.5), (GROUP_WIDTH, 1.0), (GROUP_WIDTH, 1.0),
                (GROUP_WIDTH, HEAD_DIM ** -0.5))


def _inproj_kernel(x_ref, g_ref, w_ref, *out_refs):
    h = _rms(x_ref[...], g_ref[...]).astype(BF16)
    c0 = 0
    for ref, (n, scale) in zip(out_refs, _IN_SEGMENTS):
        p = _dot(h, w_ref[:, c0:c0 + n])
        if scale != 1.0:
            p = p * scale
        ref[...] = p.astype(BF16)
        c0 += n


def _in_proj(x2d, g, w):
    t, d = x2d.shape
    tm = min(TOK_TILE, t)
    return pl.pallas_call(
        _inproj_kernel,
        grid=(t // tm,),
        in_specs=[pl.BlockSpec((tm, d), lambda i: (i, 0)),
                  pl.BlockSpec((1, d), lambda i: (0, 0)),
                  pl.BlockSpec(w.shape, lambda i: (0, 0))],
        out_specs=[pl.BlockSpec((tm, n), lambda i: (i, 0)) for n, _ in _IN_SEGMENTS],
        out_shape=[jax.ShapeDtypeStruct((t, n), BF16) for n, _ in _IN_SEGMENTS],
        compiler_params=_cparams("arbitrary"),
        name="in_proj",
    )(x2d, g, w)


def _diff_kernel(slopes_ref, lam_ref, g_ref, q_ref, k_ref, v_ref, kpos_ref, ones_ref, o_ref,
                 m_ref, a_ref, s_ref, *, lam_init, tile):
    h = pl.program_id(1)
    n_full = 2 * pl.program_id(2)
    slope = slopes_ref[h]

    lane = lax.broadcasted_iota(I32, (tile, DIFF_VDIM), 1)
    qpos = jnp.where(lane == 0, slope * float(POS_SPLIT), jnp.where(lane == 1, slope, 0.0)).astype(BF16)
    qs = []
    for sub in range(2):
        q = q_ref[0, sub * tile:(sub + 1) * tile, :]
        zero = jnp.zeros_like(q)
        qs.append(jnp.concatenate([jnp.where(lane < HEAD_DIM, q, zero), qpos], axis=1))
        qs.append(jnp.concatenate([jnp.where(lane >= HEAD_DIM, q, zero), qpos], axis=1))
    sub_a, sub_b, both = (0, 1), (2, 3), (0, 1, 2, 3)

    m_ref[...] = jnp.full(m_ref.shape, NEG, F32)
    a_ref[...] = jnp.zeros(a_ref.shape, F32)

    def scores(j, slot, chains):
        start = pl.multiple_of(j * tile, tile)
        kk = jnp.concatenate([k_ref[0, pl.ds(start, tile), :], kpos_ref[pl.ds(start, tile), :]], axis=1)
        for c in chains:
            s_ref[slot, c] = _dot_nt(qs[c], kk)

    def consume(j, slot, chains, diagonal):
        start = pl.multiple_of(j * tile, tile)
        vv = jnp.concatenate([v_ref[0, pl.ds(start, tile), :], ones_ref[...]], axis=1)
        for c in chains:
            s = s_ref[slot, c]
            if diagonal:
                row = lax.broadcasted_iota(I32, s.shape, 0)
                col = lax.broadcasted_iota(I32, s.shape, 1)
                s = jnp.where(col <= row, s, NEG)
            m_old = m_ref[c]
            m_new = jnp.maximum(m_old, jnp.max(s, axis=1, keepdims=True))
            p = jnp.exp(s - m_new)
            a_ref[c] = jnp.exp(m_old - m_new) * a_ref[c] + _dot(p.astype(BF16), vv)
            m_ref[c] = m_new

    scores(0, 0, both)

    def pair(p, carry):
        j = 2 * p
        scores(j + 1, 1, both)
        consume(j, 0, both, False)
        scores(j + 2, 0, both)
        consume(j + 1, 1, both, False)
        return carry

    lax.fori_loop(0, n_full // 2, pair, 0)
    scores(n_full + 1, 1, sub_b)
    consume(n_full, 0, sub_a, True)
    consume(n_full, 0, sub_b, False)
    consume(n_full + 1, 1, sub_b, True)

    lv = lam_ref[...]
    lam = (jnp.exp(jnp.sum(lv[0:1] * lv[1:2], axis=1, keepdims=True))
           - jnp.exp(jnp.sum(lv[2:3] * lv[3:4], axis=1, keepdims=True)) + lam_init)
    n = DIFF_VDIM
    for sub in range(2):
        a1, a2 = a_ref[2 * sub], a_ref[2 * sub + 1]
        o = (a1[:, :n] * (1.0 / a1[:, n:n + 1])) - lam * (a2[:, :n] * (1.0 / a2[:, n:n + 1]))
        o = _rms(o, g_ref[...]) * (1.0 - lam_init)
        o_ref[0, sub * tile:(sub + 1) * tile, :] = o.astype(o_ref.dtype)


def _diff_attention(qd, kd, vd, slopes, lam_qk, subln_g, lam_init):
    b, s, _ = qd.shape
    tile = ATT_TILE
    assert s % (2 * tile) == 0
    pos = jnp.arange(s, dtype=I32)[:, None]
    lane = jnp.arange(V7X_LANES, dtype=I32)[None, :]
    kpos = jnp.where(lane == 0, pos // POS_SPLIT, jnp.where(lane == 1, pos % POS_SPLIT, 0)).astype(BF16)
    ones = jnp.broadcast_to(lane == 0, (tile, V7X_LANES)).astype(BF16)
    vec = lambda shape: pltpu.VMEM(shape, F32)
    const = lambda a: pl.BlockSpec(a.shape, lambda bb, hh, ii: (0, 0))
    return pl.pallas_call(
        functools.partial(_diff_kernel, lam_init=lam_init, tile=tile),
        grid=(b, N_HEADS, s // (2 * tile)),
        in_specs=[pl.BlockSpec(memory_space=pltpu.SMEM), const(lam_qk), const(subln_g),
                  pl.BlockSpec((1, 2 * tile, DIFF_VDIM), lambda bb, hh, ii: (bb, ii, hh)),
                  pl.BlockSpec((1, s, DIFF_VDIM), lambda bb, hh, ii: (bb, 0, hh)),
                  pl.BlockSpec((1, s, DIFF_VDIM), lambda bb, hh, ii: (bb, 0, hh)),
                  const(kpos), const(ones)],
        out_specs=pl.BlockSpec((1, 2 * tile, DIFF_VDIM), lambda bb, hh, ii: (bb, ii, hh)),
        out_shape=jax.ShapeDtypeStruct((b, s, DIFF_WIDTH), BF16),
        scratch_shapes=[vec((4, tile, 1)), vec((4, tile, 2 * DIFF_VDIM)), vec((2, 4, tile, tile))],
        compiler_params=_cparams("arbitrary", "arbitrary", "arbitrary"),
        name="diff_attn",
    )(slopes, lam_qk, subln_g, qd, kd, vd, kpos, ones)


def _head_mask(lane, h):
    return (lane >= h * HEAD_DIM) & (lane < (h + 1) * HEAD_DIM)


def _split3(x):
    hi = x.astype(BF16)
    rem = x - hi.astype(F32)
    mid = rem.astype(BF16)
    return hi, mid, (rem - mid.astype(F32)).astype(BF16)


def _dil_kernel(q_ref, k_ref, v_ref, perm_ref, bias_ref, spread_ref, o_ref, qc, kc, vc, res_o, res_l, *, seq):
    tile, chunk = DIL_TILE, DIL_CHUNK
    n_blocks, n_chunks = seq // tile, seq // chunk
    lane = lax.broadcasted_iota(I32, (tile, GROUP_WIDTH), 1)
    tail_lane = lax.broadcasted_iota(I32, (tile, V7X_LANES), 1)

    def to_class_order(bi, r):
        n, sr = chunk // r, seq // r

        def one_chunk(t, carry):
            src0 = pl.multiple_of(t * chunk, chunk)
            for src, dst in ((q_ref, qc), (k_ref, kc), (v_ref, vc)):
                y = _dot(perm_ref[bi - 1, 0], src[0, pl.ds(src0, chunk), :]).astype(BF16)
                for rho in range(r):
                    dst[pl.ds(pl.multiple_of(rho * sr + t * n, n), n), :] = y[rho * n:(rho + 1) * n, :]
            return carry

        lax.fori_loop(0, n_chunks, one_chunk, 0)

    def attend(bi, r, load):
        blocks_per_class = (seq // r) // tile

        def one_block(g, carry):
            cur = pl.multiple_of(g * tile, tile)
            prev = pl.multiple_of(jnp.maximum(g - 1, 0) * tile, tile)
            has_prev = (g % blocks_per_class) != 0
            q = load(0, cur)
            qs = jnp.concatenate([jnp.where(_head_mask(lane, h), q, jnp.zeros_like(q)) for h in range(N_HEADS)], axis=0)
            kk = jnp.concatenate([load(1, prev), load(1, cur)], axis=0)
            vv = jnp.concatenate([load(2, prev), load(2, cur)], axis=0)
            bias = jnp.where(has_prev, bias_ref[bi, 1], bias_ref[bi, 0])
            s = _dot_nt(qs, kk) + bias
            m = jnp.max(s, axis=1, keepdims=True)
            p = jnp.exp(s - m)
            l = jnp.sum(p, axis=1, keepdims=True)
            pv = _dot(p.astype(BF16), vv) * (1.0 / l)
            lse = m + jnp.log(l)
            out = jnp.zeros((tile, GROUP_WIDTH), F32)
            tail = jnp.zeros((tile, V7X_LANES), F32)
            for h in range(N_HEADS):
                rows = slice(h * tile, (h + 1) * tile)
                out = jnp.where(_head_mask(lane, h), pv[rows], out)
                tail = jnp.where(tail_lane == h, lse[rows], tail)
            res_o[bi, pl.ds(cur, tile), :] = out.astype(BF16)
            res_l[bi, pl.ds(cur, tile), :] = tail
            return carry

        lax.fori_loop(0, n_blocks, one_block, 0, unroll=4)

    refs = (q_ref, k_ref, v_ref)
    scratch = (qc, kc, vc)
    attend(0, 1, lambda which, start: refs[which][0, pl.ds(start, tile), :])
    for bi, r in enumerate(DIL_DILATIONS[1:], start=1):
        to_class_order(bi, r)
        attend(bi, r, lambda which, start: scratch[which][pl.ds(start, tile), :])

    def merge_chunk(t, carry):
        base = pl.multiple_of(t * chunk, chunk)
        def pack3(lse):
            hi, mid, lo = _split3(lse)
            return (hi.astype(F32) + pltpu.roll(mid.astype(F32), N_HEADS, 1)
                    + pltpu.roll(lo.astype(F32), 2 * N_HEADS, 1)).astype(BF16)

        outs, lses = [], []
        for bi, r in enumerate(DIL_DILATIONS):
            if r == 1:
                out = res_o[0, pl.ds(base, chunk), :].astype(F32)
                packed = pack3(res_l[0, pl.ds(base, chunk), :])
            else:
                n, sr = chunk // r, seq // r
                rows = [pl.ds(pl.multiple_of(rho * sr + t * n, n), n) for rho in range(r)]
                back = perm_ref[bi - 1, 1]
                out = _dot(back, jnp.concatenate([res_o[bi, rw, :] for rw in rows], axis=0))
                packed = _dot(back, pack3(jnp.concatenate([res_l[bi, rw, :] for rw in rows], axis=0))).astype(BF16)
            outs.append(out)
            lses.append(_dot(packed, spread_ref[...]))
        big = jnp.maximum(jnp.maximum(lses[0], lses[1]), lses[2])
        wts = [jnp.exp(z - big) for z in lses]
        o = (wts[0] * outs[0] + wts[1] * outs[1] + wts[2] * outs[2]) * (1.0 / (wts[0] + wts[1] + wts[2]))
        o_ref[0, pl.ds(base, chunk), :] = o.astype(o_ref.dtype)
        return carry

    lax.fori_loop(0, n_chunks, merge_chunk, 0)


def _dil_constants(slopes, seq):
    chunk, tile = DIL_CHUNK, DIL_TILE
    perms = []
    for r in DIL_DILATIONS[1:]:
        n = chunk // r
        dst = jnp.arange(chunk, dtype=I32)
        src = r * (dst % n) + dst // n
        fwd = (src[:, None] == jnp.arange(chunk, dtype=I32)[None, :]).astype(BF16)
        perms.append(jnp.stack([fwd, fwd.T]))
    m = jnp.arange(tile, dtype=I32)[:, None]
    j = jnp.arange(2 * tile, dtype=I32)[None, :]
    dist = jnp.where(j < tile, m - j + tile, m - (j - tile))
    ok_cur = (j >= tile) & (dist >= 0)
    ok_prev = (j < tile) & (dist <= DIL_WINDOW_KEYS)
    biases = []
    for r in DIL_DILATIONS:
        per_head = [-(slopes[h] * r) * dist.astype(F32) for h in range(N_HEADS)]
        with_prev = jnp.concatenate([jnp.where(ok_cur | ok_prev, b, NEG) for b in per_head], axis=0)
        no_prev = jnp.concatenate([jnp.where(ok_cur, b, NEG) for b in per_head], axis=0)
        biases.append(jnp.stack([no_prev, with_prev]))
    src_lane = jnp.arange(V7X_LANES, dtype=I32)[:, None]
    spread = ((src_lane < 3 * N_HEADS)
              & (src_lane % N_HEADS == jnp.arange(GROUP_WIDTH, dtype=I32)[None, :] // HEAD_DIM)).astype(BF16)
    return jnp.stack(perms), jnp.stack(biases), spread


def _dil_attention(ql, kl, vl, slopes):
    b, s, w = ql.shape
    perms, biases, spread = _dil_constants(slopes, s)
    row = pl.BlockSpec((1, s, w), lambda i: (i, 0, 0))
    const = lambda a: pl.BlockSpec(a.shape, lambda i: (0,) * a.ndim)
    return pl.pallas_call(
        functools.partial(_dil_kernel, seq=s),
        grid=(b,),
        in_specs=[row, row, row, const(perms), const(biases), const(spread)],
        out_specs=row,
        out_shape=jax.ShapeDtypeStruct((b, s, w), BF16),
        scratch_shapes=[pltpu.VMEM((s, w), BF16)] * 3
                       + [pltpu.VMEM((len(DIL_DILATIONS), s, w), BF16),
                          pltpu.VMEM((len(DIL_DILATIONS), s, V7X_LANES), F32)],
        compiler_params=_cparams("arbitrary"),
        name="dil_attn",
    )(ql, kl, vl, perms, biases, spread)


def _memattn_kernel(q_ref, mk_ref, mv_ref, o_ref):
    q = q_ref[0]
    mk, mv = mk_ref[0], mv_ref[0]
    lane = lax.broadcasted_iota(I32, q.shape, 1)
    out = jnp.zeros(q.shape, F32)
    for h in range(N_HEADS):
        hm = _head_mask(lane, h)
        s = _dot_nt(jnp.where(hm, q, jnp.zeros_like(q)), mk)
        m = jnp.max(s, axis=1, keepdims=True)
        p = jnp.exp(s - m)
        l = jnp.sum(p, axis=1, keepdims=True)
        out = jnp.where(hm, _dot(p.astype(BF16), mv) * (1.0 / l), out)
    o_ref[0] = out.astype(o_ref.dtype)


def _mem_attention(mq, mk, mv):
    b, s, w = mq.shape
    m = mk.shape[1]
    tile = min(TOK_TILE, s)
    return pl.pallas_call(
        _memattn_kernel,
        grid=(b, s // tile),
        in_specs=[pl.BlockSpec((1, tile, w), lambda bb, ii: (bb, ii, 0)),
                  pl.BlockSpec((1, m, w), lambda bb, ii: (bb, 0, 0)),
                  pl.BlockSpec((1, m, w), lambda bb, ii: (bb, 0, 0))],
        out_specs=pl.BlockSpec((1, tile, w), lambda bb, ii: (bb, ii, 0)),
        out_shape=jax.ShapeDtypeStruct((b, s, w), BF16),
        compiler_params=_cparams("arbitrary", "arbitrary"),
        name="mem_attn",
    )(mq, mk, mv)


def _outrouter_kernel(x_ref, od_ref, ol_ref, om_ref, wo_ref, g_ref, wr_ref, br_ref,
                      x2_ref, h2_ref, eidx_ref, rank_ref, gate_ref, cnt_ref, run_ref, *, tm):
    step = pl.program_id(0)

    @pl.when(step == 0)
    def _():
        run_ref[...] = jnp.zeros(run_ref.shape, F32)

    c1, c2 = DIFF_WIDTH, DIFF_WIDTH + GROUP_WIDTH
    mix = (_dot(od_ref[...], wo_ref[0:c1, :]) + _dot(ol_ref[...], wo_ref[c1:c2, :])
           + _dot(om_ref[...], wo_ref[c2:, :]))
    x2 = x_ref[...] + mix
    x2_ref[...] = x2
    h2 = _rms(x2, g_ref[...])
    h2_ref[...] = h2

    h_hi = h2.astype(BF16)
    h_lo = (h2 - h_hi.astype(F32)).astype(BF16)
    logits = (_dot(h_hi, wr_ref[0]) + (_dot(h_hi, wr_ref[1]) + _dot(h_lo, wr_ref[0]))) + br_ref[...]
    lane = lax.broadcasted_iota(I32, logits.shape, 1)
    work = logits
    sel = jnp.zeros(logits.shape, F32)
    vals, idxs = [], []
    for _ in range(TOP_K):
        mx = jnp.max(work, axis=1, keepdims=True)
        idx = jnp.min(jnp.where(work == mx, lane, V7X_LANES), axis=1, keepdims=True)
        hit = lane == idx
        sel = jnp.where(hit, 1.0, sel)
        work = jnp.where(hit, -jnp.inf, work)
        vals.append(mx)
        idxs.append(idx)

    r_io = lax.broadcasted_iota(I32, (tm, tm), 0)
    c_io = lax.broadcasted_iota(I32, (tm, tm), 1)
    tri = jnp.where(c_io < r_io, 1.0, 0.0).astype(BF16)
    before = _dot(tri, sel.astype(BF16)) + run_ref[...]
    run_ref[...] = run_ref[...] + jnp.sum(sel, axis=0, keepdims=True)
    cnt_ref[...] = run_ref[...].astype(I32)

    exps = [jnp.exp(v - vals[0]) for v in vals]
    inv = 1.0 / (exps[0] + exps[1] + exps[2] + exps[3])
    eidx = jnp.zeros(logits.shape, I32)
    rank = jnp.zeros(logits.shape, I32)
    gate = jnp.zeros(logits.shape, F32)
    for k in range(TOP_K):
        rk = jnp.sum(jnp.where(lane == idxs[k], before, 0.0), axis=1, keepdims=True)
        eidx = jnp.where(lane == k, idxs[k], eidx)
        rank = jnp.where(lane == k, rk.astype(I32), rank)
        gate = jnp.where(lane == k, exps[k] * inv, gate)
    eidx_ref[...] = eidx
    rank_ref[...] = rank
    gate_ref[...] = gate


def _out_router(x2d, o_diff, o_dil, o_mem, w_out, g, w_router_p, b_router_p):
    t, d = x2d.shape
    tm = min(TOK_TILE, t)
    row = lambda n: pl.BlockSpec((tm, n), lambda i: (i, 0))
    full = lambda a: pl.BlockSpec(a.shape, lambda i: (0,) * a.ndim)
    return pl.pallas_call(
        functools.partial(_outrouter_kernel, tm=tm),
        grid=(t // tm,),
        in_specs=[row(d), row(DIFF_WIDTH), row(GROUP_WIDTH), row(GROUP_WIDTH),
                  full(w_out), full(g), full(w_router_p), full(b_router_p)],
        out_specs=[row(d), row(d), row(V7X_LANES), row(V7X_LANES), row(V7X_LANES),
                   pl.BlockSpec((1, V7X_LANES), lambda i: (0, 0))],
        out_shape=[jax.ShapeDtypeStruct((t, d), F32), jax.ShapeDtypeStruct((t, d), F32),
                   jax.ShapeDtypeStruct((t, V7X_LANES), I32), jax.ShapeDtypeStruct((t, V7X_LANES), I32),
                   jax.ShapeDtypeStruct((t, V7X_LANES), F32), jax.ShapeDtypeStruct((1, V7X_LANES), I32)],
        scratch_shapes=[pltpu.VMEM((1, V7X_LANES), F32)],
        compiler_params=_cparams("arbitrary"),
        name="out_router",
    )(x2d, o_diff, o_dil, o_mem, w_out, g, w_router_p, b_router_p)


def _row_copy(src, dst, sem):
    return pltpu.make_async_copy(src, dst, sem)


def _dispatch_kernel(dest_ref, h_ref, xs_hbm, sem, *, tm):
    def issue(t, carry):
        for k in range(TOP_K):
            d = dest_ref[0, 0, t * TOP_K + k]
            _row_copy(h_ref.at[pl.ds(t, 1), :], xs_hbm.at[pl.ds(d, 1), :], sem).start()
        return carry

    lax.fori_loop(0, tm, issue, 0, unroll=DMA_UNROLL)
    for _ in range(TOP_K):
        _row_copy(h_ref, xs_hbm.at[pl.ds(0, tm), :], sem).wait()


def _dispatch(h2, dest3):
    t, d = h2.shape
    tm = min(DMA_TILE, t)
    return pl.pallas_call(
        functools.partial(_dispatch_kernel, tm=tm),
        grid=(t // tm,),
        in_specs=[pl.BlockSpec((1, 1, tm * TOP_K), lambda i: (i, 0, 0), memory_space=pltpu.SMEM),
                  pl.BlockSpec((tm, d), lambda i: (i, 0))],
        out_specs=pl.BlockSpec(memory_space=pl.ANY),
        out_shape=jax.ShapeDtypeStruct((t * TOP_K, d), h2.dtype),
        scratch_shapes=[pltpu.SemaphoreType.DMA],
        compiler_params=_cparams("arbitrary"),
        name="dispatch",
    )(dest3, h2)


def _split_gate_up_kernel(w_ref, sel_ref, wg_ref, wu_ref):
    x = w_ref[0].astype(BF16)
    k, n = sel_ref.shape[1], sel_ref.shape[2]
    for j in range(wg_ref.shape[2] // n):
        blk = x[:, j * k:(j + 1) * k]
        wg_ref[0, :, j * n:(j + 1) * n] = _dot(blk, sel_ref[0]).astype(BF16)
        wu_ref[0, :, j * n:(j + 1) * n] = _dot(blk, sel_ref[1]).astype(BF16)


def _split_gate_up(w):
    e, d, f2 = w.shape
    f = f2 // 2
    rows, n = min(TOK_TILE, d), V7X_MXU_DIM
    src = jnp.arange(2 * n, dtype=I32)[:, None]
    dst = jnp.arange(n, dtype=I32)[None, :]
    sel = jnp.stack([src == 2 * dst, src == 2 * dst + 1]).astype(BF16)
    out = jax.ShapeDtypeStruct((e, d, f), BF16)
    return pl.pallas_call(
        _split_gate_up_kernel,
        grid=(e, d // rows),
        in_specs=[pl.BlockSpec((1, rows, f2), lambda i, j: (i, j, 0)),
                  pl.BlockSpec(sel.shape, lambda i, j: (0, 0, 0))],
        out_specs=[pl.BlockSpec((1, rows, f), lambda i, j: (i, j, 0))] * 2,
        out_shape=[out, out],
        compiler_params=_cparams("arbitrary", "arbitrary"),
        name="split_gate_up",
    )(w, sel)


def _experts_kernel(tile_ref, exp_ref, lo_ref, hi_ref, first_ref, act_ref,
                    xs_ref, wg_ref, wu_ref, bg_ref, bu_ref, wd_ref, bd_ref, y_ref, *, tm):
    w = pl.program_id(0)

    @pl.when(act_ref[w] == 1)
    def _():
        x = xs_ref[...].astype(BF16)
        gate = jnp.minimum(_dot(x, wg_ref[0]) + bg_ref[0], SWIGLU_LIMIT)
        up = jnp.clip(_dot(x, wu_ref[0]) + bu_ref[0], -SWIGLU_LIMIT, SWIGLU_LIMIT)
        act = (up + 1.0) * (gate * jax.nn.sigmoid(gate * SWIGLU_ALPHA))
        y = _dot(act.astype(BF16), wd_ref[0]) + bd_ref[0]
        row = lax.broadcasted_iota(I32, y.shape, 0)
        mine = (row >= lo_ref[w]) & (row < hi_ref[w])

        @pl.when(first_ref[w] == 1)
        def _():
            y_ref[...] = jnp.where(mine, y, 0.0)

        @pl.when(first_ref[w] == 0)
        def _():
            y_ref[...] = jnp.where(mine, y, y_ref[...])


def _experts(xs, meta, wg, wu, bg, bu, wd, bd):
    n, d = xs.shape
    e, _, f = wg.shape
    tm = min(ROW_TILE, n)
    n_items = meta[0].shape[0]
    xmap = lambda w, tile, ex, lo, hi, fi, ac: (tile[w], 0)
    wmap = lambda w, tile, ex, lo, hi, fi, ac: (ex[w], 0, 0)
    grid_spec = pltpu.PrefetchScalarGridSpec(
        num_scalar_prefetch=6,
        grid=(n_items,),
        in_specs=[pl.BlockSpec((tm, d), xmap),
                  pl.BlockSpec((1, d, f), wmap), pl.BlockSpec((1, d, f), wmap),
                  pl.BlockSpec((1, 1, f), wmap), pl.BlockSpec((1, 1, f), wmap),
                  pl.BlockSpec((1, f, d), wmap), pl.BlockSpec((1, 1, d), wmap)],
        out_specs=pl.BlockSpec((tm, d), xmap),
    )
    return pl.pallas_call(
        functools.partial(_experts_kernel, tm=tm),
        grid_spec=grid_spec,
        out_shape=jax.ShapeDtypeStruct((n, d), F32),
        compiler_params=_cparams("arbitrary"),
        name="experts",
    )(*meta, xs, wg, wu, bg, bu, wd, bd)


def _expert_work_items(counts, n_rows, tm):
    e = counts.shape[0]
    n_tiles = n_rows // tm
    n_items = n_tiles + e - 1
    ends = jnp.cumsum(counts)
    starts = ends - counts
    first_tile = starts // tm
    n_t = jnp.where(counts > 0, (ends - 1) // tm - first_tile + 1, 0)
    item_end = jnp.cumsum(n_t)
    item_start = item_end - n_t
    total = item_end[-1]
    w = jnp.arange(n_items, dtype=I32)
    active = w < total
    w_eff = jnp.minimum(w, total - 1)
    ex = jnp.minimum(jnp.sum((item_end[None, :] <= w_eff[:, None]).astype(I32), axis=1), e - 1)
    tile = (first_tile[ex] + (w_eff - item_start[ex])).astype(I32)
    lo = jnp.clip(starts[ex] - tile * tm, 0, tm).astype(I32)
    hi = jnp.clip(ends[ex] - tile * tm, 0, tm).astype(I32)
    first = jnp.concatenate([jnp.ones((1,), I32), (tile[1:] != tile[:-1]).astype(I32)])
    return tile, ex, lo, hi, first, active.astype(I32)


def _combine_kernel(dest_ref, dest_next_ref, x2_ref, gate_ref, g_ref, y_hbm, o_ref, ybuf, sems, *, tm, n_steps):
    i = pl.program_id(0)
    slot = i % 2

    def gather(dref, into):
        def issue(t, carry):
            for k in range(TOP_K):
                d = dref[0, 0, t * TOP_K + k]
                _row_copy(y_hbm.at[pl.ds(d, 1), :], ybuf.at[into, k, pl.ds(t, 1), :], sems.at[into]).start()
            return carry
        lax.fori_loop(0, tm, issue, 0, unroll=DMA_UNROLL)

    @pl.when(i == 0)
    def _():
        gather(dest_ref, 0)

    @pl.when(i + 1 < n_steps)
    def _():
        gather(dest_next_ref, 1 - slot)

    for k in range(TOP_K):
        _row_copy(y_hbm.at[pl.ds(0, tm), :], ybuf.at[slot, k], sems.at[slot]).wait()

    gates = gate_ref[...]
    acc = x2_ref[...]
    for k in range(TOP_K):
        acc = acc + ybuf[slot, k] * gates[:, k:k + 1]
    o_ref[...] = _rms(acc, g_ref[...])


def _combine(y, dest3, x2, gates, g_final):
    t, d = x2.shape
    tm = min(DMA_TILE, t)
    n_steps = t // tm
    dest_spec = lambda f: pl.BlockSpec((1, 1, tm * TOP_K), f, memory_space=pltpu.SMEM)
    return pl.pallas_call(
        functools.partial(_combine_kernel, tm=tm, n_steps=n_steps),
        grid=(n_steps,),
        in_specs=[dest_spec(lambda i: (i, 0, 0)),
                  dest_spec(lambda i: (jnp.minimum(i + 1, n_steps - 1), 0, 0)),
                  pl.BlockSpec((tm, d), lambda i: (i, 0)),
                  pl.BlockSpec((tm, V7X_LANES), lambda i: (i, 0)),
                  pl.BlockSpec((1, d), lambda i: (0, 0)),
                  pl.BlockSpec(memory_space=pl.ANY)],
        out_specs=pl.BlockSpec((tm, d), lambda i: (i, 0)),
        out_shape=jax.ShapeDtypeStruct((t, d), F32),
        scratch_shapes=[pltpu.VMEM((2, TOP_K, tm, d), F32), pltpu.SemaphoreType.DMA((2,))],
        compiler_params=_cparams("arbitrary"),
        name="combine",
    )(dest3, dest3, x2, gates, g_final, y)


def _alibi_slopes():
    n = 2 * N_HEADS
    s = 2.0 ** (-8.0 * (jnp.arange(n, dtype=F32) + 1.0) / n)
    return s[0::2], s[1::2]


def kernel(x, mem, attn_norm_g, mem_norm_g, w_in, diff_lambda_qk, diff_subln_g, w_mem_kv, w_out,
           ffn_norm_g, w_router, b_router, w_gate_up, b_gate_up, w_down, b_down, final_norm_g):
    b, s, d = x.shape
    t = b * s
    assert w_in.shape[0] == 1, "the final norm is fused into the layer's combine step: one layer only"
    assert s % (DIL_DILATIONS[-1] * DIL_TILE) == 0 and t % TOK_TILE == 0
    n_exp = w_router.shape[-1]
    diff_slopes, dil_slopes = _alibi_slopes()
    x2d = x.reshape(t, d)

    for l in range(1):
        lam_init = 0.8 - 0.6 * math.exp(-0.3 * l)
        mk, mv = _mem_kv(mem, mem_norm_g[l][None], w_mem_kv[l].astype(BF16))
        qd, kd, vd, ql, kl, vl, mq = _in_proj(x2d, attn_norm_g[l][None], w_in[l].astype(BF16))
        sh = lambda a: a.reshape(b, s, a.shape[-1])
        o_diff = _diff_attention(sh(qd), sh(kd), sh(vd), diff_slopes, diff_lambda_qk[l],
                                 diff_subln_g[l][None], lam_init)
        o_dil = _dil_attention(sh(ql), sh(kl), sh(vl), dil_slopes)
        o_mem = _mem_attention(sh(mq), mk, mv)

        w_router_f = jnp.zeros((d, V7X_LANES), F32).at[:, :n_exp].set(w_router[l])
        w_router_hi = w_router_f.astype(BF16)
        w_router_p = jnp.stack([w_router_hi, (w_router_f - w_router_hi.astype(F32)).astype(BF16)])
        b_router_p = jnp.full((1, V7X_LANES), NEG, F32).at[0, :n_exp].set(b_router[l])
        x2, h2, eidx, rank, gates, cnt = _out_router(
            x2d, o_diff.reshape(t, DIFF_WIDTH), o_dil.reshape(t, GROUP_WIDTH), o_mem.reshape(t, GROUP_WIDTH),
            w_out[l].astype(BF16), ffn_norm_g[l][None], w_router_p, b_router_p)

        counts = cnt[0, :n_exp]
        starts = jnp.cumsum(counts) - counts
        e_sel = eidx[:, :TOP_K]
        onehot = e_sel[:, :, None] == jnp.arange(n_exp, dtype=I32)[None, None, :]
        dest = rank[:, :TOP_K] + jnp.sum(jnp.where(onehot, starts[None, None, :], 0), axis=-1)
        tm_dma = min(DMA_TILE, t)
        dest3 = dest.astype(I32).reshape(t // tm_dma, 1, tm_dma * TOP_K)

        xs = _dispatch(h2, dest3)

        f = w_down.shape[2]
        w_gate, w_up = _split_gate_up(w_gate_up[l])
        bgu = b_gate_up[l].reshape(n_exp, 1, f, 2)
        meta = _expert_work_items(counts, t * TOP_K, min(ROW_TILE, t * TOP_K))
        y = _experts(xs, meta, w_gate, w_up, bgu[..., 0], bgu[..., 1], w_down[l].astype(BF16), b_down[l][:, None, :])

        out = _combine(y, dest3, x2, gates, final_norm_g[None])
    return out.reshape(b, s, d)
```

```python
import functools
import math

import jax
import jax.numpy as jnp
from jax import lax
from jax.experimental import pallas as pl
from jax.experimental.pallas import tpu as pltpu

F32 = jnp.float32
BF16 = jnp.bfloat16
I32 = jnp.int32

NORM_EPS = 1e-5
HEAD_DIM = 64
N_HEADS = 4
DIFF_VDIM = 2 * HEAD_DIM
DIFF_WIDTH = N_HEADS * DIFF_VDIM
GROUP_WIDTH = N_HEADS * HEAD_DIM
DIL_DILATIONS = (1, 4, 16)
DIL_WINDOW_KEYS = 128
TOP_K = 4
SWIGLU_ALPHA = 1.702
SWIGLU_LIMIT = 7.0
NEG = -1e30

V7X_LANES = 128
V7X_MXU_DIM = 256
V7X_VMEM_LIMIT = 56 * 1024 * 1024

TOK_TILE = 512
ATT_TILE = 512
POS_SPLIT = 64
DIL_TILE = DIL_WINDOW_KEYS
DIL_CHUNK = 256
ROW_TILE = 512
DMA_TILE = 256
DISPATCH_TILE = 1024
DMA_UNROLL = 4


def _cparams(*sem):
    return pltpu.CompilerParams(dimension_semantics=tuple(sem), vmem_limit_bytes=V7X_VMEM_LIMIT)


def _rms(x, g):
    return x * lax.rsqrt(jnp.mean(x * x, axis=-1, keepdims=True) + NORM_EPS) * g


def _dot_nt(a, b):
    return lax.dot_general(a, b, (((1,), (1,)), ((), ())), preferred_element_type=F32)


def _dot(a, b):
    return jnp.dot(a, b, preferred_element_type=F32)


def _memkv_kernel(mem_ref, g_ref, w_ref, mk_ref, mv_ref):
    h = _rms(mem_ref[0], g_ref[...]).astype(BF16)
    kv = _dot(h, w_ref[...])
    mk_ref[0] = kv[:, :GROUP_WIDTH].astype(BF16)
    mv_ref[0] = kv[:, GROUP_WIDTH:].astype(BF16)


def _mem_kv(mem, g, w):
    b, m, d = mem.shape
    out = jax.ShapeDtypeStruct((b, m, GROUP_WIDTH), BF16)
    return pl.pallas_call(
        _memkv_kernel,
        grid=(b,),
        in_specs=[pl.BlockSpec((1, m, d), lambda i: (i, 0, 0)),
                  pl.BlockSpec((1, d), lambda i: (0, 0)),
                  pl.BlockSpec((d, 2 * GROUP_WIDTH), lambda i: (0, 0))],
        out_specs=[pl.BlockSpec((1, m, GROUP_WIDTH), lambda i: (i, 0, 0))] * 2,
        out_shape=[out, out],
        compiler_params=_cparams("arbitrary"),
        name="mem_kv",
    )(mem, g, w)


_IN_SEGMENTS = ((DIFF_WIDTH, HEAD_DIM ** -0.5), (DIFF_WIDTH, 1.0), (DIFF_WIDTH, 1.0),
                (GROUP_WIDTH, HEAD_DIM ** -0.5), (GROUP_WIDTH, 1.0), (GROUP_WIDTH, 1.0),
                (GROUP_WIDTH, HEAD_DIM ** -0.5))


def _inproj_kernel(x_ref, g_ref, w_ref, *refs):
    out_refs, w_bf16 = refs[:-1], refs[-1]

    @pl.when(pl.program_id(0) == 0)
    def _():
        w_bf16[...] = w_ref[...].astype(BF16)

    h = _rms(x_ref[...], g_ref[...]).astype(BF16)
    c0 = 0
    for ref, (n, scale) in zip(out_refs, _IN_SEGMENTS):
        p = _dot(h, w_bf16[:, c0:c0 + n])
        if scale != 1.0:
            p = p * scale
        ref[...] = p.astype(BF16)
        c0 += n


def _in_proj(x2d, g, w):
    t, d = x2d.shape
    tm = min(TOK_TILE, t)
    return pl.pallas_call(
        _inproj_kernel,
        grid=(t // tm,),
        in_specs=[pl.BlockSpec((tm, d), lambda i: (i, 0)),
                  pl.BlockSpec((1, d), lambda i: (0, 0)),
                  pl.BlockSpec(w.shape, lambda i: (0, 0))],
        out_specs=[pl.BlockSpec((tm, n), lambda i: (i, 0)) for n, _ in _IN_SEGMENTS],
        out_shape=[jax.ShapeDtypeStruct((t, n), BF16) for n, _ in _IN_SEGMENTS],
        scratch_shapes=[pltpu.VMEM(w.shape, BF16)],
        compiler_params=_cparams("arbitrary"),
        name="in_proj",
    )(x2d, g, w)


def _diff_kernel(slopes_ref, lam_ref, g_ref, q_ref, k_ref, v_ref, kpos_ref, ones_ref, o_ref,
                 m_ref, a_ref, s_ref, *, lam_init, tile):
    h = pl.program_id(1)
    n_full = 2 * pl.program_id(2)
    slope = slopes_ref[h]

    lane = lax.broadcasted_iota(I32, (tile, DIFF_VDIM), 1)
    qpos = jnp.where(lane == 0, slope * float(POS_SPLIT), jnp.where(lane == 1, slope, 0.0)).astype(BF16)
    qs = []
    for sub in range(2):
        q = q_ref[0, sub * tile:(sub + 1) * tile, :]
        zero = jnp.zeros_like(q)
        qs.append(jnp.concatenate([jnp.where(lane < HEAD_DIM, q, zero), qpos], axis=1))
        qs.append(jnp.concatenate([jnp.where(lane >= HEAD_DIM, q, zero), qpos], axis=1))
    sub_a, sub_b, both = (0, 1), (2, 3), (0, 1, 2, 3)

    m_ref[...] = jnp.full(m_ref.shape, NEG, F32)
    a_ref[...] = jnp.zeros(a_ref.shape, F32)

    def scores(j, slot, chains):
        start = pl.multiple_of(j * tile, tile)
        kk = jnp.concatenate([k_ref[0, pl.ds(start, tile), :], kpos_ref[pl.ds(start, tile), :]], axis=1)
        for c in chains:
            s_ref[slot, c] = _dot_nt(qs[c], kk)

    def consume(j, slot, chains, diagonal):
        start = pl.multiple_of(j * tile, tile)
        vv = jnp.concatenate([v_ref[0, pl.ds(start, tile), :], ones_ref[...]], axis=1)
        for c in chains:
            s = s_ref[slot, c]
            if diagonal:
                row = lax.broadcasted_iota(I32, s.shape, 0)
                col = lax.broadcasted_iota(I32, s.shape, 1)
                s = jnp.where(col <= row, s, NEG)
            m_old = m_ref[c]
            m_new = jnp.maximum(m_old, jnp.max(s, axis=1, keepdims=True))
            p = jnp.exp(s - m_new)
            a_ref[c] = jnp.exp(m_old - m_new) * a_ref[c] + _dot(p.astype(BF16), vv)
            m_ref[c] = m_new

    scores(0, 0, both)

    def pair(p, carry):
        j = 2 * p
        scores(j + 1, 1, both)
        consume(j, 0, both, False)
        scores(j + 2, 0, both)
        consume(j + 1, 1, both, False)
        return carry

    lax.fori_loop(0, n_full // 2, pair, 0)
    scores(n_full + 1, 1, sub_b)
    consume(n_full, 0, sub_a, True)
    consume(n_full, 0, sub_b, False)
    consume(n_full + 1, 1, sub_b, True)

    lv = lam_ref[...]
    lam = (jnp.exp(jnp.sum(lv[0:1] * lv[1:2], axis=1, keepdims=True))
           - jnp.exp(jnp.sum(lv[2:3] * lv[3:4], axis=1, keepdims=True)) + lam_init)
    n = DIFF_VDIM
    for sub in range(2):
        a1, a2 = a_ref[2 * sub], a_ref[2 * sub + 1]
        o = (a1[:, :n] * (1.0 / a1[:, n:n + 1])) - lam * (a2[:, :n] * (1.0 / a2[:, n:n + 1]))
        o = _rms(o, g_ref[...]) * (1.0 - lam_init)
        o_ref[0, sub * tile:(sub + 1) * tile, :] = o.astype(o_ref.dtype)


def _diff_attention(qd, kd, vd, slopes, lam_qk, subln_g, lam_init):
    b, s, _ = qd.shape
    tile = ATT_TILE
    assert s % (2 * tile) == 0
    pos = jnp.arange(s, dtype=I32)[:, None]
    lane = jnp.arange(V7X_LANES, dtype=I32)[None, :]
    kpos = jnp.where(lane == 0, pos // POS_SPLIT, jnp.where(lane == 1, pos % POS_SPLIT, 0)).astype(BF16)
    ones = jnp.broadcast_to(lane == 0, (tile, V7X_LANES)).astype(BF16)
    vec = lambda shape: pltpu.VMEM(shape, F32)
    const = lambda a: pl.BlockSpec(a.shape, lambda bb, hh, ii: (0, 0))
    return pl.pallas_call(
        functools.partial(_diff_kernel, lam_init=lam_init, tile=tile),
        grid=(b, N_HEADS, s // (2 * tile)),
        in_specs=[pl.BlockSpec(memory_space=pltpu.SMEM), const(lam_qk), const(subln_g),
                  pl.BlockSpec((1, 2 * tile, DIFF_VDIM), lambda bb, hh, ii: (bb, ii, hh)),
                  pl.BlockSpec((1, s, DIFF_VDIM), lambda bb, hh, ii: (bb, 0, hh)),
                  pl.BlockSpec((1, s, DIFF_VDIM), lambda bb, hh, ii: (bb, 0, hh)),
                  const(kpos), const(ones)],
        out_specs=pl.BlockSpec((1, 2 * tile, DIFF_VDIM), lambda bb, hh, ii: (bb, ii, hh)),
        out_shape=jax.ShapeDtypeStruct((b, s, DIFF_WIDTH), BF16),
        scratch_shapes=[vec((4, tile, 1)), vec((4, tile, 2 * DIFF_VDIM)), vec((2, 4, tile, tile))],
        compiler_params=_cparams("arbitrary", "arbitrary", "arbitrary"),
        name="diff_attn",
    )(slopes, lam_qk, subln_g, qd, kd, vd, kpos, ones)


def _head_mask(lane, h):
    return (lane >= h * HEAD_DIM) & (lane < (h + 1) * HEAD_DIM)


def _split3(x):
    hi = x.astype(BF16)
    rem = x - hi.astype(F32)
    mid = rem.astype(BF16)
    return hi, mid, (rem - mid.astype(F32)).astype(BF16)


def _dil_kernel(q_ref, k_ref, v_ref, perm_ref, bias_ref, spread_ref, o_ref, qc, kc, vc, res_o, res_l, *, seq):
    tile, chunk = DIL_TILE, DIL_CHUNK
    n_blocks, n_chunks = seq // tile, seq // chunk
    lane = lax.broadcasted_iota(I32, (tile, GROUP_WIDTH), 1)
    tail_lane = lax.broadcasted_iota(I32, (tile, V7X_LANES), 1)

    def to_class_order(bi, r):
        n, sr = chunk // r, seq // r

        def one_chunk(t, carry):
            src0 = pl.multiple_of(t * chunk, chunk)
            for src, dst in ((q_ref, qc), (k_ref, kc), (v_ref, vc)):
                y = _dot(perm_ref[bi - 1, 0], src[0, pl.ds(src0, chunk), :]).astype(BF16)
                for rho in range(r):
                    dst[pl.ds(pl.multiple_of(rho * sr + t * n, n), n), :] = y[rho * n:(rho + 1) * n, :]
            return carry

        lax.fori_loop(0, n_chunks, one_chunk, 0)

    def attend(bi, r, load):
        blocks_per_class = (seq // r) // tile

        def one_block(g, carry):
            cur = pl.multiple_of(g * tile, tile)
            prev = pl.multiple_of(jnp.maximum(g - 1, 0) * tile, tile)
            has_prev = (g % blocks_per_class) != 0
            q = load(0, cur)
            qs = jnp.concatenate([jnp.where(_head_mask(lane, h), q, jnp.zeros_like(q)) for h in range(N_HEADS)], axis=0)
            kk = jnp.concatenate([load(1, prev), load(1, cur)], axis=0)
            vv = jnp.concatenate([load(2, prev), load(2, cur)], axis=0)
            bias = jnp.where(has_prev, bias_ref[bi, 1], bias_ref[bi, 0])
            s = _dot_nt(qs, kk) + bias
            m = jnp.max(s, axis=1, keepdims=True)
            p = jnp.exp(s - m)
            l = jnp.sum(p, axis=1, keepdims=True)
            pv = _dot(p.astype(BF16), vv) * (1.0 / l)
            lse = m + jnp.log(l)
            out = jnp.zeros((tile, GROUP_WIDTH), F32)
            tail = jnp.zeros((tile, V7X_LANES), F32)
            for h in range(N_HEADS):
                rows = slice(h * tile, (h + 1) * tile)
                out = jnp.where(_head_mask(lane, h), pv[rows], out)
                tail = jnp.where(tail_lane == h, lse[rows], tail)
            res_o[bi, pl.ds(cur, tile), :] = out.astype(BF16)
            res_l[bi, pl.ds(cur, tile), :] = tail
            return carry

        lax.fori_loop(0, n_blocks, one_block, 0, unroll=8)

    refs = (q_ref, k_ref, v_ref)
    scratch = (qc, kc, vc)
    attend(0, 1, lambda which, start: refs[which][0, pl.ds(start, tile), :])
    for bi, r in enumerate(DIL_DILATIONS[1:], start=1):
        to_class_order(bi, r)
        attend(bi, r, lambda which, start: scratch[which][pl.ds(start, tile), :])

    def merge_chunk(t, carry):
        base = pl.multiple_of(t * chunk, chunk)
        def pack3(lse):
            hi, mid, lo = _split3(lse)
            return (hi.astype(F32) + pltpu.roll(mid.astype(F32), N_HEADS, 1)
                    + pltpu.roll(lo.astype(F32), 2 * N_HEADS, 1)).astype(BF16)

        outs, lses = [], []
        for bi, r in enumerate(DIL_DILATIONS):
            if r == 1:
                out = res_o[0, pl.ds(base, chunk), :].astype(F32)
                packed = pack3(res_l[0, pl.ds(base, chunk), :])
            else:
                n, sr = chunk // r, seq // r
                rows = [pl.ds(pl.multiple_of(rho * sr + t * n, n), n) for rho in range(r)]
                back = perm_ref[bi - 1, 1]
                out = _dot(back, jnp.concatenate([res_o[bi, rw, :] for rw in rows], axis=0))
                packed = _dot(back, pack3(jnp.concatenate([res_l[bi, rw, :] for rw in rows], axis=0))).astype(BF16)
            outs.append(out)
            lses.append(_dot(packed, spread_ref[...]))
        big = jnp.maximum(jnp.maximum(lses[0], lses[1]), lses[2])
        wts = [jnp.exp(z - big) for z in lses]
        o = (wts[0] * outs[0] + wts[1] * outs[1] + wts[2] * outs[2]) * (1.0 / (wts[0] + wts[1] + wts[2]))
        o_ref[0, pl.ds(base, chunk), :] = o.astype(o_ref.dtype)
        return carry

    lax.fori_loop(0, n_chunks, merge_chunk, 0)


def _dil_constants(slopes, seq):
    chunk, tile = DIL_CHUNK, DIL_TILE
    perms = []
    for r in DIL_DILATIONS[1:]:
        n = chunk // r
        dst = jnp.arange(chunk, dtype=I32)
        src = r * (dst % n) + dst // n
        fwd = (src[:, None] == jnp.arange(chunk, dtype=I32)[None, :]).astype(BF16)
        perms.append(jnp.stack([fwd, fwd.T]))
    m = jnp.arange(tile, dtype=I32)[:, None]
    j = jnp.arange(2 * tile, dtype=I32)[None, :]
    dist = jnp.where(j < tile, m - j + tile, m - (j - tile))
    ok_cur = (j >= tile) & (dist >= 0)
    ok_prev = (j < tile) & (dist <= DIL_WINDOW_KEYS)
    biases = []
    for r in DIL_DILATIONS:
        per_head = [-(slopes[h] * r) * dist.astype(F32) for h in range(N_HEADS)]
        with_prev = jnp.concatenate([jnp.where(ok_cur | ok_prev, b, NEG) for b in per_head], axis=0)
        no_prev = jnp.concatenate([jnp.where(ok_cur, b, NEG) for b in per_head], axis=0)
        biases.append(jnp.stack([no_prev, with_prev]))
    src_lane = jnp.arange(V7X_LANES, dtype=I32)[:, None]
    spread = ((src_lane < 3 * N_HEADS)
              & (src_lane % N_HEADS == jnp.arange(GROUP_WIDTH, dtype=I32)[None, :] // HEAD_DIM)).astype(BF16)
    return jnp.stack(perms), jnp.stack(biases), spread


def _dil_attention(ql, kl, vl, slopes):
    b, s, w = ql.shape
    perms, biases, spread = _dil_constants(slopes, s)
    row = pl.BlockSpec((1, s, w), lambda i: (i, 0, 0))
    const = lambda a: pl.BlockSpec(a.shape, lambda i: (0,) * a.ndim)
    return pl.pallas_call(
        functools.partial(_dil_kernel, seq=s),
        grid=(b,),
        in_specs=[row, row, row, const(perms), const(biases), const(spread)],
        out_specs=row,
        out_shape=jax.ShapeDtypeStruct((b, s, w), BF16),
        scratch_shapes=[pltpu.VMEM((s, w), BF16)] * 3
                       + [pltpu.VMEM((len(DIL_DILATIONS), s, w), BF16),
                          pltpu.VMEM((len(DIL_DILATIONS), s, V7X_LANES), F32)],
        compiler_params=_cparams("arbitrary"),
        name="dil_attn",
    )(ql, kl, vl, perms, biases, spread)


def _memattn_kernel(q_ref, mk_ref, mv_ref, o_ref):
    q = q_ref[0]
    mk, mv = mk_ref[0], mv_ref[0]
    lane = lax.broadcasted_iota(I32, q.shape, 1)
    out = jnp.zeros(q.shape, F32)
    for h in range(N_HEADS):
        hm = _head_mask(lane, h)
        s = _dot_nt(jnp.where(hm, q, jnp.zeros_like(q)), mk)
        m = jnp.max(s, axis=1, keepdims=True)
        p = jnp.exp(s - m)
        l = jnp.sum(p, axis=1, keepdims=True)
        out = jnp.where(hm, _dot(p.astype(BF16), mv) * (1.0 / l), out)
    o_ref[0] = out.astype(o_ref.dtype)


def _mem_attention(mq, mk, mv):
    b, s, w = mq.shape
    m = mk.shape[1]
    tile = min(TOK_TILE, s)
    return pl.pallas_call(
        _memattn_kernel,
        grid=(b, s // tile),
        in_specs=[pl.BlockSpec((1, tile, w), lambda bb, ii: (bb, ii, 0)),
                  pl.BlockSpec((1, m, w), lambda bb, ii: (bb, 0, 0)),
                  pl.BlockSpec((1, m, w), lambda bb, ii: (bb, 0, 0))],
        out_specs=pl.BlockSpec((1, tile, w), lambda bb, ii: (bb, ii, 0)),
        out_shape=jax.ShapeDtypeStruct((b, s, w), BF16),
        compiler_params=_cparams("arbitrary", "arbitrary"),
        name="mem_attn",
    )(mq, mk, mv)


def _outrouter_kernel(x_ref, od_ref, ol_ref, om_ref, wo_ref, g_ref, wr_ref, br_ref,
                      x2_ref, h2_ref, eidx_ref, rank_ref, gate_ref, cnt_ref, run_ref, *, tm):
    step = pl.program_id(0)

    @pl.when(step == 0)
    def _():
        run_ref[...] = jnp.zeros(run_ref.shape, F32)

    c1, c2 = DIFF_WIDTH, DIFF_WIDTH + GROUP_WIDTH
    mix = (_dot(od_ref[...], wo_ref[0:c1, :]) + _dot(ol_ref[...], wo_ref[c1:c2, :])
           + _dot(om_ref[...], wo_ref[c2:, :]))
    x2 = x_ref[...] + mix
    x2_ref[...] = x2
    h2 = _rms(x2, g_ref[...])
    h2_ref[...] = h2

    h_hi = h2.astype(BF16)
    h_lo = (h2 - h_hi.astype(F32)).astype(BF16)
    logits = (_dot(h_hi, wr_ref[0]) + (_dot(h_hi, wr_ref[1]) + _dot(h_lo, wr_ref[0]))) + br_ref[...]
    lane = lax.broadcasted_iota(I32, logits.shape, 1)
    lane_rev = (V7X_LANES - 1 - lane).astype(F32)
    work = logits
    sel = jnp.zeros(logits.shape, F32)
    vals, idxs = [], []
    for _ in range(TOP_K):
        mx = jnp.max(work, axis=1, keepdims=True)
        key = jnp.max(jnp.where(work == mx, lane_rev, -1.0), axis=1, keepdims=True)
        hit = lane_rev == key
        sel = jnp.where(hit, 1.0, sel)
        work = jnp.where(hit, -jnp.inf, work)
        vals.append(mx)
        idxs.append(V7X_LANES - 1 - key.astype(I32))

    r_io = lax.broadcasted_iota(I32, (tm, tm), 0)
    c_io = lax.broadcasted_iota(I32, (tm, tm), 1)
    tri = jnp.where(c_io < r_io, 1.0, 0.0).astype(BF16)
    before = _dot(tri, sel.astype(BF16)) + run_ref[...]
    run_ref[...] = run_ref[...] + jnp.sum(sel, axis=0, keepdims=True)
    cnt_ref[...] = run_ref[...].astype(I32)

    exps = [jnp.exp(v - vals[0]) for v in vals]
    inv = 1.0 / (exps[0] + exps[1] + exps[2] + exps[3])
    eidx = jnp.zeros(logits.shape, I32)
    rank = jnp.zeros(logits.shape, I32)
    gate = jnp.zeros(logits.shape, F32)
    for k in range(TOP_K):
        rk = jnp.sum(jnp.where(lane == idxs[k], before, 0.0), axis=1, keepdims=True)
        eidx = jnp.where(lane == k, idxs[k], eidx)
        rank = jnp.where(lane == k, rk.astype(I32), rank)
        gate = jnp.where(lane == k, exps[k] * inv, gate)
    eidx_ref[...] = eidx
    rank_ref[...] = rank
    gate_ref[...] = gate


def _out_router(x2d, o_diff, o_dil, o_mem, w_out, g, w_router_p, b_router_p):
    t, d = x2d.shape
    tm = min(TOK_TILE, t)
    row = lambda n: pl.BlockSpec((tm, n), lambda i: (i, 0))
    full = lambda a: pl.BlockSpec(a.shape, lambda i: (0,) * a.ndim)
    return pl.pallas_call(
        functools.partial(_outrouter_kernel, tm=tm),
        grid=(t // tm,),
        in_specs=[row(d), row(DIFF_WIDTH), row(GROUP_WIDTH), row(GROUP_WIDTH),
                  full(w_out), full(g), full(w_router_p), full(b_router_p)],
        out_specs=[row(d), row(d), row(V7X_LANES), row(V7X_LANES), row(V7X_LANES),
                   pl.BlockSpec((1, V7X_LANES), lambda i: (0, 0))],
        out_shape=[jax.ShapeDtypeStruct((t, d), F32), jax.ShapeDtypeStruct((t, d), F32),
                   jax.ShapeDtypeStruct((t, V7X_LANES), I32), jax.ShapeDtypeStruct((t, V7X_LANES), I32),
                   jax.ShapeDtypeStruct((t, V7X_LANES), F32), jax.ShapeDtypeStruct((1, V7X_LANES), I32)],
        scratch_shapes=[pltpu.VMEM((1, V7X_LANES), F32)],
        compiler_params=_cparams("arbitrary"),
        name="out_router",
    )(x2d, o_diff, o_dil, o_mem, w_out, g, w_router_p, b_router_p)


def _row_copy(src, dst, sem):
    return pltpu.make_async_copy(src, dst, sem)


def _dispatch_kernel(dest_ref, h_ref, xs_hbm, sem, *, tm):
    def issue(t, carry):
        for k in range(TOP_K):
            d = dest_ref[0, 0, t * TOP_K + k]
            _row_copy(h_ref.at[pl.ds(t, 1), :], xs_hbm.at[pl.ds(d, 1), :], sem).start()
        return carry

    lax.fori_loop(0, tm, issue, 0, unroll=DMA_UNROLL)
    for _ in range(TOP_K):
        _row_copy(h_ref, xs_hbm.at[pl.ds(0, tm), :], sem).wait()


def _dispatch(h2, dest):
    t, d = h2.shape
    tm = min(DISPATCH_TILE, t)
    dest3 = dest.reshape(t // tm, 1, tm * TOP_K)
    return pl.pallas_call(
        functools.partial(_dispatch_kernel, tm=tm),
        grid=(t // tm,),
        in_specs=[pl.BlockSpec((1, 1, tm * TOP_K), lambda i: (i, 0, 0), memory_space=pltpu.SMEM),
                  pl.BlockSpec((tm, d), lambda i: (i, 0))],
        out_specs=pl.BlockSpec(memory_space=pl.ANY),
        out_shape=jax.ShapeDtypeStruct((t * TOP_K, d), h2.dtype),
        scratch_shapes=[pltpu.SemaphoreType.DMA],
        compiler_params=_cparams("arbitrary"),
        name="dispatch",
    )(dest3, h2)


def _split_gate_up_kernel(w_ref, sel_ref, wg_ref, wu_ref):
    x = w_ref[0].astype(BF16)
    k, n = sel_ref.shape[1], sel_ref.shape[2]
    for j in range(wg_ref.shape[2] // n):
        blk = x[:, j * k:(j + 1) * k]
        wg_ref[0, :, j * n:(j + 1) * n] = _dot(blk, sel_ref[0]).astype(BF16)
        wu_ref[0, :, j * n:(j + 1) * n] = _dot(blk, sel_ref[1]).astype(BF16)


def _split_gate_up(w):
    e, d, f2 = w.shape
    f = f2 // 2
    rows, n = min(TOK_TILE, d), V7X_MXU_DIM
    src = jnp.arange(2 * n, dtype=I32)[:, None]
    dst = jnp.arange(n, dtype=I32)[None, :]
    sel = jnp.stack([src == 2 * dst, src == 2 * dst + 1]).astype(BF16)
    out = jax.ShapeDtypeStruct((e, d, f), BF16)
    return pl.pallas_call(
        _split_gate_up_kernel,
        grid=(e, d // rows),
        in_specs=[pl.BlockSpec((1, rows, f2), lambda i, j: (i, j, 0)),
                  pl.BlockSpec(sel.shape, lambda i, j: (0, 0, 0))],
        out_specs=[pl.BlockSpec((1, rows, f), lambda i, j: (i, j, 0))] * 2,
        out_shape=[out, out],
        compiler_params=_cparams("arbitrary", "arbitrary"),
        name="split_gate_up",
    )(w, sel)


def _experts_kernel(tile_ref, exp_ref, lo_ref, hi_ref, first_ref, act_ref, fresh_ref,
                    xs_ref, wg_ref, wu_ref, bg_ref, bu_ref, wd_ref, bd_ref, y_ref, wd_bf16, *, tm):
    w = pl.program_id(0)

    @pl.when(fresh_ref[w] == 1)
    def _():
        wd_bf16[...] = wd_ref[0].astype(BF16)

    @pl.when(act_ref[w] == 1)
    def _():
        x = xs_ref[...].astype(BF16)
        gate = jnp.minimum(_dot(x, wg_ref[0]) + bg_ref[0], SWIGLU_LIMIT)
        up = jnp.clip(_dot(x, wu_ref[0]) + bu_ref[0], -SWIGLU_LIMIT, SWIGLU_LIMIT)
        act = (up + 1.0) * (gate * jax.nn.sigmoid(gate * SWIGLU_ALPHA))
        y = _dot(act.astype(BF16), wd_bf16[...]) + bd_ref[0]
        row = lax.broadcasted_iota(I32, y.shape, 0)
        mine = (row >= lo_ref[w]) & (row < hi_ref[w])

        @pl.when(first_ref[w] == 1)
        def _():
            y_ref[...] = jnp.where(mine, y, 0.0)

        @pl.when(first_ref[w] == 0)
        def _():
            y_ref[...] = jnp.where(mine, y, y_ref[...])


def _experts(xs, meta, wg, wu, bg, bu, wd, bd):
    n, d = xs.shape
    e, _, f = wg.shape
    tm = min(ROW_TILE, n)
    n_items = meta[0].shape[0]
    xmap = lambda w, tile, ex, *_: (tile[w], 0)
    wmap = lambda w, tile, ex, *_: (ex[w], 0, 0)
    grid_spec = pltpu.PrefetchScalarGridSpec(
        num_scalar_prefetch=len(meta),
        grid=(n_items,),
        in_specs=[pl.BlockSpec((tm, d), xmap),
                  pl.BlockSpec((1, d, f), wmap), pl.BlockSpec((1, d, f), wmap),
                  pl.BlockSpec((1, 1, f), wmap), pl.BlockSpec((1, 1, f), wmap),
                  pl.BlockSpec((1, f, d), wmap), pl.BlockSpec((1, 1, d), wmap)],
        out_specs=pl.BlockSpec((tm, d), xmap),
        scratch_shapes=[pltpu.VMEM((f, d), BF16)],
    )
    return pl.pallas_call(
        functools.partial(_experts_kernel, tm=tm),
        grid_spec=grid_spec,
        out_shape=jax.ShapeDtypeStruct((n, d), F32),
        compiler_params=_cparams("arbitrary"),
        name="experts",
    )(*meta, xs, wg, wu, bg, bu, wd, bd)


def _expert_work_items(counts, n_rows, tm):
    e = counts.shape[0]
    n_tiles = n_rows // tm
    n_items = n_tiles + e - 1
    ends = jnp.cumsum(counts)
    starts = ends - counts
    first_tile = starts // tm
    n_t = jnp.where(counts > 0, (ends - 1) // tm - first_tile + 1, 0)
    item_end = jnp.cumsum(n_t)
    item_start = item_end - n_t
    total = item_end[-1]
    w = jnp.arange(n_items, dtype=I32)
    active = w < total
    w_eff = jnp.minimum(w, total - 1)
    ex = jnp.minimum(jnp.sum((item_end[None, :] <= w_eff[:, None]).astype(I32), axis=1), e - 1)
    tile = (first_tile[ex] + (w_eff - item_start[ex])).astype(I32)
    lo = jnp.clip(starts[ex] - tile * tm, 0, tm).astype(I32)
    hi = jnp.clip(ends[ex] - tile * tm, 0, tm).astype(I32)
    one = jnp.ones((1,), I32)
    first = jnp.concatenate([one, (tile[1:] != tile[:-1]).astype(I32)])
    fresh = jnp.concatenate([one, (ex[1:] != ex[:-1]).astype(I32)])
    return tile, ex, lo, hi, first, active.astype(I32), fresh


def _combine_kernel(dest_ref, dest_next_ref, x2_ref, gate_ref, g_ref, y_hbm, o_ref, ybuf, sems, *, tm, n_steps):
    i = pl.program_id(0)
    slot = i % 2

    def gather(dref, into):
        def issue(t, carry):
            for k in range(TOP_K):
                d = dref[0, 0, t * TOP_K + k]
                _row_copy(y_hbm.at[pl.ds(d, 1), :], ybuf.at[into, k, pl.ds(t, 1), :], sems.at[into]).start()
            return carry
        lax.fori_loop(0, tm, issue, 0, unroll=DMA_UNROLL)

    @pl.when(i == 0)
    def _():
        gather(dest_ref, 0)

    @pl.when(i + 1 < n_steps)
    def _():
        gather(dest_next_ref, 1 - slot)

    for k in range(TOP_K):
        _row_copy(y_hbm.at[pl.ds(0, tm), :], ybuf.at[slot, k], sems.at[slot]).wait()

    gates = gate_ref[...]
    acc = x2_ref[...]
    for k in range(TOP_K):
        acc = acc + ybuf[slot, k] * gates[:, k:k + 1]
    o_ref[...] = _rms(acc, g_ref[...])


def _combine(y, dest, x2, gates, g_final):
    t, d = x2.shape
    tm = min(DMA_TILE, t)
    n_steps = t // tm
    dest3 = dest.reshape(n_steps, 1, tm * TOP_K)
    dest_spec = lambda f: pl.BlockSpec((1, 1, tm * TOP_K), f, memory_space=pltpu.SMEM)
    return pl.pallas_call(
        functools.partial(_combine_kernel, tm=tm, n_steps=n_steps),
        grid=(n_steps,),
        in_specs=[dest_spec(lambda i: (i, 0, 0)),
                  dest_spec(lambda i: (jnp.minimum(i + 1, n_steps - 1), 0, 0)),
                  pl.BlockSpec((tm, d), lambda i: (i, 0)),
                  pl.BlockSpec((tm, V7X_LANES), lambda i: (i, 0)),
                  pl.BlockSpec((1, d), lambda i: (0, 0)),
                  pl.BlockSpec(memory_space=pl.ANY)],
        out_specs=pl.BlockSpec((tm, d), lambda i: (i, 0)),
        out_shape=jax.ShapeDtypeStruct((t, d), F32),
        scratch_shapes=[pltpu.VMEM((2, TOP_K, tm, d), F32), pltpu.SemaphoreType.DMA((2,))],
        compiler_params=_cparams("arbitrary"),
        name="combine",
    )(dest3, dest3, x2, gates, g_final, y)


def _alibi_slopes():
    n = 2 * N_HEADS
    s = 2.0 ** (-8.0 * (jnp.arange(n, dtype=F32) + 1.0) / n)
    return s[0::2], s[1::2]


def kernel(x, mem, attn_norm_g, mem_norm_g, w_in, diff_lambda_qk, diff_subln_g, w_mem_kv, w_out,
           ffn_norm_g, w_router, b_router, w_gate_up, b_gate_up, w_down, b_down, final_norm_g):
    b, s, d = x.shape
    t = b * s
    assert w_in.shape[0] == 1, "the final norm is fused into the layer's combine step: one layer only"
    assert s % (DIL_DILATIONS[-1] * DIL_TILE) == 0 and t % TOK_TILE == 0
    n_exp = w_router.shape[-1]
    diff_slopes, dil_slopes = _alibi_slopes()
    x2d = x.reshape(t, d)

    for l in range(1):
        lam_init = 0.8 - 0.6 * math.exp(-0.3 * l)
        mk, mv = _mem_kv(mem, mem_norm_g[l][None], w_mem_kv[l].astype(BF16))
        qd, kd, vd, ql, kl, vl, mq = _in_proj(x2d, attn_norm_g[l][None], w_in[l])
        sh = lambda a: a.reshape(b, s, a.shape[-1])
        o_diff = _diff_attention(sh(qd), sh(kd), sh(vd), diff_slopes, diff_lambda_qk[l],
                                 diff_subln_g[l][None], lam_init)
        o_dil = _dil_attention(sh(ql), sh(kl), sh(vl), dil_slopes)
        o_mem = _mem_attention(sh(mq), mk, mv)

        w_router_f = jnp.zeros((d, V7X_LANES), F32).at[:, :n_exp].set(w_router[l])
        w_router_hi = w_router_f.astype(BF16)
        w_router_p = jnp.stack([w_router_hi, (w_router_f - w_router_hi.astype(F32)).astype(BF16)])
        b_router_p = jnp.full((1, V7X_LANES), NEG, F32).at[0, :n_exp].set(b_router[l])
        x2, h2, eidx, rank, gates, cnt = _out_router(
            x2d, o_diff.reshape(t, DIFF_WIDTH), o_dil.reshape(t, GROUP_WIDTH), o_mem.reshape(t, GROUP_WIDTH),
            w_out[l].astype(BF16), ffn_norm_g[l][None], w_router_p, b_router_p)

        counts = cnt[0, :n_exp]
        starts = jnp.cumsum(counts) - counts
        e_sel = eidx[:, :TOP_K]
        onehot = e_sel[:, :, None] == jnp.arange(n_exp, dtype=I32)[None, None, :]
        dest = (rank[:, :TOP_K] + jnp.sum(jnp.where(onehot, starts[None, None, :], 0), axis=-1)).astype(I32)

        xs = _dispatch(h2, dest)

        f = w_down.shape[2]
        w_gate, w_up = _split_gate_up(w_gate_up[l])
        bgu = b_gate_up[l].reshape(n_exp, 1, f, 2)
        meta = _expert_work_items(counts, t * TOP_K, min(ROW_TILE, t * TOP_K))
        y = _experts(xs, meta, w_gate, w_up, bgu[..., 0], bgu[..., 1], w_down[l], b_down[l][:, None, :])

        out = _combine(y, dest, x2, gates, final_norm_g[None])
    return out.reshape(b, s, d)
```

```python
import functools
import math

import jax
import jax.numpy as jnp
from jax import lax
from jax.experimental import pallas as pl
from jax.experimental.pallas import tpu as pltpu

F32 = jnp.float32
BF16 = jnp.bfloat16
I32 = jnp.int32

NORM_EPS = 1e-5
HEAD_DIM = 64
N_HEADS = 4
DIFF_VDIM = 2 * HEAD_DIM
DIFF_WIDTH = N_HEADS * DIFF_VDIM
GROUP_WIDTH = N_HEADS * HEAD_DIM
DIL_DILATIONS = (1, 4, 16)
DIL_WINDOW_KEYS = 128
TOP_K = 4
SWIGLU_ALPHA = 1.702
SWIGLU_LIMIT = 7.0
NEG = -1e30

V7X_LANES = 128
V7X_SUBLANES = 8
V7X_MXU_DIM = 256
V7X_VMEM_LIMIT = 56 * 1024 * 1024

TOK_TILE = 512
ATT_TILE = 512
POS_SPLIT = 64
DIL_TILE = DIL_WINDOW_KEYS
DIL_CHUNK = 256
ROW_TILE = 512
ROW_SUBLANES = V7X_SUBLANES
DMA_TILE = 256
DISPATCH_TILE = 1024
DMA_UNROLL = 4


def _cparams(*sem):
    return pltpu.CompilerParams(dimension_semantics=tuple(sem), vmem_limit_bytes=V7X_VMEM_LIMIT)


def _rms(x, g):
    return x * lax.rsqrt(jnp.mean(x * x, axis=-1, keepdims=True) + NORM_EPS) * g


def _dot_nt(a, b):
    return lax.dot_general(a, b, (((1,), (1,)), ((), ())), preferred_element_type=F32)


def _dot(a, b):
    return jnp.dot(a, b, preferred_element_type=F32)


def _memkv_kernel(mem_ref, g_ref, w_ref, mk_ref, mv_ref):
    h = _rms(mem_ref[0], g_ref[...]).astype(BF16)
    kv = _dot(h, w_ref[...])
    mk_ref[0] = kv[:, :GROUP_WIDTH].astype(BF16)
    mv_ref[0] = kv[:, GROUP_WIDTH:].astype(BF16)


def _mem_kv(mem, g, w):
    b, m, d = mem.shape
    out = jax.ShapeDtypeStruct((b, m, GROUP_WIDTH), BF16)
    return pl.pallas_call(
        _memkv_kernel,
        grid=(b,),
        in_specs=[pl.BlockSpec((1, m, d), lambda i: (i, 0, 0)),
                  pl.BlockSpec((1, d), lambda i: (0, 0)),
                  pl.BlockSpec((d, 2 * GROUP_WIDTH), lambda i: (0, 0))],
        out_specs=[pl.BlockSpec((1, m, GROUP_WIDTH), lambda i: (i, 0, 0))] * 2,
        out_shape=[out, out],
        compiler_params=_cparams("arbitrary"),
        name="mem_kv",
    )(mem, g, w)


_IN_SEGMENTS = ((DIFF_WIDTH, HEAD_DIM ** -0.5), (DIFF_WIDTH, 1.0), (DIFF_WIDTH, 1.0),
                (GROUP_WIDTH, HEAD_DIM ** -0.5), (GROUP_WIDTH, 1.0), (GROUP_WIDTH, 1.0),
                (GROUP_WIDTH, HEAD_DIM ** -0.5))


def _inproj_kernel(x_ref, g_ref, w_ref, *refs):
    out_refs, w_bf16 = refs[:-1], refs[-1]

    @pl.when(pl.program_id(0) == 0)
    def _():
        w_bf16[...] = w_ref[...].astype(BF16)

    h = _rms(x_ref[...], g_ref[...]).astype(BF16)
    c0 = 0
    for ref, (n, scale) in zip(out_refs, _IN_SEGMENTS):
        p = _dot(h, w_bf16[:, c0:c0 + n])
        if scale != 1.0:
            p = p * scale
        ref[...] = p.astype(BF16)
        c0 += n


def _in_proj(x2d, g, w):
    t, d = x2d.shape
    tm = min(TOK_TILE, t)
    return pl.pallas_call(
        _inproj_kernel,
        grid=(t // tm,),
        in_specs=[pl.BlockSpec((tm, d), lambda i: (i, 0)),
                  pl.BlockSpec((1, d), lambda i: (0, 0)),
                  pl.BlockSpec(w.shape, lambda i: (0, 0))],
        out_specs=[pl.BlockSpec((tm, n), lambda i: (i, 0)) for n, _ in _IN_SEGMENTS],
        out_shape=[jax.ShapeDtypeStruct((t, n), BF16) for n, _ in _IN_SEGMENTS],
        scratch_shapes=[pltpu.VMEM(w.shape, BF16)],
        compiler_params=_cparams("arbitrary"),
        name="in_proj",
    )(x2d, g, w)


def _diff_kernel(slopes_ref, lam_ref, g_ref, q_ref, k_ref, v_ref, kpos_ref, ones_ref, o_ref,
                 m_ref, a_ref, s_ref, *, lam_init, tile):
    h = pl.program_id(1)
    n_full = 2 * pl.program_id(2)
    slope = slopes_ref[h]

    lane = lax.broadcasted_iota(I32, (tile, DIFF_VDIM), 1)
    qpos = jnp.where(lane == 0, slope * float(POS_SPLIT), jnp.where(lane == 1, slope, 0.0)).astype(BF16)
    qs = []
    for sub in range(2):
        q = q_ref[0, sub * tile:(sub + 1) * tile, :]
        zero = jnp.zeros_like(q)
        qs.append(jnp.concatenate([jnp.where(lane < HEAD_DIM, q, zero), qpos], axis=1))
        qs.append(jnp.concatenate([jnp.where(lane >= HEAD_DIM, q, zero), qpos], axis=1))
    sub_a, sub_b, both = (0, 1), (2, 3), (0, 1, 2, 3)

    m_ref[...] = jnp.full(m_ref.shape, NEG, F32)
    a_ref[...] = jnp.zeros(a_ref.shape, F32)

    def scores(j, slot, chains):
        start = pl.multiple_of(j * tile, tile)
        kk = jnp.concatenate([k_ref[0, pl.ds(start, tile), :], kpos_ref[pl.ds(start, tile), :]], axis=1)
        for c in chains:
            s_ref[slot, c] = _dot_nt(qs[c], kk)

    def consume(j, slot, chains, diagonal):
        start = pl.multiple_of(j * tile, tile)
        vv = jnp.concatenate([v_ref[0, pl.ds(start, tile), :], ones_ref[...]], axis=1)
        for c in chains:
            s = s_ref[slot, c]
            if diagonal:
                row = lax.broadcasted_iota(I32, s.shape, 0)
                col = lax.broadcasted_iota(I32, s.shape, 1)
                s = jnp.where(col <= row, s, NEG)
            m_old = m_ref[c]
            m_new = jnp.maximum(m_old, jnp.max(s, axis=1, keepdims=True))
            p = jnp.exp(s - m_new)
            a_ref[c] = jnp.exp(m_old - m_new) * a_ref[c] + _dot(p.astype(BF16), vv)
            m_ref[c] = m_new

    scores(0, 0, both)

    def pair(p, carry):
        j = 2 * p
        scores(j + 1, 1, both)
        consume(j, 0, both, False)
        scores(j + 2, 0, both)
        consume(j + 1, 1, both, False)
        return carry

    lax.fori_loop(0, n_full // 2, pair, 0)
    scores(n_full + 1, 1, sub_b)
    consume(n_full, 0, sub_a, True)
    consume(n_full, 0, sub_b, False)
    consume(n_full + 1, 1, sub_b, True)

    lv = lam_ref[...]
    lam = (jnp.exp(jnp.sum(lv[0:1] * lv[1:2], axis=1, keepdims=True))
           - jnp.exp(jnp.sum(lv[2:3] * lv[3:4], axis=1, keepdims=True)) + lam_init)
    n = DIFF_VDIM
    for sub in range(2):
        a1, a2 = a_ref[2 * sub], a_ref[2 * sub + 1]
        o = (a1[:, :n] * (1.0 / a1[:, n:n + 1])) - lam * (a2[:, :n] * (1.0 / a2[:, n:n + 1]))
        o = _rms(o, g_ref[...]) * (1.0 - lam_init)
        o_ref[0, sub * tile:(sub + 1) * tile, :] = o.astype(o_ref.dtype)


def _diff_attention(qd, kd, vd, slopes, lam_qk, subln_g, lam_init):
    b, s, _ = qd.shape
    tile = ATT_TILE
    assert s % (2 * tile) == 0
    pos = jnp.arange(s, dtype=I32)[:, None]
    lane = jnp.arange(V7X_LANES, dtype=I32)[None, :]
    kpos = jnp.where(lane == 0, pos // POS_SPLIT, jnp.where(lane == 1, pos % POS_SPLIT, 0)).astype(BF16)
    ones = jnp.broadcast_to(lane == 0, (tile, V7X_LANES)).astype(BF16)
    vec = lambda shape: pltpu.VMEM(shape, F32)
    const = lambda a: pl.BlockSpec(a.shape, lambda bb, hh, ii: (0, 0))
    return pl.pallas_call(
        functools.partial(_diff_kernel, lam_init=lam_init, tile=tile),
        grid=(b, N_HEADS, s // (2 * tile)),
        in_specs=[pl.BlockSpec(memory_space=pltpu.SMEM), const(lam_qk), const(subln_g),
                  pl.BlockSpec((1, 2 * tile, DIFF_VDIM), lambda bb, hh, ii: (bb, ii, hh)),
                  pl.BlockSpec((1, s, DIFF_VDIM), lambda bb, hh, ii: (bb, 0, hh)),
                  pl.BlockSpec((1, s, DIFF_VDIM), lambda bb, hh, ii: (bb, 0, hh)),
                  const(kpos), const(ones)],
        out_specs=pl.BlockSpec((1, 2 * tile, DIFF_VDIM), lambda bb, hh, ii: (bb, ii, hh)),
        out_shape=jax.ShapeDtypeStruct((b, s, DIFF_WIDTH), BF16),
        scratch_shapes=[vec((4, tile, 1)), vec((4, tile, 2 * DIFF_VDIM)), vec((2, 4, tile, tile))],
        compiler_params=_cparams("arbitrary", "arbitrary", "arbitrary"),
        name="diff_attn",
    )(slopes, lam_qk, subln_g, qd, kd, vd, kpos, ones)


def _head_mask(lane, h):
    return (lane >= h * HEAD_DIM) & (lane < (h + 1) * HEAD_DIM)


def _split3(x):
    hi = x.astype(BF16)
    rem = x - hi.astype(F32)
    mid = rem.astype(BF16)
    return hi, mid, (rem - mid.astype(F32)).astype(BF16)


def _dil_kernel(q_ref, k_ref, v_ref, perm_ref, bias_ref, spread_ref, o_ref, qc, kc, vc, res_o, res_l, *, seq):
    tile, chunk = DIL_TILE, DIL_CHUNK
    n_blocks, n_chunks = seq // tile, seq // chunk
    lane = lax.broadcasted_iota(I32, (tile, GROUP_WIDTH), 1)
    tail_lane = lax.broadcasted_iota(I32, (tile, V7X_LANES), 1)

    def to_class_order(bi, r):
        n, sr = chunk // r, seq // r

        def one_chunk(t, carry):
            src0 = pl.multiple_of(t * chunk, chunk)
            for src, dst in ((q_ref, qc), (k_ref, kc), (v_ref, vc)):
                y = _dot(perm_ref[bi - 1, 0], src[0, pl.ds(src0, chunk), :]).astype(BF16)
                for rho in range(r):
                    dst[pl.ds(pl.multiple_of(rho * sr + t * n, n), n), :] = y[rho * n:(rho + 1) * n, :]
            return carry

        lax.fori_loop(0, n_chunks, one_chunk, 0)

    def attend(bi, r, load):
        blocks_per_class = (seq // r) // tile

        def one_block(g, carry):
            cur = pl.multiple_of(g * tile, tile)
            prev = pl.multiple_of(jnp.maximum(g - 1, 0) * tile, tile)
            has_prev = (g % blocks_per_class) != 0
            q = load(0, cur)
            qs = jnp.concatenate([jnp.where(_head_mask(lane, h), q, jnp.zeros_like(q)) for h in range(N_HEADS)], axis=0)
            kk = jnp.concatenate([load(1, prev), load(1, cur)], axis=0)
            vv = jnp.concatenate([load(2, prev), load(2, cur)], axis=0)
            bias = jnp.where(has_prev, bias_ref[bi, 1], bias_ref[bi, 0])
            s = _dot_nt(qs, kk) + bias
            m = jnp.max(s, axis=1, keepdims=True)
            p = jnp.exp(s - m)
            l = jnp.sum(p, axis=1, keepdims=True)
            pv = _dot(p.astype(BF16), vv) * (1.0 / l)
            lse = m + jnp.log(l)
            out = jnp.zeros((tile, GROUP_WIDTH), F32)
            tail = jnp.zeros((tile, V7X_LANES), F32)
            for h in range(N_HEADS):
                rows = slice(h * tile, (h + 1) * tile)
                out = jnp.where(_head_mask(lane, h), pv[rows], out)
                tail = jnp.where(tail_lane == h, lse[rows], tail)
            res_o[bi, pl.ds(cur, tile), :] = out.astype(BF16)
            res_l[bi, pl.ds(cur, tile), :] = tail
            return carry

        lax.fori_loop(0, n_blocks, one_block, 0, unroll=8)

    refs = (q_ref, k_ref, v_ref)
    scratch = (qc, kc, vc)
    attend(0, 1, lambda which, start: refs[which][0, pl.ds(start, tile), :])
    for bi, r in enumerate(DIL_DILATIONS[1:], start=1):
        to_class_order(bi, r)
        attend(bi, r, lambda which, start: scratch[which][pl.ds(start, tile), :])

    def merge_chunk(t, carry):
        base = pl.multiple_of(t * chunk, chunk)
        def pack3(lse):
            hi, mid, lo = _split3(lse)
            return (hi.astype(F32) + pltpu.roll(mid.astype(F32), N_HEADS, 1)
                    + pltpu.roll(lo.astype(F32), 2 * N_HEADS, 1)).astype(BF16)

        outs, lses = [], []
        for bi, r in enumerate(DIL_DILATIONS):
            if r == 1:
                out = res_o[0, pl.ds(base, chunk), :].astype(F32)
                packed = pack3(res_l[0, pl.ds(base, chunk), :])
            else:
                n, sr = chunk // r, seq // r
                rows = [pl.ds(pl.multiple_of(rho * sr + t * n, n), n) for rho in range(r)]
                back = perm_ref[bi - 1, 1]
                out = _dot(back, jnp.concatenate([res_o[bi, rw, :] for rw in rows], axis=0))
                packed = _dot(back, pack3(jnp.concatenate([res_l[bi, rw, :] for rw in rows], axis=0))).astype(BF16)
            outs.append(out)
            lses.append(_dot(packed, spread_ref[...]))
        big = jnp.maximum(jnp.maximum(lses[0], lses[1]), lses[2])
        wts = [jnp.exp(z - big) for z in lses]
        o = (wts[0] * outs[0] + wts[1] * outs[1] + wts[2] * outs[2]) * (1.0 / (wts[0] + wts[1] + wts[2]))
        o_ref[0, pl.ds(base, chunk), :] = o.astype(o_ref.dtype)
        return carry

    lax.fori_loop(0, n_chunks, merge_chunk, 0)


def _dil_constants(slopes, seq):
    chunk, tile = DIL_CHUNK, DIL_TILE
    perms = []
    for r in DIL_DILATIONS[1:]:
        n = chunk // r
        dst = jnp.arange(chunk, dtype=I32)
        src = r * (dst % n) + dst // n
        fwd = (src[:, None] == jnp.arange(chunk, dtype=I32)[None, :]).astype(BF16)
        perms.append(jnp.stack([fwd, fwd.T]))
    m = jnp.arange(tile, dtype=I32)[:, None]
    j = jnp.arange(2 * tile, dtype=I32)[None, :]
    dist = jnp.where(j < tile, m - j + tile, m - (j - tile))
    ok_cur = (j >= tile) & (dist >= 0)
    ok_prev = (j < tile) & (dist <= DIL_WINDOW_KEYS)
    biases = []
    for r in DIL_DILATIONS:
        per_head = [-(slopes[h] * r) * dist.astype(F32) for h in range(N_HEADS)]
        with_prev = jnp.concatenate([jnp.where(ok_cur | ok_prev, b, NEG) for b in per_head], axis=0)
        no_prev = jnp.concatenate([jnp.where(ok_cur, b, NEG) for b in per_head], axis=0)
        biases.append(jnp.stack([no_prev, with_prev]))
    src_lane = jnp.arange(V7X_LANES, dtype=I32)[:, None]
    spread = ((src_lane < 3 * N_HEADS)
              & (src_lane % N_HEADS == jnp.arange(GROUP_WIDTH, dtype=I32)[None, :] // HEAD_DIM)).astype(BF16)
    return jnp.stack(perms), jnp.stack(biases), spread


def _dil_attention(ql, kl, vl, slopes):
    b, s, w = ql.shape
    perms, biases, spread = _dil_constants(slopes, s)
    row = pl.BlockSpec((1, s, w), lambda i: (i, 0, 0))
    const = lambda a: pl.BlockSpec(a.shape, lambda i: (0,) * a.ndim)
    return pl.pallas_call(
        functools.partial(_dil_kernel, seq=s),
        grid=(b,),
        in_specs=[row, row, row, const(perms), const(biases), const(spread)],
        out_specs=row,
        out_shape=jax.ShapeDtypeStruct((b, s, w), BF16),
        scratch_shapes=[pltpu.VMEM((s, w), BF16)] * 3
                       + [pltpu.VMEM((len(DIL_DILATIONS), s, w), BF16),
                          pltpu.VMEM((len(DIL_DILATIONS), s, V7X_LANES), F32)],
        compiler_params=_cparams("arbitrary"),
        name="dil_attn",
    )(ql, kl, vl, perms, biases, spread)


def _memattn_kernel(q_ref, mk_ref, mv_ref, o_ref):
    q = q_ref[0]
    mk, mv = mk_ref[0], mv_ref[0]
    lane = lax.broadcasted_iota(I32, q.shape, 1)
    out = jnp.zeros(q.shape, F32)
    for h in range(N_HEADS):
        hm = _head_mask(lane, h)
        s = _dot_nt(jnp.where(hm, q, jnp.zeros_like(q)), mk)
        m = jnp.max(s, axis=1, keepdims=True)
        p = jnp.exp(s - m)
        l = jnp.sum(p, axis=1, keepdims=True)
        out = jnp.where(hm, _dot(p.astype(BF16), mv) * (1.0 / l), out)
    o_ref[0] = out.astype(o_ref.dtype)


def _mem_attention(mq, mk, mv):
    b, s, w = mq.shape
    m = mk.shape[1]
    tile = min(TOK_TILE, s)
    return pl.pallas_call(
        _memattn_kernel,
        grid=(b, s // tile),
        in_specs=[pl.BlockSpec((1, tile, w), lambda bb, ii: (bb, ii, 0)),
                  pl.BlockSpec((1, m, w), lambda bb, ii: (bb, 0, 0)),
                  pl.BlockSpec((1, m, w), lambda bb, ii: (bb, 0, 0))],
        out_specs=pl.BlockSpec((1, tile, w), lambda bb, ii: (bb, ii, 0)),
        out_shape=jax.ShapeDtypeStruct((b, s, w), BF16),
        compiler_params=_cparams("arbitrary", "arbitrary"),
        name="mem_attn",
    )(mq, mk, mv)


def _outrouter_kernel(x_ref, od_ref, ol_ref, om_ref, wo_ref, g_ref, wr_ref, br_ref,
                      x2_ref, h2_ref, eidx_ref, rank_ref, gate_ref, cnt_ref, run_ref, *, tm):
    step = pl.program_id(0)

    @pl.when(step == 0)
    def _():
        run_ref[...] = jnp.zeros(run_ref.shape, F32)

    c1, c2 = DIFF_WIDTH, DIFF_WIDTH + GROUP_WIDTH
    mix = (_dot(od_ref[...], wo_ref[0:c1, :]) + _dot(ol_ref[...], wo_ref[c1:c2, :])
           + _dot(om_ref[...], wo_ref[c2:, :]))
    x2 = x_ref[...] + mix
    x2_ref[...] = x2
    h2 = _rms(x2, g_ref[...])
    h2_ref[...] = h2

    h_hi = h2.astype(BF16)
    h_lo = (h2 - h_hi.astype(F32)).astype(BF16)
    logits = (_dot(h_hi, wr_ref[0]) + (_dot(h_hi, wr_ref[1]) + _dot(h_lo, wr_ref[0]))) + br_ref[...]
    lane = lax.broadcasted_iota(I32, logits.shape, 1)
    lane_rev = (V7X_LANES - 1 - lane).astype(F32)
    work = logits
    sel = jnp.zeros(logits.shape, F32)
    vals, idxs = [], []
    for _ in range(TOP_K):
        mx = jnp.max(work, axis=1, keepdims=True)
        key = jnp.max(jnp.where(work == mx, lane_rev, -1.0), axis=1, keepdims=True)
        hit = lane_rev == key
        sel = jnp.where(hit, 1.0, sel)
        work = jnp.where(hit, -jnp.inf, work)
        vals.append(mx)
        idxs.append(V7X_LANES - 1 - key.astype(I32))

    r_io = lax.broadcasted_iota(I32, (tm, tm), 0)
    c_io = lax.broadcasted_iota(I32, (tm, tm), 1)
    tri = jnp.where(c_io < r_io, 1.0, 0.0).astype(BF16)
    before = _dot(tri, sel.astype(BF16)) + run_ref[...]
    run_ref[...] = run_ref[...] + jnp.sum(sel, axis=0, keepdims=True)
    cnt_ref[...] = run_ref[...].astype(I32)

    exps = [jnp.exp(v - vals[0]) for v in vals]
    inv = 1.0 / (exps[0] + exps[1] + exps[2] + exps[3])
    eidx = jnp.zeros(logits.shape, I32)
    rank = jnp.zeros(logits.shape, I32)
    gate = jnp.zeros(logits.shape, F32)
    for k in range(TOP_K):
        rk = jnp.sum(jnp.where(lane == idxs[k], before, 0.0), axis=1, keepdims=True)
        eidx = jnp.where(lane == k, idxs[k], eidx)
        rank = jnp.where(lane == k, rk.astype(I32), rank)
        gate = jnp.where(lane == k, exps[k] * inv, gate)
    eidx_ref[...] = eidx
    rank_ref[...] = rank
    gate_ref[...] = gate


def _out_router(x2d, o_diff, o_dil, o_mem, w_out, g, w_router_p, b_router_p):
    t, d = x2d.shape
    tm = min(TOK_TILE, t)
    row = lambda n: pl.BlockSpec((tm, n), lambda i: (i, 0))
    full = lambda a: pl.BlockSpec(a.shape, lambda i: (0,) * a.ndim)
    return pl.pallas_call(
        functools.partial(_outrouter_kernel, tm=tm),
        grid=(t // tm,),
        in_specs=[row(d), row(DIFF_WIDTH), row(GROUP_WIDTH), row(GROUP_WIDTH),
                  full(w_out), full(g), full(w_router_p), full(b_router_p)],
        out_specs=[row(d), row(d), row(V7X_LANES), row(V7X_LANES), row(V7X_LANES),
                   pl.BlockSpec((1, V7X_LANES), lambda i: (0, 0))],
        out_shape=[jax.ShapeDtypeStruct((t, d), F32), jax.ShapeDtypeStruct((t, d), F32),
                   jax.ShapeDtypeStruct((t, V7X_LANES), I32), jax.ShapeDtypeStruct((t, V7X_LANES), I32),
                   jax.ShapeDtypeStruct((t, V7X_LANES), F32), jax.ShapeDtypeStruct((1, V7X_LANES), I32)],
        scratch_shapes=[pltpu.VMEM((1, V7X_LANES), F32)],
        compiler_params=_cparams("arbitrary"),
        name="out_router",
    )(x2d, o_diff, o_dil, o_mem, w_out, g, w_router_p, b_router_p)


def _row_copy(src, dst, sem):
    return pltpu.make_async_copy(src, dst, sem)


def _dispatch_kernel(dest_ref, h_ref, xs_hbm, sem, *, tm):
    def issue(t, carry):
        for k in range(TOP_K):
            d = dest_ref[0, 0, t * TOP_K + k]
            _row_copy(h_ref.at[pl.ds(t, 1), :], xs_hbm.at[pl.ds(d, 1), :], sem).start()
        return carry

    lax.fori_loop(0, tm, issue, 0, unroll=DMA_UNROLL)
    for _ in range(TOP_K):
        _row_copy(h_ref, xs_hbm.at[pl.ds(0, tm), :], sem).wait()


def _dispatch(h2, dest):
    t, d = h2.shape
    tm = min(DISPATCH_TILE, t)
    dest3 = dest.reshape(t // tm, 1, tm * TOP_K)
    return pl.pallas_call(
        functools.partial(_dispatch_kernel, tm=tm),
        grid=(t // tm,),
        in_specs=[pl.BlockSpec((1, 1, tm * TOP_K), lambda i: (i, 0, 0), memory_space=pltpu.SMEM),
                  pl.BlockSpec((tm, d), lambda i: (i, 0))],
        out_specs=pl.BlockSpec(memory_space=pl.ANY),
        out_shape=jax.ShapeDtypeStruct((t * TOP_K, d), h2.dtype),
        scratch_shapes=[pltpu.SemaphoreType.DMA],
        compiler_params=_cparams("arbitrary"),
        name="dispatch",
    )(dest3, h2)


def _split_gate_up_kernel(w_ref, sel_ref, wg_ref, wu_ref):
    x = w_ref[0].astype(BF16)
    k, n = sel_ref.shape[1], sel_ref.shape[2]
    for j in range(wg_ref.shape[2] // n):
        blk = x[:, j * k:(j + 1) * k]
        wg_ref[0, :, j * n:(j + 1) * n] = _dot(blk, sel_ref[0]).astype(BF16)
        wu_ref[0, :, j * n:(j + 1) * n] = _dot(blk, sel_ref[1]).astype(BF16)


def _split_gate_up(w):
    e, d, f2 = w.shape
    f = f2 // 2
    rows, n = min(TOK_TILE, d), V7X_MXU_DIM
    src = jnp.arange(2 * n, dtype=I32)[:, None]
    dst = jnp.arange(n, dtype=I32)[None, :]
    sel = jnp.stack([src == 2 * dst, src == 2 * dst + 1]).astype(BF16)
    out = jax.ShapeDtypeStruct((e, d, f), BF16)
    return pl.pallas_call(
        _split_gate_up_kernel,
        grid=(e, d // rows),
        in_specs=[pl.BlockSpec((1, rows, f2), lambda i, j: (i, j, 0)),
                  pl.BlockSpec(sel.shape, lambda i, j: (0, 0, 0))],
        out_specs=[pl.BlockSpec((1, rows, f), lambda i, j: (i, j, 0))] * 2,
        out_shape=[out, out],
        compiler_params=_cparams("arbitrary", "arbitrary"),
        name="split_gate_up",
    )(w, sel)


def _experts_kernel(tile_ref, exp_ref, lo_ref, hi_ref, first_ref, act_ref, fresh_ref,
                    xs_ref, wg_ref, wu_ref, bg_ref, bu_ref, wd_ref, bd_ref, y_ref, wd_bf16, *, tm):
    w = pl.program_id(0)

    @pl.when(fresh_ref[w] == 1)
    def _():
        wd_bf16[...] = wd_ref[0].astype(BF16)

    @pl.when(act_ref[w] == 1)
    def _():
        x = xs_ref[...].astype(BF16)
        gate = jnp.minimum(_dot(x, wg_ref[0]) + bg_ref[0], SWIGLU_LIMIT)
        up = jnp.clip(_dot(x, wu_ref[0]) + bu_ref[0], -SWIGLU_LIMIT, SWIGLU_LIMIT)
        act = (up + 1.0) * (gate * jax.nn.sigmoid(gate * SWIGLU_ALPHA))
        y = _dot(act.astype(BF16), wd_bf16[...]) + bd_ref[0]
        row = lax.broadcasted_iota(I32, (tm, V7X_LANES), 0)
        mine = (row >= lo_ref[w]) & (row < hi_ref[w])
        blocks = [(pl.ds(s, tm, stride=ROW_SUBLANES), y[:, s * V7X_LANES:(s + 1) * V7X_LANES])
                  for s in range(ROW_SUBLANES)]

        @pl.when(first_ref[w] == 1)
        def _():
            for rows, ys in blocks:
                y_ref[rows, :] = jnp.where(mine, ys, 0.0)

        @pl.when(first_ref[w] == 0)
        def _():
            for rows, ys in blocks:
                y_ref[rows, :] = jnp.where(mine, ys, y_ref[rows, :])


def _experts(xs, meta, wg, wu, bg, bu, wd, bd):
    n, d = xs.shape
    e, _, f = wg.shape
    tm = min(ROW_TILE, n)
    n_items = meta[0].shape[0]
    xmap = lambda w, tile, ex, *_: (tile[w], 0)
    wmap = lambda w, tile, ex, *_: (ex[w], 0, 0)
    grid_spec = pltpu.PrefetchScalarGridSpec(
        num_scalar_prefetch=len(meta),
        grid=(n_items,),
        in_specs=[pl.BlockSpec((tm, d), xmap),
                  pl.BlockSpec((1, d, f), wmap), pl.BlockSpec((1, d, f), wmap),
                  pl.BlockSpec((1, 1, f), wmap), pl.BlockSpec((1, 1, f), wmap),
                  pl.BlockSpec((1, f, d), wmap), pl.BlockSpec((1, 1, d), wmap)],
        out_specs=pl.BlockSpec((tm * ROW_SUBLANES, V7X_LANES), xmap),
        scratch_shapes=[pltpu.VMEM((f, d), BF16)],
    )
    assert d == ROW_SUBLANES * V7X_LANES
    return pl.pallas_call(
        functools.partial(_experts_kernel, tm=tm),
        grid_spec=grid_spec,
        out_shape=jax.ShapeDtypeStruct((n * ROW_SUBLANES, V7X_LANES), F32),
        compiler_params=_cparams("arbitrary"),
        name="experts",
    )(*meta, xs, wg, wu, bg, bu, wd, bd)


def _expert_work_items(counts, n_rows, tm):
    e = counts.shape[0]
    n_tiles = n_rows // tm
    n_items = n_tiles + e - 1
    ends = jnp.cumsum(counts)
    starts = ends - counts
    first_tile = starts // tm
    n_t = jnp.where(counts > 0, (ends - 1) // tm - first_tile + 1, 0)
    item_end = jnp.cumsum(n_t)
    item_start = item_end - n_t
    total = item_end[-1]
    w = jnp.arange(n_items, dtype=I32)
    active = w < total
    w_eff = jnp.minimum(w, total - 1)
    ex = jnp.minimum(jnp.sum((item_end[None, :] <= w_eff[:, None]).astype(I32), axis=1), e - 1)
    mine = ex[:, None] == jnp.arange(e, dtype=I32)[None, :]
    pick = lambda v: jnp.sum(jnp.where(mine, v[None, :], 0), axis=1)
    tile = (pick(first_tile) + (w_eff - pick(item_start))).astype(I32)
    lo = jnp.clip(pick(starts) - tile * tm, 0, tm).astype(I32)
    hi = jnp.clip(pick(ends) - tile * tm, 0, tm).astype(I32)
    one = jnp.ones((1,), I32)
    first = jnp.concatenate([one, (tile[1:] != tile[:-1]).astype(I32)])
    fresh = jnp.concatenate([one, (ex[1:] != ex[:-1]).astype(I32)])
    return tile, ex, lo, hi, first, active.astype(I32), fresh


def _combine_kernel(dest_ref, dest_next_ref, x2_ref, gate_ref, g_ref, y_hbm, o_ref, ybuf, sems, *, tm, n_steps):
    i = pl.program_id(0)
    slot = i % 2

    def gather(dref, into):
        def issue(t, carry):
            for k in range(TOP_K):
                src = pl.multiple_of(dref[0, 0, t * TOP_K + k] * ROW_SUBLANES, ROW_SUBLANES)
                dst = pl.multiple_of(t * ROW_SUBLANES, ROW_SUBLANES)
                _row_copy(y_hbm.at[pl.ds(src, ROW_SUBLANES), :], ybuf.at[into, k, pl.ds(dst, ROW_SUBLANES), :],
                          sems.at[into]).start()
            return carry
        lax.fori_loop(0, tm, issue, 0, unroll=DMA_UNROLL)

    @pl.when(i == 0)
    def _():
        gather(dest_ref, 0)

    @pl.when(i + 1 < n_steps)
    def _():
        gather(dest_next_ref, 1 - slot)

    for k in range(TOP_K):
        _row_copy(y_hbm.at[pl.ds(0, tm * ROW_SUBLANES), :], ybuf.at[slot, k], sems.at[slot]).wait()

    gates = gate_ref[...]
    cols = []
    for s in range(ROW_SUBLANES):
        acc = x2_ref[:, s * V7X_LANES:(s + 1) * V7X_LANES]
        for k in range(TOP_K):
            acc = acc + ybuf[slot, k, pl.ds(s, tm, stride=ROW_SUBLANES), :] * gates[:, k:k + 1]
        cols.append(acc)
    o_ref[...] = _rms(jnp.concatenate(cols, axis=1), g_ref[...])


def _combine(y, dest, x2, gates, g_final):
    t, d = x2.shape
    tm = min(DMA_TILE, t)
    n_steps = t // tm
    dest3 = dest.reshape(n_steps, 1, tm * TOP_K)
    dest_spec = lambda f: pl.BlockSpec((1, 1, tm * TOP_K), f, memory_space=pltpu.SMEM)
    return pl.pallas_call(
        functools.partial(_combine_kernel, tm=tm, n_steps=n_steps),
        grid=(n_steps,),
        in_specs=[dest_spec(lambda i: (i, 0, 0)),
                  dest_spec(lambda i: (jnp.minimum(i + 1, n_steps - 1), 0, 0)),
                  pl.BlockSpec((tm, d), lambda i: (i, 0)),
                  pl.BlockSpec((tm, V7X_LANES), lambda i: (i, 0)),
                  pl.BlockSpec((1, d), lambda i: (0, 0)),
                  pl.BlockSpec(memory_space=pl.ANY)],
        out_specs=pl.BlockSpec((tm, d), lambda i: (i, 0)),
        out_shape=jax.ShapeDtypeStruct((t, d), F32),
        scratch_shapes=[pltpu.VMEM((2, TOP_K, tm * ROW_SUBLANES, V7X_LANES), F32), pltpu.SemaphoreType.DMA((2,))],
        compiler_params=_cparams("arbitrary"),
        name="combine",
    )(dest3, dest3, x2, gates, g_final, y)


def _alibi_slopes():
    n = 2 * N_HEADS
    s = 2.0 ** (-8.0 * (jnp.arange(n, dtype=F32) + 1.0) / n)
    return s[0::2], s[1::2]


def kernel(x, mem, attn_norm_g, mem_norm_g, w_in, diff_lambda_qk, diff_subln_g, w_mem_kv, w_out,
           ffn_norm_g, w_router, b_router, w_gate_up, b_gate_up, w_down, b_down, final_norm_g):
    b, s, d = x.shape
    t = b * s
    assert w_in.shape[0] == 1, "the final norm is fused into the layer's combine step: one layer only"
    assert s % (DIL_DILATIONS[-1] * DIL_TILE) == 0 and t % TOK_TILE == 0
    n_exp = w_router.shape[-1]
    diff_slopes, dil_slopes = _alibi_slopes()
    x2d = x.reshape(t, d)

    for l in range(1):
        lam_init = 0.8 - 0.6 * math.exp(-0.3 * l)
        mk, mv = _mem_kv(mem, mem_norm_g[l][None], w_mem_kv[l].astype(BF16))
        qd, kd, vd, ql, kl, vl, mq = _in_proj(x2d, attn_norm_g[l][None], w_in[l])
        sh = lambda a: a.reshape(b, s, a.shape[-1])
        o_diff = _diff_attention(sh(qd), sh(kd), sh(vd), diff_slopes, diff_lambda_qk[l],
                                 diff_subln_g[l][None], lam_init)
        o_dil = _dil_attention(sh(ql), sh(kl), sh(vl), dil_slopes)
        o_mem = _mem_attention(sh(mq), mk, mv)

        w_router_f = jnp.zeros((d, V7X_LANES), F32).at[:, :n_exp].set(w_router[l])
        w_router_hi = w_router_f.astype(BF16)
        w_router_p = jnp.stack([w_router_hi, (w_router_f - w_router_hi.astype(F32)).astype(BF16)])
        b_router_p = jnp.full((1, V7X_LANES), NEG, F32).at[0, :n_exp].set(b_router[l])
        x2, h2, eidx, rank, gates, cnt = _out_router(
            x2d, o_diff.reshape(t, DIFF_WIDTH), o_dil.reshape(t, GROUP_WIDTH), o_mem.reshape(t, GROUP_WIDTH),
            w_out[l].astype(BF16), ffn_norm_g[l][None], w_router_p, b_router_p)

        counts = cnt[0, :n_exp]
        starts = jnp.cumsum(counts) - counts
        e_sel = eidx[:, :TOP_K]
        onehot = e_sel[:, :, None] == jnp.arange(n_exp, dtype=I32)[None, None, :]
        dest = (rank[:, :TOP_K] + jnp.sum(jnp.where(onehot, starts[None, None, :], 0), axis=-1)).astype(I32)

        xs = _dispatch(h2, dest)

        f = w_down.shape[2]
        w_gate, w_up = _split_gate_up(w_gate_up[l])
        bgu = b_gate_up[l].reshape(n_exp, 1, f, 2)
        meta = _expert_work_items(counts, t * TOP_K, min(ROW_TILE, t * TOP_K))
        y = _experts(xs, meta, w_gate, w_up, bgu[..., 0], bgu[..., 1], w_down[l], b_down[l][:, None, :])

        out = _combine(y, dest, x2, gates, final_norm_g[None])
    return out.reshape(b, s, d)
```

```python
import functools
import math

import jax
import jax.numpy as jnp
from jax import lax
from jax.experimental import pallas as pl
from jax.experimental.pallas import tpu as pltpu

F32 = jnp.float32
BF16 = jnp.bfloat16
I32 = jnp.int32

NORM_EPS = 1e-5
HEAD_DIM = 64
N_HEADS = 4
DIFF_VDIM = 2 * HEAD_DIM
DIFF_WIDTH = N_HEADS * DIFF_VDIM
GROUP_WIDTH = N_HEADS * HEAD_DIM
DIL_DILATIONS = (1, 4, 16)
DIL_WINDOW_KEYS = 128
TOP_K = 4
SWIGLU_ALPHA = 1.702
SWIGLU_LIMIT = 7.0
NEG = -1e30

V7X_LANES = 128
V7X_SUBLANES = 8
V7X_MXU_DIM = 256
V7X_VMEM_LIMIT = 56 * 1024 * 1024

TOK_TILE = 512
ATT_TILE = 512
POS_SPLIT = 64
DIL_TILE = DIL_WINDOW_KEYS
DIL_CHUNK = 256
ROW_TILE = 512
ROW_SUBLANES = V7X_SUBLANES
DMA_TILE = 256
DISPATCH_TILE = 1024
DMA_UNROLL = 4


def _cparams(*sem):
    return pltpu.CompilerParams(dimension_semantics=tuple(sem), vmem_limit_bytes=V7X_VMEM_LIMIT)


def _rms(x, g):
    return x * lax.rsqrt(jnp.mean(x * x, axis=-1, keepdims=True) + NORM_EPS) * g


def _dot_nt(a, b):
    return lax.dot_general(a, b, (((1,), (1,)), ((), ())), preferred_element_type=F32)


def _dot(a, b):
    return jnp.dot(a, b, preferred_element_type=F32)


def _memkv_kernel(mem_ref, g_ref, w_ref, mk_ref, mv_ref):
    h = _rms(mem_ref[0], g_ref[...]).astype(BF16)
    kv = _dot(h, w_ref[...])
    mk_ref[0] = kv[:, :GROUP_WIDTH].astype(BF16)
    mv_ref[0] = kv[:, GROUP_WIDTH:].astype(BF16)


def _mem_kv(mem, g, w):
    b, m, d = mem.shape
    out = jax.ShapeDtypeStruct((b, m, GROUP_WIDTH), BF16)
    return pl.pallas_call(
        _memkv_kernel,
        grid=(b,),
        in_specs=[pl.BlockSpec((1, m, d), lambda i: (i, 0, 0)),
                  pl.BlockSpec((1, d), lambda i: (0, 0)),
                  pl.BlockSpec((d, 2 * GROUP_WIDTH), lambda i: (0, 0))],
        out_specs=[pl.BlockSpec((1, m, GROUP_WIDTH), lambda i: (i, 0, 0))] * 2,
        out_shape=[out, out],
        compiler_params=_cparams("arbitrary"),
        name="mem_kv",
    )(mem, g, w)


_IN_SEGMENTS = ((DIFF_WIDTH, HEAD_DIM ** -0.5), (DIFF_WIDTH, 1.0), (DIFF_WIDTH, 1.0),
                (GROUP_WIDTH, HEAD_DIM ** -0.5), (GROUP_WIDTH, 1.0), (GROUP_WIDTH, 1.0),
                (GROUP_WIDTH, HEAD_DIM ** -0.5))


def _inproj_kernel(x_ref, g_ref, w_ref, *refs):
    out_refs, w_bf16 = refs[:-1], refs[-1]

    @pl.when(pl.program_id(0) == 0)
    def _():
        w_bf16[...] = w_ref[...].astype(BF16)

    h = _rms(x_ref[...], g_ref[...]).astype(BF16)
    c0 = 0
    for ref, (n, scale) in zip(out_refs, _IN_SEGMENTS):
        p = _dot(h, w_bf16[:, c0:c0 + n])
        if scale != 1.0:
            p = p * scale
        ref[...] = p.astype(BF16)
        c0 += n


def _in_proj(x2d, g, w):
    t, d = x2d.shape
    tm = min(TOK_TILE, t)
    return pl.pallas_call(
        _inproj_kernel,
        grid=(t // tm,),
        in_specs=[pl.BlockSpec((tm, d), lambda i: (i, 0)),
                  pl.BlockSpec((1, d), lambda i: (0, 0)),
                  pl.BlockSpec(w.shape, lambda i: (0, 0))],
        out_specs=[pl.BlockSpec((tm, n), lambda i: (i, 0)) for n, _ in _IN_SEGMENTS],
        out_shape=[jax.ShapeDtypeStruct((t, n), BF16) for n, _ in _IN_SEGMENTS],
        scratch_shapes=[pltpu.VMEM(w.shape, BF16)],
        compiler_params=_cparams("arbitrary"),
        name="in_proj",
    )(x2d, g, w)


def _diff_kernel(slopes_ref, lam_ref, g_ref, q_ref, k_ref, v_ref, kpos_ref, ones_ref, o_ref,
                 m_ref, a_ref, s_ref, *, lam_init, tile):
    h = pl.program_id(1)
    n_full = 2 * pl.program_id(2)
    slope = slopes_ref[h]

    lane = lax.broadcasted_iota(I32, (tile, DIFF_VDIM), 1)
    qpos = jnp.where(lane == 0, slope * float(POS_SPLIT), jnp.where(lane == 1, slope, 0.0)).astype(BF16)
    qs = []
    for sub in range(2):
        q = q_ref[0, sub * tile:(sub + 1) * tile, :]
        zero = jnp.zeros_like(q)
        qs.append(jnp.concatenate([jnp.where(lane < HEAD_DIM, q, zero), qpos], axis=1))
        qs.append(jnp.concatenate([jnp.where(lane >= HEAD_DIM, q, zero), qpos], axis=1))
    sub_a, sub_b, both = (0, 1), (2, 3), (0, 1, 2, 3)

    m_ref[...] = jnp.full(m_ref.shape, NEG, F32)
    a_ref[...] = jnp.zeros(a_ref.shape, F32)

    def scores(j, slot, chains):
        start = pl.multiple_of(j * tile, tile)
        kk = jnp.concatenate([k_ref[0, pl.ds(start, tile), :], kpos_ref[pl.ds(start, tile), :]], axis=1)
        for c in chains:
            s_ref[slot, c] = _dot_nt(qs[c], kk)

    def consume(j, slot, chains, diagonal):
        start = pl.multiple_of(j * tile, tile)
        vv = jnp.concatenate([v_ref[0, pl.ds(start, tile), :], ones_ref[...]], axis=1)
        for c in chains:
            s = s_ref[slot, c]
            if diagonal:
                row = lax.broadcasted_iota(I32, s.shape, 0)
                col = lax.broadcasted_iota(I32, s.shape, 1)
                s = jnp.where(col <= row, s, NEG)
            m_old = m_ref[c]
            m_new = jnp.maximum(m_old, jnp.max(s, axis=1, keepdims=True))
            p = jnp.exp(s - m_new)
            a_ref[c] = jnp.exp(m_old - m_new) * a_ref[c] + _dot(p.astype(BF16), vv)
            m_ref[c] = m_new

    scores(0, 0, both)

    def pair(p, carry):
        j = 2 * p
        scores(j + 1, 1, both)
        consume(j, 0, both, False)
        scores(j + 2, 0, both)
        consume(j + 1, 1, both, False)
        return carry

    lax.fori_loop(0, n_full // 2, pair, 0)
    scores(n_full + 1, 1, sub_b)
    consume(n_full, 0, sub_a, True)
    consume(n_full, 0, sub_b, False)
    consume(n_full + 1, 1, sub_b, True)

    lv = lam_ref[...]
    lam = (jnp.exp(jnp.sum(lv[0:1] * lv[1:2], axis=1, keepdims=True))
           - jnp.exp(jnp.sum(lv[2:3] * lv[3:4], axis=1, keepdims=True)) + lam_init)
    n = DIFF_VDIM
    for sub in range(2):
        a1, a2 = a_ref[2 * sub], a_ref[2 * sub + 1]
        o = (a1[:, :n] * (1.0 / a1[:, n:n + 1])) - lam * (a2[:, :n] * (1.0 / a2[:, n:n + 1]))
        o = _rms(o, g_ref[...]) * (1.0 - lam_init)
        o_ref[0, sub * tile:(sub + 1) * tile, :] = o.astype(o_ref.dtype)


def _diff_attention(qd, kd, vd, slopes, lam_qk, subln_g, lam_init):
    b, s, _ = qd.shape
    tile = ATT_TILE
    assert s % (2 * tile) == 0
    pos = jnp.arange(s, dtype=I32)[:, None]
    lane = jnp.arange(V7X_LANES, dtype=I32)[None, :]
    kpos = jnp.where(lane == 0, pos // POS_SPLIT, jnp.where(lane == 1, pos % POS_SPLIT, 0)).astype(BF16)
    ones = jnp.broadcast_to(lane == 0, (tile, V7X_LANES)).astype(BF16)
    vec = lambda shape: pltpu.VMEM(shape, F32)
    const = lambda a: pl.BlockSpec(a.shape, lambda bb, hh, ii: (0, 0))
    return pl.pallas_call(
        functools.partial(_diff_kernel, lam_init=lam_init, tile=tile),
        grid=(b, N_HEADS, s // (2 * tile)),
        in_specs=[pl.BlockSpec(memory_space=pltpu.SMEM), const(lam_qk), const(subln_g),
                  pl.BlockSpec((1, 2 * tile, DIFF_VDIM), lambda bb, hh, ii: (bb, ii, hh)),
                  pl.BlockSpec((1, s, DIFF_VDIM), lambda bb, hh, ii: (bb, 0, hh)),
                  pl.BlockSpec((1, s, DIFF_VDIM), lambda bb, hh, ii: (bb, 0, hh)),
                  const(kpos), const(ones)],
        out_specs=pl.BlockSpec((1, 2 * tile, DIFF_VDIM), lambda bb, hh, ii: (bb, ii, hh)),
        out_shape=jax.ShapeDtypeStruct((b, s, DIFF_WIDTH), BF16),
        scratch_shapes=[vec((4, tile, 1)), vec((4, tile, 2 * DIFF_VDIM)), vec((2, 4, tile, tile))],
        compiler_params=_cparams("arbitrary", "arbitrary", "arbitrary"),
        name="diff_attn",
    )(slopes, lam_qk, subln_g, qd, kd, vd, kpos, ones)


def _head_mask(lane, h):
    return (lane >= h * HEAD_DIM) & (lane < (h + 1) * HEAD_DIM)


def _split3(x):
    hi = x.astype(BF16)
    rem = x - hi.astype(F32)
    mid = rem.astype(BF16)
    return hi, mid, (rem - mid.astype(F32)).astype(BF16)


def _dil_kernel(q_ref, k_ref, v_ref, perm_ref, bias_ref, spread_ref, o_ref, qc, kc, vc, res_o, res_l, *, seq):
    tile, chunk = DIL_TILE, DIL_CHUNK
    n_blocks, n_chunks = seq // tile, seq // chunk
    lane = lax.broadcasted_iota(I32, (tile, GROUP_WIDTH), 1)
    tail_lane = lax.broadcasted_iota(I32, (tile, V7X_LANES), 1)

    def to_class_order(bi, r):
        n, sr = chunk // r, seq // r

        def one_chunk(t, carry):
            src0 = pl.multiple_of(t * chunk, chunk)
            for src, dst in ((q_ref, qc), (k_ref, kc), (v_ref, vc)):
                y = _dot(perm_ref[bi - 1, 0], src[0, pl.ds(src0, chunk), :]).astype(BF16)
                for rho in range(r):
                    dst[pl.ds(pl.multiple_of(rho * sr + t * n, n), n), :] = y[rho * n:(rho + 1) * n, :]
            return carry

        lax.fori_loop(0, n_chunks, one_chunk, 0)

    def attend(bi, r, load):
        blocks_per_class = (seq // r) // tile

        def one_block(g, carry):
            cur = pl.multiple_of(g * tile, tile)
            prev = pl.multiple_of(jnp.maximum(g - 1, 0) * tile, tile)
            has_prev = (g % blocks_per_class) != 0
            q = load(0, cur)
            qs = jnp.concatenate([jnp.where(_head_mask(lane, h), q, jnp.zeros_like(q)) for h in range(N_HEADS)], axis=0)
            kk = jnp.concatenate([load(1, prev), load(1, cur)], axis=0)
            vv = jnp.concatenate([load(2, prev), load(2, cur)], axis=0)
            bias = jnp.where(has_prev, bias_ref[bi, 1], bias_ref[bi, 0])
            s = _dot_nt(qs, kk) + bias
            m = jnp.max(s, axis=1, keepdims=True)
            p = jnp.exp(s - m)
            l = jnp.sum(p, axis=1, keepdims=True)
            pv = _dot(p.astype(BF16), vv) * (1.0 / l)
            lse = m + jnp.log(l)
            out = jnp.zeros((tile, GROUP_WIDTH), F32)
            tail = jnp.zeros((tile, V7X_LANES), F32)
            for h in range(N_HEADS):
                rows = slice(h * tile, (h + 1) * tile)
                out = jnp.where(_head_mask(lane, h), pv[rows], out)
                tail = jnp.where(tail_lane == h, lse[rows], tail)
            res_o[bi, pl.ds(cur, tile), :] = out.astype(BF16)
            res_l[bi, pl.ds(cur, tile), :] = tail
            return carry

        lax.fori_loop(0, n_blocks, one_block, 0, unroll=8)

    refs = (q_ref, k_ref, v_ref)
    scratch = (qc, kc, vc)
    attend(0, 1, lambda which, start: refs[which][0, pl.ds(start, tile), :])
    for bi, r in enumerate(DIL_DILATIONS[1:], start=1):
        to_class_order(bi, r)
        attend(bi, r, lambda which, start: scratch[which][pl.ds(start, tile), :])

    def merge_chunk(t, carry):
        base = pl.multiple_of(t * chunk, chunk)
        def pack3(lse):
            hi, mid, lo = _split3(lse)
            return (hi.astype(F32) + pltpu.roll(mid.astype(F32), N_HEADS, 1)
                    + pltpu.roll(lo.astype(F32), 2 * N_HEADS, 1)).astype(BF16)

        outs, lses = [], []
        for bi, r in enumerate(DIL_DILATIONS):
            if r == 1:
                out = res_o[0, pl.ds(base, chunk), :].astype(F32)
                packed = pack3(res_l[0, pl.ds(base, chunk), :])
            else:
                n, sr = chunk // r, seq // r
                rows = [pl.ds(pl.multiple_of(rho * sr + t * n, n), n) for rho in range(r)]
                back = perm_ref[bi - 1, 1]
                out = _dot(back, jnp.concatenate([res_o[bi, rw, :] for rw in rows], axis=0))
                packed = _dot(back, pack3(jnp.concatenate([res_l[bi, rw, :] for rw in rows], axis=0))).astype(BF16)
            outs.append(out)
            lses.append(_dot(packed, spread_ref[...]))
        big = jnp.maximum(jnp.maximum(lses[0], lses[1]), lses[2])
        wts = [jnp.exp(z - big) for z in lses]
        o = (wts[0] * outs[0] + wts[1] * outs[1] + wts[2] * outs[2]) * (1.0 / (wts[0] + wts[1] + wts[2]))
        o_ref[0, pl.ds(base, chunk), :] = o.astype(o_ref.dtype)
        return carry

    lax.fori_loop(0, n_chunks, merge_chunk, 0)


def _dil_constants(slopes, seq):
    chunk, tile = DIL_CHUNK, DIL_TILE
    perms = []
    for r in DIL_DILATIONS[1:]:
        n = chunk // r
        dst = jnp.arange(chunk, dtype=I32)
        src = r * (dst % n) + dst // n
        fwd = (src[:, None] == jnp.arange(chunk, dtype=I32)[None, :]).astype(BF16)
        perms.append(jnp.stack([fwd, fwd.T]))
    m = jnp.arange(tile, dtype=I32)[:, None]
    j = jnp.arange(2 * tile, dtype=I32)[None, :]
    dist = jnp.where(j < tile, m - j + tile, m - (j - tile))
    ok_cur = (j >= tile) & (dist >= 0)
    ok_prev = (j < tile) & (dist <= DIL_WINDOW_KEYS)
    biases = []
    for r in DIL_DILATIONS:
        per_head = [-(slopes[h] * r) * dist.astype(F32) for h in range(N_HEADS)]
        with_prev = jnp.concatenate([jnp.where(ok_cur | ok_prev, b, NEG) for b in per_head], axis=0)
        no_prev = jnp.concatenate([jnp.where(ok_cur, b, NEG) for b in per_head], axis=0)
        biases.append(jnp.stack([no_prev, with_prev]))
    src_lane = jnp.arange(V7X_LANES, dtype=I32)[:, None]
    spread = ((src_lane < 3 * N_HEADS)
              & (src_lane % N_HEADS == jnp.arange(GROUP_WIDTH, dtype=I32)[None, :] // HEAD_DIM)).astype(BF16)
    return jnp.stack(perms), jnp.stack(biases), spread


def _dil_attention(ql, kl, vl, slopes):
    b, s, w = ql.shape
    perms, biases, spread = _dil_constants(slopes, s)
    row = pl.BlockSpec((1, s, w), lambda i: (i, 0, 0))
    const = lambda a: pl.BlockSpec(a.shape, lambda i: (0,) * a.ndim)
    return pl.pallas_call(
        functools.partial(_dil_kernel, seq=s),
        grid=(b,),
        in_specs=[row, row, row, const(perms), const(biases), const(spread)],
        out_specs=row,
        out_shape=jax.ShapeDtypeStruct((b, s, w), BF16),
        scratch_shapes=[pltpu.VMEM((s, w), BF16)] * 3
                       + [pltpu.VMEM((len(DIL_DILATIONS), s, w), BF16),
                          pltpu.VMEM((len(DIL_DILATIONS), s, V7X_LANES), F32)],
        compiler_params=_cparams("arbitrary"),
        name="dil_attn",
    )(ql, kl, vl, perms, biases, spread)


def _memattn_kernel(q_ref, mk_ref, mv_ref, o_ref):
    q = q_ref[0]
    mk, mv = mk_ref[0], mv_ref[0]
    lane = lax.broadcasted_iota(I32, q.shape, 1)
    out = jnp.zeros(q.shape, F32)
    for h in range(N_HEADS):
        hm = _head_mask(lane, h)
        s = _dot_nt(jnp.where(hm, q, jnp.zeros_like(q)), mk)
        m = jnp.max(s, axis=1, keepdims=True)
        p = jnp.exp(s - m)
        l = jnp.sum(p, axis=1, keepdims=True)
        out = jnp.where(hm, _dot(p.astype(BF16), mv) * (1.0 / l), out)
    o_ref[0] = out.astype(o_ref.dtype)


def _mem_attention(mq, mk, mv):
    b, s, w = mq.shape
    m = mk.shape[1]
    tile = min(TOK_TILE, s)
    return pl.pallas_call(
        _memattn_kernel,
        grid=(b, s // tile),
        in_specs=[pl.BlockSpec((1, tile, w), lambda bb, ii: (bb, ii, 0)),
                  pl.BlockSpec((1, m, w), lambda bb, ii: (bb, 0, 0)),
                  pl.BlockSpec((1, m, w), lambda bb, ii: (bb, 0, 0))],
        out_specs=pl.BlockSpec((1, tile, w), lambda bb, ii: (bb, ii, 0)),
        out_shape=jax.ShapeDtypeStruct((b, s, w), BF16),
        compiler_params=_cparams("arbitrary", "arbitrary"),
        name="mem_attn",
    )(mq, mk, mv)


def _outrouter_kernel(x_ref, od_ref, ol_ref, om_ref, wo_ref, g_ref, wr_ref, br_ref,
                      x2_ref, h2_ref, eidx_ref, rank_ref, gate_ref, cnt_ref, run_ref, *, tm):
    step = pl.program_id(0)

    @pl.when(step == 0)
    def _():
        run_ref[...] = jnp.zeros(run_ref.shape, F32)

    c1, c2 = DIFF_WIDTH, DIFF_WIDTH + GROUP_WIDTH
    mix = (_dot(od_ref[...], wo_ref[0:c1, :]) + _dot(ol_ref[...], wo_ref[c1:c2, :])
           + _dot(om_ref[...], wo_ref[c2:, :]))
    x2 = x_ref[...] + mix
    x2_ref[...] = x2
    h2 = _rms(x2, g_ref[...])
    h2_ref[...] = h2

    h_hi = h2.astype(BF16)
    h_lo = (h2 - h_hi.astype(F32)).astype(BF16)
    logits = (_dot(h_hi, wr_ref[0]) + (_dot(h_hi, wr_ref[1]) + _dot(h_lo, wr_ref[0]))) + br_ref[...]
    lane = lax.broadcasted_iota(I32, logits.shape, 1)
    lane_rev = (V7X_LANES - 1 - lane).astype(F32)
    work = logits
    sel = jnp.zeros(logits.shape, F32)
    vals, idxs = [], []
    for _ in range(TOP_K):
        mx = jnp.max(work, axis=1, keepdims=True)
        key = jnp.max(jnp.where(work == mx, lane_rev, -1.0), axis=1, keepdims=True)
        hit = lane_rev == key
        sel = jnp.where(hit, 1.0, sel)
        work = jnp.where(hit, -jnp.inf, work)
        vals.append(mx)
        idxs.append(V7X_LANES - 1 - key.astype(I32))

    r_io = lax.broadcasted_iota(I32, (tm, tm), 0)
    c_io = lax.broadcasted_iota(I32, (tm, tm), 1)
    tri = jnp.where(c_io < r_io, 1.0, 0.0).astype(BF16)
    before = _dot(tri, sel.astype(BF16)) + run_ref[...]
    run_ref[...] = run_ref[...] + jnp.sum(sel, axis=0, keepdims=True)
    cnt_ref[...] = run_ref[...].astype(I32)

    exps = [jnp.exp(v - vals[0]) for v in vals]
    inv = 1.0 / (exps[0] + exps[1] + exps[2] + exps[3])
    eidx = jnp.zeros(logits.shape, I32)
    rank = jnp.zeros(logits.shape, I32)
    gate = jnp.zeros(logits.shape, F32)
    for k in range(TOP_K):
        rk = jnp.sum(jnp.where(lane == idxs[k], before, 0.0), axis=1, keepdims=True)
        eidx = jnp.where(lane == k, idxs[k], eidx)
        rank = jnp.where(lane == k, rk.astype(I32), rank)
        gate = jnp.where(lane == k, exps[k] * inv, gate)
    eidx_ref[...] = eidx
    rank_ref[...] = rank
    gate_ref[...] = gate


def _out_router(x2d, o_diff, o_dil, o_mem, w_out, g, w_router_p, b_router_p):
    t, d = x2d.shape
    tm = min(TOK_TILE, t)
    row = lambda n: pl.BlockSpec((tm, n), lambda i: (i, 0))
    full = lambda a: pl.BlockSpec(a.shape, lambda i: (0,) * a.ndim)
    return pl.pallas_call(
        functools.partial(_outrouter_kernel, tm=tm),
        grid=(t // tm,),
        in_specs=[row(d), row(DIFF_WIDTH), row(GROUP_WIDTH), row(GROUP_WIDTH),
                  full(w_out), full(g), full(w_router_p), full(b_router_p)],
        out_specs=[row(d), row(d), row(V7X_LANES), row(V7X_LANES), row(V7X_LANES),
                   pl.BlockSpec((1, V7X_LANES), lambda i: (0, 0))],
        out_shape=[jax.ShapeDtypeStruct((t, d), F32), jax.ShapeDtypeStruct((t, d), F32),
                   jax.ShapeDtypeStruct((t, V7X_LANES), I32), jax.ShapeDtypeStruct((t, V7X_LANES), I32),
                   jax.ShapeDtypeStruct((t, V7X_LANES), F32), jax.ShapeDtypeStruct((1, V7X_LANES), I32)],
        scratch_shapes=[pltpu.VMEM((1, V7X_LANES), F32)],
        compiler_params=_cparams("arbitrary"),
        name="out_router",
    )(x2d, o_diff, o_dil, o_mem, w_out, g, w_router_p, b_router_p)


def _row_copy(src, dst, sem):
    return pltpu.make_async_copy(src, dst, sem)


def _dispatch_kernel(dest_ref, h_ref, xs_hbm, sem, *, tm):
    def issue(t, carry):
        for k in range(TOP_K):
            d = dest_ref[0, 0, t * TOP_K + k]
            _row_copy(h_ref.at[pl.ds(t, 1), :], xs_hbm.at[pl.ds(d, 1), :], sem).start()
        return carry

    lax.fori_loop(0, tm, issue, 0, unroll=DMA_UNROLL)
    for _ in range(TOP_K):
        _row_copy(h_ref, xs_hbm.at[pl.ds(0, tm), :], sem).wait()


def _dispatch(h2, dest):
    t, d = h2.shape
    tm = min(DISPATCH_TILE, t)
    dest3 = dest.reshape(t // tm, 1, tm * TOP_K)
    return pl.pallas_call(
        functools.partial(_dispatch_kernel, tm=tm),
        grid=(t // tm,),
        in_specs=[pl.BlockSpec((1, 1, tm * TOP_K), lambda i: (i, 0, 0), memory_space=pltpu.SMEM),
                  pl.BlockSpec((tm, d), lambda i: (i, 0))],
        out_specs=pl.BlockSpec(memory_space=pl.ANY),
        out_shape=jax.ShapeDtypeStruct((t * TOP_K, d), h2.dtype),
        scratch_shapes=[pltpu.SemaphoreType.DMA],
        compiler_params=_cparams("arbitrary"),
        name="dispatch",
    )(dest3, h2)


def _split_gate_up_kernel(w_ref, sel_ref, wg_ref, wu_ref):
    x = w_ref[0].astype(BF16)
    k, n = sel_ref.shape[1], sel_ref.shape[2]
    for j in range(wg_ref.shape[2] // n):
        blk = x[:, j * k:(j + 1) * k]
        wg_ref[0, :, j * n:(j + 1) * n] = _dot(blk, sel_ref[0]).astype(BF16)
        wu_ref[0, :, j * n:(j + 1) * n] = _dot(blk, sel_ref[1]).astype(BF16)


def _split_gate_up(w):
    e, d, f2 = w.shape
    f = f2 // 2
    rows, n = min(TOK_TILE, d), V7X_MXU_DIM
    src = jnp.arange(2 * n, dtype=I32)[:, None]
    dst = jnp.arange(n, dtype=I32)[None, :]
    sel = jnp.stack([src == 2 * dst, src == 2 * dst + 1]).astype(BF16)
    out = jax.ShapeDtypeStruct((e, d, f), BF16)
    return pl.pallas_call(
        _split_gate_up_kernel,
        grid=(e, d // rows),
        in_specs=[pl.BlockSpec((1, rows, f2), lambda i, j: (i, j, 0)),
                  pl.BlockSpec(sel.shape, lambda i, j: (0, 0, 0))],
        out_specs=[pl.BlockSpec((1, rows, f), lambda i, j: (i, j, 0))] * 2,
        out_shape=[out, out],
        compiler_params=_cparams("arbitrary", "arbitrary"),
        name="split_gate_up",
    )(w, sel)


def _experts_kernel(tile_ref, exp_ref, lo_ref, hi_ref, act_ref, fresh_ref,
                    xs_ref, wg_ref, wu_ref, bg_ref, bu_ref, wd_ref, bd_ref, y_ref, wd_bf16, *, tm):
    w = pl.program_id(0)

    @pl.when(fresh_ref[w] == 1)
    def _():
        wd_bf16[...] = wd_ref[0].astype(BF16)

    @pl.when(act_ref[w] == 1)
    def _():
        x = xs_ref[...].astype(BF16)
        gate = jnp.minimum(_dot(x, wg_ref[0]) + bg_ref[0], SWIGLU_LIMIT)
        up = jnp.clip(_dot(x, wu_ref[0]) + bu_ref[0], -SWIGLU_LIMIT, SWIGLU_LIMIT)
        act = (up + 1.0) * (gate * jax.nn.sigmoid(gate * SWIGLU_ALPHA))
        y = _dot(act.astype(BF16), wd_bf16[...]) + bd_ref[0]
        row = lax.broadcasted_iota(I32, (tm, V7X_LANES), 0)
        mine = (row >= lo_ref[w]) & (row < hi_ref[w])
        for s in range(ROW_SUBLANES):
            y_ref[pl.ds(s, tm, stride=ROW_SUBLANES), :] = jnp.where(mine, y[:, s * V7X_LANES:(s + 1) * V7X_LANES], 0.0)

    @pl.when(act_ref[w] == 0)
    def _():
        y_ref[...] = jnp.zeros(y_ref.shape, F32)


def _experts(xs, meta, wg, wu, bg, bu, wd, bd):
    n, d = xs.shape
    e, _, f = wg.shape
    tm = min(ROW_TILE, n)
    n_items = meta[0].shape[0]
    xmap = lambda w, tile, ex, *_: (tile[w], 0)
    wmap = lambda w, tile, ex, *_: (ex[w], 0, 0)
    grid_spec = pltpu.PrefetchScalarGridSpec(
        num_scalar_prefetch=len(meta),
        grid=(n_items,),
        in_specs=[pl.BlockSpec((tm, d), xmap),
                  pl.BlockSpec((1, d, f), wmap), pl.BlockSpec((1, d, f), wmap),
                  pl.BlockSpec((1, 1, f), wmap), pl.BlockSpec((1, 1, f), wmap),
                  pl.BlockSpec((1, f, d), wmap), pl.BlockSpec((1, 1, d), wmap)],
        out_specs=pl.BlockSpec((tm * ROW_SUBLANES, V7X_LANES), lambda w, *_: (w, 0)),
        scratch_shapes=[pltpu.VMEM((f, d), BF16)],
    )
    assert d == ROW_SUBLANES * V7X_LANES
    return pl.pallas_call(
        functools.partial(_experts_kernel, tm=tm),
        grid_spec=grid_spec,
        out_shape=jax.ShapeDtypeStruct((n_items * tm * ROW_SUBLANES, V7X_LANES), F32),
        compiler_params=_cparams("arbitrary"),
        name="experts",
    )(*meta, xs, wg, wu, bg, bu, wd, bd)


def _expert_work_items(counts, n_rows, tm):
    e = counts.shape[0]
    n_tiles = n_rows // tm
    n_items = n_tiles + e - 1
    ends = jnp.cumsum(counts)
    starts = ends - counts
    first_tile = starts // tm
    n_t = jnp.where(counts > 0, (ends - 1) // tm - first_tile + 1, 0)
    item_end = jnp.cumsum(n_t)
    item_start = item_end - n_t
    total = item_end[-1]
    w = jnp.arange(n_items, dtype=I32)
    active = w < total
    w_eff = jnp.minimum(w, total - 1)
    ex = jnp.minimum(jnp.sum((item_end[None, :] <= w_eff[:, None]).astype(I32), axis=1), e - 1)
    mine = ex[:, None] == jnp.arange(e, dtype=I32)[None, :]
    pick = lambda v: jnp.sum(jnp.where(mine, v[None, :], 0), axis=1)
    tile = (pick(first_tile) + (w_eff - pick(item_start))).astype(I32)
    lo = jnp.clip(pick(starts) - tile * tm, 0, tm).astype(I32)
    hi = jnp.clip(pick(ends) - tile * tm, 0, tm).astype(I32)
    fresh = jnp.concatenate([jnp.ones((1,), I32), (ex[1:] != ex[:-1]).astype(I32)])
    return (tile, ex, lo, hi, active.astype(I32), fresh), first_tile, item_start


def _combine_kernel(dest_ref, dest_next_ref, x2_ref, gate_ref, g_ref, y_hbm, o_ref, ybuf, sems, *, tm, n_steps):
    i = pl.program_id(0)
    slot = i % 2

    def gather(dref, into):
        def issue(t, carry):
            for k in range(TOP_K):
                src = pl.multiple_of(dref[0, 0, t * TOP_K + k] * ROW_SUBLANES, ROW_SUBLANES)
                dst = pl.multiple_of(t * ROW_SUBLANES, ROW_SUBLANES)
                _row_copy(y_hbm.at[pl.ds(src, ROW_SUBLANES), :], ybuf.at[into, k, pl.ds(dst, ROW_SUBLANES), :],
                          sems.at[into]).start()
            return carry
        lax.fori_loop(0, tm, issue, 0, unroll=DMA_UNROLL)

    @pl.when(i == 0)
    def _():
        gather(dest_ref, 0)

    @pl.when(i + 1 < n_steps)
    def _():
        gather(dest_next_ref, 1 - slot)

    for k in range(TOP_K):
        _row_copy(y_hbm.at[pl.ds(0, tm * ROW_SUBLANES), :], ybuf.at[slot, k], sems.at[slot]).wait()

    gates = gate_ref[...]
    cols = []
    for s in range(ROW_SUBLANES):
        acc = x2_ref[:, s * V7X_LANES:(s + 1) * V7X_LANES]
        for k in range(TOP_K):
            acc = acc + ybuf[slot, k, pl.ds(s, tm, stride=ROW_SUBLANES), :] * gates[:, k:k + 1]
        cols.append(acc)
    o_ref[...] = _rms(jnp.concatenate(cols, axis=1), g_ref[...])


def _combine(y, dest, x2, gates, g_final):
    t, d = x2.shape
    tm = min(DMA_TILE, t)
    n_steps = t // tm
    dest3 = dest.reshape(n_steps, 1, tm * TOP_K)
    dest_spec = lambda f: pl.BlockSpec((1, 1, tm * TOP_K), f, memory_space=pltpu.SMEM)
    return pl.pallas_call(
        functools.partial(_combine_kernel, tm=tm, n_steps=n_steps),
        grid=(n_steps,),
        in_specs=[dest_spec(lambda i: (i, 0, 0)),
                  dest_spec(lambda i: (jnp.minimum(i + 1, n_steps - 1), 0, 0)),
                  pl.BlockSpec((tm, d), lambda i: (i, 0)),
                  pl.BlockSpec((tm, V7X_LANES), lambda i: (i, 0)),
                  pl.BlockSpec((1, d), lambda i: (0, 0)),
                  pl.BlockSpec(memory_space=pl.ANY)],
        out_specs=pl.BlockSpec((tm, d), lambda i: (i, 0)),
        out_shape=jax.ShapeDtypeStruct((t, d), F32),
        scratch_shapes=[pltpu.VMEM((2, TOP_K, tm * ROW_SUBLANES, V7X_LANES), F32), pltpu.SemaphoreType.DMA((2,))],
        compiler_params=_cparams("arbitrary"),
        name="combine",
    )(dest3, dest3, x2, gates, g_final, y)


def _alibi_slopes():
    n = 2 * N_HEADS
    s = 2.0 ** (-8.0 * (jnp.arange(n, dtype=F32) + 1.0) / n)
    return s[0::2], s[1::2]


def kernel(x, mem, attn_norm_g, mem_norm_g, w_in, diff_lambda_qk, diff_subln_g, w_mem_kv, w_out,
           ffn_norm_g, w_router, b_router, w_gate_up, b_gate_up, w_down, b_down, final_norm_g):
    b, s, d = x.shape
    t = b * s
    assert w_in.shape[0] == 1, "the final norm is fused into the layer's combine step: one layer only"
    assert s % (DIL_DILATIONS[-1] * DIL_TILE) == 0 and t % TOK_TILE == 0
    n_exp = w_router.shape[-1]
    diff_slopes, dil_slopes = _alibi_slopes()
    x2d = x.reshape(t, d)

    for l in range(1):
        lam_init = 0.8 - 0.6 * math.exp(-0.3 * l)
        mk, mv = _mem_kv(mem, mem_norm_g[l][None], w_mem_kv[l].astype(BF16))
        qd, kd, vd, ql, kl, vl, mq = _in_proj(x2d, attn_norm_g[l][None], w_in[l])
        sh = lambda a: a.reshape(b, s, a.shape[-1])
        o_diff = _diff_attention(sh(qd), sh(kd), sh(vd), diff_slopes, diff_lambda_qk[l],
                                 diff_subln_g[l][None], lam_init)
        o_dil = _dil_attention(sh(ql), sh(kl), sh(vl), dil_slopes)
        o_mem = _mem_attention(sh(mq), mk, mv)

        w_router_f = jnp.zeros((d, V7X_LANES), F32).at[:, :n_exp].set(w_router[l])
        w_router_hi = w_router_f.astype(BF16)
        w_router_p = jnp.stack([w_router_hi, (w_router_f - w_router_hi.astype(F32)).astype(BF16)])
        b_router_p = jnp.full((1, V7X_LANES), NEG, F32).at[0, :n_exp].set(b_router[l])
        x2, h2, eidx, rank, gates, cnt = _out_router(
            x2d, o_diff.reshape(t, DIFF_WIDTH), o_dil.reshape(t, GROUP_WIDTH), o_mem.reshape(t, GROUP_WIDTH),
            w_out[l].astype(BF16), ffn_norm_g[l][None], w_router_p, b_router_p)

        counts = cnt[0, :n_exp]
        starts = jnp.cumsum(counts) - counts
        e_sel = eidx[:, :TOP_K]
        onehot = e_sel[:, :, None] == jnp.arange(n_exp, dtype=I32)[None, None, :]
        pick = lambda v: jnp.sum(jnp.where(onehot, v[None, None, :], 0), axis=-1)
        dest = (rank[:, :TOP_K] + pick(starts)).astype(I32)

        xs = _dispatch(h2, dest)

        f = w_down.shape[2]
        tm_rows = min(ROW_TILE, t * TOP_K)
        w_gate, w_up = _split_gate_up(w_gate_up[l])
        bgu = b_gate_up[l].reshape(n_exp, 1, f, 2)
        meta, first_tile, item_start = _expert_work_items(counts, t * TOP_K, tm_rows)
        y = _experts(xs, meta, w_gate, w_up, bgu[..., 0], bgu[..., 1], w_down[l], b_down[l][:, None, :])

        item = pick(item_start) + dest // tm_rows - pick(first_tile)
        y_row = (item * tm_rows + dest % tm_rows).astype(I32)
        out = _combine(y, y_row, x2, gates, final_norm_g[None])
    return out.reshape(b, s, d)
```

```python
import functools
import math

import jax
import jax.numpy as jnp
from jax import lax
from jax.experimental import pallas as pl
from jax.experimental.pallas import tpu as pltpu

F32 = jnp.float32
BF16 = jnp.bfloat16
I32 = jnp.int32

NORM_EPS = 1e-5
HEAD_DIM = 64
N_HEADS = 4
DIFF_VDIM = 2 * HEAD_DIM
DIFF_WIDTH = N_HEADS * DIFF_VDIM
GROUP_WIDTH = N_HEADS * HEAD_DIM
DIL_DILATIONS = (1, 4, 16)
DIL_WINDOW_KEYS = 128
TOP_K = 4
SWIGLU_ALPHA = 1.702
SWIGLU_LIMIT = 7.0
NEG = -1e30

V7X_LANES = 128
V7X_SUBLANES = 8
V7X_MXU_DIM = 256
V7X_VMEM_LIMIT = 56 * 1024 * 1024

TOK_TILE = 512
ATT_TILE = 512
POS_SPLIT = 64
DIFF_HEADS_PER_STEP = 1
DIL_TILE = DIL_WINDOW_KEYS
DIL_CHUNK = 256
ROW_TILE = 512
ROW_SUBLANES = V7X_SUBLANES
DMA_TILE = 256
DISPATCH_TILE = 1024
DMA_UNROLL = 4


def _cparams(*sem):
    return pltpu.CompilerParams(dimension_semantics=tuple(sem), vmem_limit_bytes=V7X_VMEM_LIMIT)


def _rms(x, g):
    return x * lax.rsqrt(jnp.mean(x * x, axis=-1, keepdims=True) + NORM_EPS) * g


def _dot_nt(a, b):
    return lax.dot_general(a, b, (((1,), (1,)), ((), ())), preferred_element_type=F32)


def _dot(a, b):
    return jnp.dot(a, b, preferred_element_type=F32)


def _memkv_kernel(mem_ref, g_ref, w_ref, mk_ref, mv_ref):
    h = _rms(mem_ref[0], g_ref[...]).astype(BF16)
    kv = _dot(h, w_ref[...])
    mk_ref[0] = kv[:, :GROUP_WIDTH].astype(BF16)
    mv_ref[0] = kv[:, GROUP_WIDTH:].astype(BF16)


def _mem_kv(mem, g, w):
    b, m, d = mem.shape
    out = jax.ShapeDtypeStruct((b, m, GROUP_WIDTH), BF16)
    return pl.pallas_call(
        _memkv_kernel,
        grid=(b,),
        in_specs=[pl.BlockSpec((1, m, d), lambda i: (i, 0, 0)),
                  pl.BlockSpec((1, d), lambda i: (0, 0)),
                  pl.BlockSpec((d, 2 * GROUP_WIDTH), lambda i: (0, 0))],
        out_specs=[pl.BlockSpec((1, m, GROUP_WIDTH), lambda i: (i, 0, 0))] * 2,
        out_shape=[out, out],
        compiler_params=_cparams("arbitrary"),
        name="mem_kv",
    )(mem, g, w)


_IN_SEGMENTS = ((DIFF_WIDTH, HEAD_DIM ** -0.5), (DIFF_WIDTH, 1.0), (DIFF_WIDTH, 1.0),
                (GROUP_WIDTH, HEAD_DIM ** -0.5), (GROUP_WIDTH, 1.0), (GROUP_WIDTH, 1.0),
                (GROUP_WIDTH, HEAD_DIM ** -0.5))


def _inproj_kernel(x_ref, g_ref, w_ref, *refs):
    out_refs, w_bf16 = refs[:-1], refs[-1]

    @pl.when(pl.program_id(0) == 0)
    def _():
        w_bf16[...] = w_ref[...].astype(BF16)

    h = _rms(x_ref[...], g_ref[...]).astype(BF16)
    c0 = 0
    for ref, (n, scale) in zip(out_refs, _IN_SEGMENTS):
        p = _dot(h, w_bf16[:, c0:c0 + n])
        if scale != 1.0:
            p = p * scale
        ref[...] = p.astype(BF16)
        c0 += n


def _in_proj(x2d, g, w):
    t, d = x2d.shape
    tm = min(TOK_TILE, t)
    return pl.pallas_call(
        _inproj_kernel,
        grid=(t // tm,),
        in_specs=[pl.BlockSpec((tm, d), lambda i: (i, 0)),
                  pl.BlockSpec((1, d), lambda i: (0, 0)),
                  pl.BlockSpec(w.shape, lambda i: (0, 0))],
        out_specs=[pl.BlockSpec((tm, n), lambda i: (i, 0)) for n, _ in _IN_SEGMENTS],
        out_shape=[jax.ShapeDtypeStruct((t, n), BF16) for n, _ in _IN_SEGMENTS],
        scratch_shapes=[pltpu.VMEM(w.shape, BF16)],
        compiler_params=_cparams("arbitrary"),
        name="in_proj",
    )(x2d, g, w)


def _diff_kernel(slopes_ref, lam_ref, g_ref, q_ref, k_ref, v_ref, kpos_ref, ones_ref, o_ref,
                 m_ref, a_ref, s_ref, *, lam_init, tile):
    n_full = 2 * pl.program_id(2)
    lane = lax.broadcasted_iota(I32, (tile, DIFF_VDIM), 1)
    heads = range(DIFF_HEADS_PER_STEP)
    qs = []
    for hd in heads:
        slope = slopes_ref[DIFF_HEADS_PER_STEP * pl.program_id(1) + hd]
        qpos = jnp.where(lane == 0, slope * float(POS_SPLIT), jnp.where(lane == 1, slope, 0.0)).astype(BF16)
        for sub in range(2):
            q = q_ref[0, sub * tile:(sub + 1) * tile, hd * DIFF_VDIM:(hd + 1) * DIFF_VDIM]
            zero = jnp.zeros_like(q)
            qs.append(jnp.concatenate([jnp.where(lane < HEAD_DIM, q, zero), qpos], axis=1))
            qs.append(jnp.concatenate([jnp.where(lane >= HEAD_DIM, q, zero), qpos], axis=1))
    chains = lambda subs: [(hd, 4 * hd + 2 * sub + mp) for hd in heads for sub in subs for mp in range(2)]
    sub_a, sub_b, both = chains((0,)), chains((1,)), chains((0, 1))

    m_ref[...] = jnp.full(m_ref.shape, NEG, F32)
    a_ref[...] = jnp.zeros(a_ref.shape, F32)

    def scores(j, slot, which):
        start = pl.multiple_of(j * tile, tile)
        kk = [jnp.concatenate([k_ref[0, pl.ds(start, tile), hd * DIFF_VDIM:(hd + 1) * DIFF_VDIM],
                               kpos_ref[pl.ds(start, tile), :]], axis=1) for hd in heads]
        for hd, c in which:
            s_ref[slot, c] = _dot_nt(qs[c], kk[hd])

    def consume(j, slot, which, diagonal):
        start = pl.multiple_of(j * tile, tile)
        vv = [jnp.concatenate([v_ref[0, pl.ds(start, tile), hd * DIFF_VDIM:(hd + 1) * DIFF_VDIM], ones_ref[...]],
                              axis=1) for hd in heads]
        for hd, c in which:
            s = s_ref[slot, c]
            if diagonal:
                row = lax.broadcasted_iota(I32, s.shape, 0)
                col = lax.broadcasted_iota(I32, s.shape, 1)
                s = jnp.where(col <= row, s, NEG)
            m_old = m_ref[c]
            m_new = jnp.maximum(m_old, jnp.max(s, axis=1, keepdims=True))
            p = jnp.exp(s - m_new)
            a_ref[c] = jnp.exp(m_old - m_new) * a_ref[c] + _dot(p.astype(BF16), vv[hd])
            m_ref[c] = m_new

    scores(0, 0, both)

    def pair(p, carry):
        j = 2 * p
        scores(j + 1, 1, both)
        consume(j, 0, both, False)
        scores(j + 2, 0, both)
        consume(j + 1, 1, both, False)
        return carry

    lax.fori_loop(0, n_full // 2, pair, 0)
    scores(n_full + 1, 1, sub_b)
    consume(n_full, 0, sub_a, True)
    consume(n_full, 0, sub_b, False)
    consume(n_full + 1, 1, sub_b, True)

    lv = lam_ref[...]
    lam = (jnp.exp(jnp.sum(lv[0:1] * lv[1:2], axis=1, keepdims=True))
           - jnp.exp(jnp.sum(lv[2:3] * lv[3:4], axis=1, keepdims=True)) + lam_init)
    n = DIFF_VDIM
    for hd in heads:
        for sub in range(2):
            a1, a2 = a_ref[4 * hd + 2 * sub], a_ref[4 * hd + 2 * sub + 1]
            o = (a1[:, :n] * (1.0 / a1[:, n:n + 1])) - lam * (a2[:, :n] * (1.0 / a2[:, n:n + 1]))
            o = _rms(o, g_ref[...]) * (1.0 - lam_init)
            o_ref[0, sub * tile:(sub + 1) * tile, hd * n:(hd + 1) * n] = o.astype(o_ref.dtype)


def _diff_attention(qd, kd, vd, slopes, lam_qk, subln_g, lam_init):
    b, s, _ = qd.shape
    tile = ATT_TILE
    assert s % (2 * tile) == 0
    pos = jnp.arange(s, dtype=I32)[:, None]
    lane = jnp.arange(V7X_LANES, dtype=I32)[None, :]
    kpos = jnp.where(lane == 0, pos // POS_SPLIT, jnp.where(lane == 1, pos % POS_SPLIT, 0)).astype(BF16)
    ones = jnp.broadcast_to(lane == 0, (tile, V7X_LANES)).astype(BF16)
    vec = lambda shape: pltpu.VMEM(shape, F32)
    const = lambda a: pl.BlockSpec(a.shape, lambda bb, hh, ii: (0, 0))
    width = DIFF_HEADS_PER_STEP * DIFF_VDIM
    n_chains = 4 * DIFF_HEADS_PER_STEP
    return pl.pallas_call(
        functools.partial(_diff_kernel, lam_init=lam_init, tile=tile),
        grid=(b, N_HEADS // DIFF_HEADS_PER_STEP, s // (2 * tile)),
        in_specs=[pl.BlockSpec(memory_space=pltpu.SMEM), const(lam_qk), const(subln_g),
                  pl.BlockSpec((1, 2 * tile, width), lambda bb, hh, ii: (bb, ii, hh)),
                  pl.BlockSpec((1, s, width), lambda bb, hh, ii: (bb, 0, hh)),
                  pl.BlockSpec((1, s, width), lambda bb, hh, ii: (bb, 0, hh)),
                  const(kpos), const(ones)],
        out_specs=pl.BlockSpec((1, 2 * tile, width), lambda bb, hh, ii: (bb, ii, hh)),
        out_shape=jax.ShapeDtypeStruct((b, s, DIFF_WIDTH), BF16),
        scratch_shapes=[vec((n_chains, tile, 1)), vec((n_chains, tile, 2 * DIFF_VDIM)),
                        vec((2, n_chains, tile, tile))],
        compiler_params=_cparams("arbitrary", "arbitrary", "arbitrary"),
        name="diff_attn",
    )(slopes, lam_qk, subln_g, qd, kd, vd, kpos, ones)


def _head_mask(lane, h):
    return (lane >= h * HEAD_DIM) & (lane < (h + 1) * HEAD_DIM)


def _split3(x):
    hi = x.astype(BF16)
    rem = x - hi.astype(F32)
    mid = rem.astype(BF16)
    return hi, mid, (rem - mid.astype(F32)).astype(BF16)


def _dil_kernel(q_ref, k_ref, v_ref, perm_ref, bias_ref, spread_ref, o_ref, qc, kc, vc, res_o, res_l, *, seq):
    tile, chunk = DIL_TILE, DIL_CHUNK
    n_blocks, n_chunks = seq // tile, seq // chunk
    lane = lax.broadcasted_iota(I32, (tile, GROUP_WIDTH), 1)
    tail_lane = lax.broadcasted_iota(I32, (tile, V7X_LANES), 1)

    def to_class_order(bi, r):
        n, sr = chunk // r, seq // r

        def one_chunk(t, carry):
            src0 = pl.multiple_of(t * chunk, chunk)
            for src, dst in ((q_ref, qc), (k_ref, kc), (v_ref, vc)):
                y = _dot(perm_ref[bi - 1, 0], src[0, pl.ds(src0, chunk), :]).astype(BF16)
                for rho in range(r):
                    dst[pl.ds(pl.multiple_of(rho * sr + t * n, n), n), :] = y[rho * n:(rho + 1) * n, :]
            return carry

        lax.fori_loop(0, n_chunks, one_chunk, 0)

    def attend(bi, r, load):
        blocks_per_class = (seq // r) // tile

        def one_block(g, carry):
            cur = pl.multiple_of(g * tile, tile)
            prev = pl.multiple_of(jnp.maximum(g - 1, 0) * tile, tile)
            has_prev = (g % blocks_per_class) != 0
            q = load(0, cur)
            qs = jnp.concatenate([jnp.where(_head_mask(lane, h), q, jnp.zeros_like(q)) for h in range(N_HEADS)], axis=0)
            kk = jnp.concatenate([load(1, prev), load(1, cur)], axis=0)
            vv = jnp.concatenate([load(2, prev), load(2, cur)], axis=0)
            bias = jnp.where(has_prev, bias_ref[bi, 1], bias_ref[bi, 0])
            s = _dot_nt(qs, kk) + bias
            m = jnp.max(s, axis=1, keepdims=True)
            p = jnp.exp(s - m)
            l = jnp.sum(p, axis=1, keepdims=True)
            pv = _dot(p.astype(BF16), vv) * (1.0 / l)
            lse = m + jnp.log(l)
            out = jnp.zeros((tile, GROUP_WIDTH), F32)
            tail = jnp.zeros((tile, V7X_LANES), F32)
            for h in range(N_HEADS):
                rows = slice(h * tile, (h + 1) * tile)
                out = jnp.where(_head_mask(lane, h), pv[rows], out)
                tail = jnp.where(tail_lane == h, lse[rows], tail)
            res_o[bi, pl.ds(cur, tile), :] = out.astype(BF16)
            res_l[bi, pl.ds(cur, tile), :] = tail
            return carry

        lax.fori_loop(0, n_blocks, one_block, 0, unroll=8)

    refs = (q_ref, k_ref, v_ref)
    scratch = (qc, kc, vc)
    attend(0, 1, lambda which, start: refs[which][0, pl.ds(start, tile), :])
    for bi, r in enumerate(DIL_DILATIONS[1:], start=1):
        to_class_order(bi, r)
        attend(bi, r, lambda which, start: scratch[which][pl.ds(start, tile), :])

    def merge_chunk(t, carry):
        base = pl.multiple_of(t * chunk, chunk)
        def pack3(lse):
            hi, mid, lo = _split3(lse)
            return (hi.astype(F32) + pltpu.roll(mid.astype(F32), N_HEADS, 1)
                    + pltpu.roll(lo.astype(F32), 2 * N_HEADS, 1)).astype(BF16)

        outs, lses = [], []
        for bi, r in enumerate(DIL_DILATIONS):
            if r == 1:
                out = res_o[0, pl.ds(base, chunk), :].astype(F32)
                packed = pack3(res_l[0, pl.ds(base, chunk), :])
            else:
                n, sr = chunk // r, seq // r
                rows = [pl.ds(pl.multiple_of(rho * sr + t * n, n), n) for rho in range(r)]
                back = perm_ref[bi - 1, 1]
                out = _dot(back, jnp.concatenate([res_o[bi, rw, :] for rw in rows], axis=0))
                packed = _dot(back, pack3(jnp.concatenate([res_l[bi, rw, :] for rw in rows], axis=0))).astype(BF16)
            outs.append(out)
            lses.append(_dot(packed, spread_ref[...]))
        big = jnp.maximum(jnp.maximum(lses[0], lses[1]), lses[2])
        wts = [jnp.exp(z - big) for z in lses]
        o = (wts[0] * outs[0] + wts[1] * outs[1] + wts[2] * outs[2]) * (1.0 / (wts[0] + wts[1] + wts[2]))
        o_ref[0, pl.ds(base, chunk), :] = o.astype(o_ref.dtype)
        return carry

    lax.fori_loop(0, n_chunks, merge_chunk, 0)


def _dil_constants(slopes, seq):
    chunk, tile = DIL_CHUNK, DIL_TILE
    perms = []
    for r in DIL_DILATIONS[1:]:
        n = chunk // r
        dst = jnp.arange(chunk, dtype=I32)
        src = r * (dst % n) + dst // n
        fwd = (src[:, None] == jnp.arange(chunk, dtype=I32)[None, :]).astype(BF16)
        perms.append(jnp.stack([fwd, fwd.T]))
    m = jnp.arange(tile, dtype=I32)[:, None]
    j = jnp.arange(2 * tile, dtype=I32)[None, :]
    dist = jnp.where(j < tile, m - j + tile, m - (j - tile))
    ok_cur = (j >= tile) & (dist >= 0)
    ok_prev = (j < tile) & (dist <= DIL_WINDOW_KEYS)
    biases = []
    for r in DIL_DILATIONS:
        per_head = [-(slopes[h] * r) * dist.astype(F32) for h in range(N_HEADS)]
        with_prev = jnp.concatenate([jnp.where(ok_cur | ok_prev, b, NEG) for b in per_head], axis=0)
        no_prev = jnp.concatenate([jnp.where(ok_cur, b, NEG) for b in per_head], axis=0)
        biases.append(jnp.stack([no_prev, with_prev]))
    src_lane = jnp.arange(V7X_LANES, dtype=I32)[:, None]
    spread = ((src_lane < 3 * N_HEADS)
              & (src_lane % N_HEADS == jnp.arange(GROUP_WIDTH, dtype=I32)[None, :] // HEAD_DIM)).astype(BF16)
    return jnp.stack(perms), jnp.stack(biases), spread


def _dil_attention(ql, kl, vl, slopes):
    b, s, w = ql.shape
    perms, biases, spread = _dil_constants(slopes, s)
    row = pl.BlockSpec((1, s, w), lambda i: (i, 0, 0))
    const = lambda a: pl.BlockSpec(a.shape, lambda i: (0,) * a.ndim)
    return pl.pallas_call(
        functools.partial(_dil_kernel, seq=s),
        grid=(b,),
        in_specs=[row, row, row, const(perms), const(biases), const(spread)],
        out_specs=row,
        out_shape=jax.ShapeDtypeStruct((b, s, w), BF16),
        scratch_shapes=[pltpu.VMEM((s, w), BF16)] * 3
                       + [pltpu.VMEM((len(DIL_DILATIONS), s, w), BF16),
                          pltpu.VMEM((len(DIL_DILATIONS), s, V7X_LANES), F32)],
        compiler_params=_cparams("arbitrary"),
        name="dil_attn",
    )(ql, kl, vl, perms, biases, spread)


def _memattn_kernel(q_ref, mk_ref, mv_ref, o_ref):
    q = q_ref[0]
    tile = q.shape[0]
    lane = lax.broadcasted_iota(I32, q.shape, 1)
    qs = jnp.concatenate([jnp.where(_head_mask(lane, h), q, jnp.zeros_like(q)) for h in range(N_HEADS)], axis=0)
    s = _dot_nt(qs, mk_ref[0])
    m = jnp.max(s, axis=1, keepdims=True)
    p = jnp.exp(s - m)
    l = jnp.sum(p, axis=1, keepdims=True)
    pv = _dot(p.astype(BF16), mv_ref[0]) * (1.0 / l)
    out = jnp.zeros(q.shape, F32)
    for h in range(N_HEADS):
        out = jnp.where(_head_mask(lane, h), pv[h * tile:(h + 1) * tile], out)
    o_ref[0] = out.astype(o_ref.dtype)


def _mem_attention(mq, mk, mv):
    b, s, w = mq.shape
    m = mk.shape[1]
    tile = min(TOK_TILE, s)
    return pl.pallas_call(
        _memattn_kernel,
        grid=(b, s // tile),
        in_specs=[pl.BlockSpec((1, tile, w), lambda bb, ii: (bb, ii, 0)),
                  pl.BlockSpec((1, m, w), lambda bb, ii: (bb, 0, 0)),
                  pl.BlockSpec((1, m, w), lambda bb, ii: (bb, 0, 0))],
        out_specs=pl.BlockSpec((1, tile, w), lambda bb, ii: (bb, ii, 0)),
        out_shape=jax.ShapeDtypeStruct((b, s, w), BF16),
        compiler_params=_cparams("arbitrary", "arbitrary"),
        name="mem_attn",
    )(mq, mk, mv)


def _outrouter_kernel(x_ref, od_ref, ol_ref, om_ref, wo_ref, g_ref, wr_ref, br_ref,
                      x2_ref, h2_ref, route_ref, gate_ref, cnt_ref, run_ref, *, tm, n_exp):
    step = pl.program_id(0)

    @pl.when(step == 0)
    def _():
        run_ref[...] = jnp.zeros(run_ref.shape, F32)

    c1, c2 = DIFF_WIDTH, DIFF_WIDTH + GROUP_WIDTH
    mix = (_dot(od_ref[...], wo_ref[0:c1, :]) + _dot(ol_ref[...], wo_ref[c1:c2, :])
           + _dot(om_ref[...], wo_ref[c2:, :]))
    x2 = x_ref[...] + mix
    x2_ref[...] = x2
    h2 = _rms(x2, g_ref[...])
    h2_ref[...] = h2

    h_hi = h2.astype(BF16)
    h_lo = (h2 - h_hi.astype(F32)).astype(BF16)
    logits = (_dot(h_hi, wr_ref[0]) + (_dot(h_hi, wr_ref[1]) + _dot(h_lo, wr_ref[0]))) + br_ref[...]
    work = logits.T[:n_exp]
    erow = lax.broadcasted_iota(I32, work.shape, 0)
    erow_rev = (n_exp - 1 - erow).astype(F32)
    sel = jnp.zeros(work.shape, F32)
    vals, idxs = [], []
    for _ in range(TOP_K):
        mx = jnp.max(work, axis=0, keepdims=True)
        key = jnp.max(jnp.where(work == mx, erow_rev, -1.0), axis=0, keepdims=True)
        hit = erow_rev == key
        sel = jnp.where(hit, 1.0, sel)
        work = jnp.where(hit, -jnp.inf, work)
        vals.append(mx)
        idxs.append(n_exp - 1 - key.astype(I32))

    r_io = lax.broadcasted_iota(I32, (tm, tm), 0)
    c_io = lax.broadcasted_iota(I32, (tm, tm), 1)
    earlier = jnp.where(r_io < c_io, 1.0, 0.0).astype(BF16)
    before = _dot(sel.astype(BF16), earlier) + run_ref[...]
    run_ref[...] = run_ref[...] + jnp.sum(sel, axis=1, keepdims=True)
    cnt_ref[...] = jnp.broadcast_to(run_ref[...], cnt_ref.shape).astype(I32)

    exps = [jnp.exp(v - vals[0]) for v in vals]
    inv = 1.0 / (exps[0] + exps[1] + exps[2] + exps[3])
    krow = lax.broadcasted_iota(I32, (2 * TOP_K, tm), 0)
    route = jnp.zeros((2 * TOP_K, tm), I32)
    gate_t = jnp.zeros((V7X_LANES, tm), F32)
    grow = lax.broadcasted_iota(I32, gate_t.shape, 0)
    for k in range(TOP_K):
        rk = jnp.sum(jnp.where(erow == idxs[k], before, 0.0), axis=0, keepdims=True)
        route = jnp.where(krow == k, idxs[k], route)
        route = jnp.where(krow == TOP_K + k, rk.astype(I32), route)
        gate_t = jnp.where(grow == k, exps[k] * inv, gate_t)
    route_ref[...] = route
    gate_ref[...] = gate_t.T


def _out_router(x2d, o_diff, o_dil, o_mem, w_out, g, w_router_p, b_router_p, n_exp):
    t, d = x2d.shape
    tm = min(TOK_TILE, t)
    row = lambda n: pl.BlockSpec((tm, n), lambda i: (i, 0))
    full = lambda a: pl.BlockSpec(a.shape, lambda i: (0,) * a.ndim)
    return pl.pallas_call(
        functools.partial(_outrouter_kernel, tm=tm, n_exp=n_exp),
        grid=(t // tm,),
        in_specs=[row(d), row(DIFF_WIDTH), row(GROUP_WIDTH), row(GROUP_WIDTH),
                  full(w_out), full(g), full(w_router_p), full(b_router_p)],
        out_specs=[row(d), row(d), pl.BlockSpec((2 * TOP_K, tm), lambda i: (0, i)), row(V7X_LANES),
                   pl.BlockSpec((n_exp, V7X_LANES), lambda i: (0, 0))],
        out_shape=[jax.ShapeDtypeStruct((t, d), F32), jax.ShapeDtypeStruct((t, d), F32),
                   jax.ShapeDtypeStruct((2 * TOP_K, t), I32), jax.ShapeDtypeStruct((t, V7X_LANES), F32),
                   jax.ShapeDtypeStruct((n_exp, V7X_LANES), I32)],
        scratch_shapes=[pltpu.VMEM((n_exp, 1), F32)],
        compiler_params=_cparams("arbitrary"),
        name="out_router",
    )(x2d, o_diff, o_dil, o_mem, w_out, g, w_router_p, b_router_p)


def _row_copy(src, dst, sem):
    return pltpu.make_async_copy(src, dst, sem)


def _dispatch_kernel(dest_ref, h_ref, xs_hbm, sem, *, tm):
    def issue(t, carry):
        for k in range(TOP_K):
            d = dest_ref[0, 0, t * TOP_K + k]
            _row_copy(h_ref.at[pl.ds(t, 1), :], xs_hbm.at[pl.ds(d, 1), :], sem).start()
        return carry

    lax.fori_loop(0, tm, issue, 0, unroll=DMA_UNROLL)
    for _ in range(TOP_K):
        _row_copy(h_ref, xs_hbm.at[pl.ds(0, tm), :], sem).wait()


def _dispatch(h2, dest):
    t, d = h2.shape
    tm = min(DISPATCH_TILE, t)
    dest3 = dest.reshape(t // tm, 1, tm * TOP_K)
    return pl.pallas_call(
        functools.partial(_dispatch_kernel, tm=tm),
        grid=(t // tm,),
        in_specs=[pl.BlockSpec((1, 1, tm * TOP_K), lambda i: (i, 0, 0), memory_space=pltpu.SMEM),
                  pl.BlockSpec((tm, d), lambda i: (i, 0))],
        out_specs=pl.BlockSpec(memory_space=pl.ANY),
        out_shape=jax.ShapeDtypeStruct((t * TOP_K, d), h2.dtype),
        scratch_shapes=[pltpu.SemaphoreType.DMA],
        compiler_params=_cparams("arbitrary"),
        name="dispatch",
    )(dest3, h2)


def _split_gate_up_kernel(w_ref, sel_ref, wg_ref, wu_ref):
    x = w_ref[0].astype(BF16)
    k, n = sel_ref.shape[1], sel_ref.shape[2]
    for j in range(wg_ref.shape[2] // n):
        blk = x[:, j * k:(j + 1) * k]
        wg_ref[0, :, j * n:(j + 1) * n] = _dot(blk, sel_ref[0]).astype(BF16)
        wu_ref[0, :, j * n:(j + 1) * n] = _dot(blk, sel_ref[1]).astype(BF16)


def _split_gate_up(w):
    e, d, f2 = w.shape
    f = f2 // 2
    rows, n = min(TOK_TILE, d), V7X_MXU_DIM
    src = jnp.arange(2 * n, dtype=I32)[:, None]
    dst = jnp.arange(n, dtype=I32)[None, :]
    sel = jnp.stack([src == 2 * dst, src == 2 * dst + 1]).astype(BF16)
    out = jax.ShapeDtypeStruct((e, d, f), BF16)
    return pl.pallas_call(
        _split_gate_up_kernel,
        grid=(e, d // rows),
        in_specs=[pl.BlockSpec((1, rows, f2), lambda i, j: (i, j, 0)),
                  pl.BlockSpec(sel.shape, lambda i, j: (0, 0, 0))],
        out_specs=[pl.BlockSpec((1, rows, f), lambda i, j: (i, j, 0))] * 2,
        out_shape=[out, out],
        compiler_params=_cparams("arbitrary", "arbitrary"),
        name="split_gate_up",
    )(w, sel)


def _experts_kernel(tile_ref, exp_ref, lo_ref, hi_ref, act_ref, fresh_ref,
                    xs_ref, wg_ref, wu_ref, bg_ref, bu_ref, wd_ref, bd_ref, y_ref, wd_bf16, *, tm):
    w = pl.program_id(0)

    @pl.when(fresh_ref[w] == 1)
    def _():
        wd_bf16[...] = wd_ref[0].astype(BF16)

    @pl.when(act_ref[w] == 1)
    def _():
        x = xs_ref[...].astype(BF16)
        gate = jnp.minimum(_dot(x, wg_ref[0]) + bg_ref[0], SWIGLU_LIMIT)
        up = jnp.clip(_dot(x, wu_ref[0]) + bu_ref[0], -SWIGLU_LIMIT, SWIGLU_LIMIT)
        act = (up + 1.0) * (gate * jax.nn.sigmoid(gate * SWIGLU_ALPHA))
        y = _dot(act.astype(BF16), wd_bf16[...]) + bd_ref[0]
        row = lax.broadcasted_iota(I32, (tm, V7X_LANES), 0)
        mine = (row >= lo_ref[w]) & (row < hi_ref[w])
        for s in range(ROW_SUBLANES):
            y_ref[pl.ds(s, tm, stride=ROW_SUBLANES), :] = jnp.where(mine, y[:, s * V7X_LANES:(s + 1) * V7X_LANES], 0.0)

    @pl.when(act_ref[w] == 0)
    def _():
        y_ref[...] = jnp.zeros(y_ref.shape, F32)


def _experts(xs, meta, wg, wu, bg, bu, wd, bd):
    n, d = xs.shape
    e, _, f = wg.shape
    tm = min(ROW_TILE, n)
    n_items = meta[0].shape[0]
    xmap = lambda w, tile, ex, *_: (tile[w], 0)
    wmap = lambda w, tile, ex, *_: (ex[w], 0, 0)
    grid_spec = pltpu.PrefetchScalarGridSpec(
        num_scalar_prefetch=len(meta),
        grid=(n_items,),
        in_specs=[pl.BlockSpec((tm, d), xmap),
                  pl.BlockSpec((1, d, f), wmap), pl.BlockSpec((1, d, f), wmap),
                  pl.BlockSpec((1, 1, f), wmap), pl.BlockSpec((1, 1, f), wmap),
                  pl.BlockSpec((1, f, d), wmap), pl.BlockSpec((1, 1, d), wmap)],
        out_specs=pl.BlockSpec((tm * ROW_SUBLANES, V7X_LANES), lambda w, *_: (w, 0)),
        scratch_shapes=[pltpu.VMEM((f, d), BF16)],
    )
    assert d == ROW_SUBLANES * V7X_LANES
    return pl.pallas_call(
        functools.partial(_experts_kernel, tm=tm),
        grid_spec=grid_spec,
        out_shape=jax.ShapeDtypeStruct((n_items * tm * ROW_SUBLANES, V7X_LANES), F32),
        compiler_params=_cparams("arbitrary"),
        name="experts",
    )(*meta, xs, wg, wu, bg, bu, wd, bd)


def _expert_work_items(counts, n_rows, tm):
    e = counts.shape[0]
    n_tiles = n_rows // tm
    n_items = n_tiles + e - 1
    ends = jnp.cumsum(counts)
    starts = ends - counts
    first_tile = starts // tm
    n_t = jnp.where(counts > 0, (ends - 1) // tm - first_tile + 1, 0)
    item_end = jnp.cumsum(n_t)
    item_start = item_end - n_t
    total = item_end[-1]
    w = jnp.arange(n_items, dtype=I32)
    active = w < total
    w_eff = jnp.minimum(w, total - 1)
    ex = jnp.minimum(jnp.sum((item_end[None, :] <= w_eff[:, None]).astype(I32), axis=1), e - 1)
    mine = ex[:, None] == jnp.arange(e, dtype=I32)[None, :]
    pick = lambda v: jnp.sum(jnp.where(mine, v[None, :], 0), axis=1)
    tile = (pick(first_tile) + (w_eff - pick(item_start))).astype(I32)
    lo = jnp.clip(pick(starts) - tile * tm, 0, tm).astype(I32)
    hi = jnp.clip(pick(ends) - tile * tm, 0, tm).astype(I32)
    fresh = jnp.concatenate([jnp.ones((1,), I32), (ex[1:] != ex[:-1]).astype(I32)])
    return (tile, ex, lo, hi, active.astype(I32), fresh), first_tile, item_start


def _combine_kernel(dest_ref, dest_next_ref, x2_ref, gate_ref, g_ref, y_hbm, o_ref, ybuf, sems, *, tm, n_steps):
    i = pl.program_id(0)
    slot = i % 2

    def gather(dref, into):
        def issue(t, carry):
            for k in range(TOP_K):
                src = pl.multiple_of(dref[0, 0, t * TOP_K + k] * ROW_SUBLANES, ROW_SUBLANES)
                dst = pl.multiple_of(t * ROW_SUBLANES, ROW_SUBLANES)
                _row_copy(y_hbm.at[pl.ds(src, ROW_SUBLANES), :], ybuf.at[into, k, pl.ds(dst, ROW_SUBLANES), :],
                          sems.at[into]).start()
            return carry
        lax.fori_loop(0, tm, issue, 0, unroll=DMA_UNROLL)

    @pl.when(i == 0)
    def _():
        gather(dest_ref, 0)

    @pl.when(i + 1 < n_steps)
    def _():
        gather(dest_next_ref, 1 - slot)

    for k in range(TOP_K):
        _row_copy(y_hbm.at[pl.ds(0, tm * ROW_SUBLANES), :], ybuf.at[slot, k], sems.at[slot]).wait()

    gates = gate_ref[...]
    cols = []
    for s in range(ROW_SUBLANES):
        acc = x2_ref[:, s * V7X_LANES:(s + 1) * V7X_LANES]
        for k in range(TOP_K):
            acc = acc + ybuf[slot, k, pl.ds(s, tm, stride=ROW_SUBLANES), :] * gates[:, k:k + 1]
        cols.append(acc)
    o_ref[...] = _rms(jnp.concatenate(cols, axis=1), g_ref[...])


def _combine(y, dest, x2, gates, g_final):
    t, d = x2.shape
    tm = min(DMA_TILE, t)
    n_steps = t // tm
    dest3 = dest.reshape(n_steps, 1, tm * TOP_K)
    dest_spec = lambda f: pl.BlockSpec((1, 1, tm * TOP_K), f, memory_space=pltpu.SMEM)
    return pl.pallas_call(
        functools.partial(_combine_kernel, tm=tm, n_steps=n_steps),
        grid=(n_steps,),
        in_specs=[dest_spec(lambda i: (i, 0, 0)),
                  dest_spec(lambda i: (jnp.minimum(i + 1, n_steps - 1), 0, 0)),
                  pl.BlockSpec((tm, d), lambda i: (i, 0)),
                  pl.BlockSpec((tm, V7X_LANES), lambda i: (i, 0)),
                  pl.BlockSpec((1, d), lambda i: (0, 0)),
                  pl.BlockSpec(memory_space=pl.ANY)],
        out_specs=pl.BlockSpec((tm, d), lambda i: (i, 0)),
        out_shape=jax.ShapeDtypeStruct((t, d), F32),
        scratch_shapes=[pltpu.VMEM((2, TOP_K, tm * ROW_SUBLANES, V7X_LANES), F32), pltpu.SemaphoreType.DMA((2,))],
        compiler_params=_cparams("arbitrary"),
        name="combine",
    )(dest3, dest3, x2, gates, g_final, y)


def _alibi_slopes():
    n = 2 * N_HEADS
    s = 2.0 ** (-8.0 * (jnp.arange(n, dtype=F32) + 1.0) / n)
    return s[0::2], s[1::2]


def kernel(x, mem, attn_norm_g, mem_norm_g, w_in, diff_lambda_qk, diff_subln_g, w_mem_kv, w_out,
           ffn_norm_g, w_router, b_router, w_gate_up, b_gate_up, w_down, b_down, final_norm_g):
    b, s, d = x.shape
    t = b * s
    assert w_in.shape[0] == 1, "the final norm is fused into the layer's combine step: one layer only"
    assert s % (DIL_DILATIONS[-1] * DIL_TILE) == 0 and t % TOK_TILE == 0
    n_exp = w_router.shape[-1]
    diff_slopes, dil_slopes = _alibi_slopes()
    x2d = x.reshape(t, d)

    for l in range(1):
        lam_init = 0.8 - 0.6 * math.exp(-0.3 * l)
        mk, mv = _mem_kv(mem, mem_norm_g[l][None], w_mem_kv[l].astype(BF16))
        qd, kd, vd, ql, kl, vl, mq = _in_proj(x2d, attn_norm_g[l][None], w_in[l])
        sh = lambda a: a.reshape(b, s, a.shape[-1])
        o_diff = _diff_attention(sh(qd), sh(kd), sh(vd), diff_slopes, diff_lambda_qk[l],
                                 diff_subln_g[l][None], lam_init)
        o_dil = _dil_attention(sh(ql), sh(kl), sh(vl), dil_slopes)
        o_mem = _mem_attention(sh(mq), mk, mv)

        w_router_f = jnp.zeros((d, V7X_LANES), F32).at[:, :n_exp].set(w_router[l])
        w_router_hi = w_router_f.astype(BF16)
        w_router_p = jnp.stack([w_router_hi, (w_router_f - w_router_hi.astype(F32)).astype(BF16)])
        b_router_p = jnp.full((1, V7X_LANES), NEG, F32).at[0, :n_exp].set(b_router[l])
        x2, h2, route, gates, cnt = _out_router(
            x2d, o_diff.reshape(t, DIFF_WIDTH), o_dil.reshape(t, GROUP_WIDTH), o_mem.reshape(t, GROUP_WIDTH),
            w_out[l].astype(BF16), ffn_norm_g[l][None], w_router_p, b_router_p, n_exp)

        counts = cnt[:, 0]
        starts = jnp.cumsum(counts) - counts
        onehot = route[:TOP_K, :, None] == jnp.arange(n_exp, dtype=I32)[None, None, :]
        pick = lambda v: jnp.sum(jnp.where(onehot, v[None, None, :], 0), axis=-1)
        dest_t = (route[TOP_K:] + pick(starts)).astype(I32)
        dest = dest_t.T

        xs = _dispatch(h2, dest)

        f = w_down.shape[2]
        tm_rows = min(ROW_TILE, t * TOP_K)
        w_gate, w_up = _split_gate_up(w_gate_up[l])
        bgu = b_gate_up[l].reshape(n_exp, 1, f, 2)
        meta, first_tile, item_start = _expert_work_items(counts, t * TOP_K, tm_rows)
        y = _experts(xs, meta, w_gate, w_up, bgu[..., 0], bgu[..., 1], w_down[l], b_down[l][:, None, :])

        item = pick(item_start) + dest_t // tm_rows - pick(first_tile)
        y_row = (item * tm_rows + dest_t % tm_rows).astype(I32).T
        out = _combine(y, y_row, x2, gates, final_norm_g[None])
    return out.reshape(b, s, d)
```

```python
import functools
import math

import jax
import jax.numpy as jnp
from jax import lax
from jax.experimental import pallas as pl
from jax.experimental.pallas import tpu as pltpu

F32 = jnp.float32
BF16 = jnp.bfloat16
I32 = jnp.int32

NORM_EPS = 1e-5
HEAD_DIM = 64
N_HEADS = 4
DIFF_VDIM = 2 * HEAD_DIM
DIFF_WIDTH = N_HEADS * DIFF_VDIM
GROUP_WIDTH = N_HEADS * HEAD_DIM
DIL_DILATIONS = (1, 4, 16)
DIL_WINDOW_KEYS = 128
TOP_K = 4
SWIGLU_ALPHA = 1.702
SWIGLU_LIMIT = 7.0
NEG = -1e30

V7X_LANES = 128
V7X_SUBLANES = 8
V7X_MXU_DIM = 256
V7X_VMEM_LIMIT = 56 * 1024 * 1024

TOK_TILE = 512
ATT_TILE = 512
POS_SPLIT = 64
DIFF_HEADS_PER_STEP = 1
DIL_TILE = DIL_WINDOW_KEYS
DIL_CHUNK = 256
ROW_TILE = 512
ROW_SUBLANES = V7X_SUBLANES
DMA_TILE = 256
DISPATCH_TILE = 1024
DMA_UNROLL = 4


def _cparams(*sem):
    return pltpu.CompilerParams(dimension_semantics=tuple(sem), vmem_limit_bytes=V7X_VMEM_LIMIT)


def _rms(x, g):
    return x * lax.rsqrt(jnp.mean(x * x, axis=-1, keepdims=True) + NORM_EPS) * g


def _dot_nt(a, b):
    return lax.dot_general(a, b, (((1,), (1,)), ((), ())), preferred_element_type=F32)


def _dot(a, b):
    return jnp.dot(a, b, preferred_element_type=F32)


def _memkv_kernel(mem_ref, g_ref, w_ref, mk_ref, mv_ref):
    h = _rms(mem_ref[0], g_ref[...]).astype(BF16)
    kv = _dot(h, w_ref[...])
    mk_ref[0] = kv[:, :GROUP_WIDTH].astype(BF16)
    mv_ref[0] = kv[:, GROUP_WIDTH:].astype(BF16)


def _mem_kv(mem, g, w):
    b, m, d = mem.shape
    out = jax.ShapeDtypeStruct((b, m, GROUP_WIDTH), BF16)
    return pl.pallas_call(
        _memkv_kernel,
        grid=(b,),
        in_specs=[pl.BlockSpec((1, m, d), lambda i: (i, 0, 0)),
                  pl.BlockSpec((1, d), lambda i: (0, 0)),
                  pl.BlockSpec((d, 2 * GROUP_WIDTH), lambda i: (0, 0))],
        out_specs=[pl.BlockSpec((1, m, GROUP_WIDTH), lambda i: (i, 0, 0))] * 2,
        out_shape=[out, out],
        compiler_params=_cparams("arbitrary"),
        name="mem_kv",
    )(mem, g, w)


_IN_SEGMENTS = ((DIFF_WIDTH, HEAD_DIM ** -0.5), (DIFF_WIDTH, 1.0), (DIFF_WIDTH, 1.0),
                (GROUP_WIDTH, HEAD_DIM ** -0.5), (GROUP_WIDTH, 1.0), (GROUP_WIDTH, 1.0),
                (GROUP_WIDTH, HEAD_DIM ** -0.5))


def _inproj_kernel(x_ref, g_ref, w_ref, *refs):
    out_refs, w_bf16 = refs[:-1], refs[-1]

    @pl.when(pl.program_id(0) == 0)
    def _():
        w_bf16[...] = w_ref[...].astype(BF16)

    h = _rms(x_ref[...], g_ref[...]).astype(BF16)
    c0 = 0
    for ref, (n, scale) in zip(out_refs, _IN_SEGMENTS):
        p = _dot(h, w_bf16[:, c0:c0 + n])
        if scale != 1.0:
            p = p * scale
        ref[...] = p.astype(BF16)
        c0 += n


def _in_proj(x2d, g, w):
    t, d = x2d.shape
    tm = min(TOK_TILE, t)
    return pl.pallas_call(
        _inproj_kernel,
        grid=(t // tm,),
        in_specs=[pl.BlockSpec((tm, d), lambda i: (i, 0)),
                  pl.BlockSpec((1, d), lambda i: (0, 0)),
                  pl.BlockSpec(w.shape, lambda i: (0, 0))],
        out_specs=[pl.BlockSpec((tm, n), lambda i: (i, 0)) for n, _ in _IN_SEGMENTS],
        out_shape=[jax.ShapeDtypeStruct((t, n), BF16) for n, _ in _IN_SEGMENTS],
        scratch_shapes=[pltpu.VMEM(w.shape, BF16)],
        compiler_params=_cparams("arbitrary"),
        name="in_proj",
    )(x2d, g, w)


def _diff_kernel(slopes_ref, lam_ref, g_ref, q_ref, k_ref, v_ref, kpos_ref, ones_ref, o_ref,
                 m_ref, a_ref, s_ref, *, lam_init, tile):
    n_full = 2 * pl.program_id(2)
    lane = lax.broadcasted_iota(I32, (tile, DIFF_VDIM), 1)
    heads = range(DIFF_HEADS_PER_STEP)
    qs = []
    for hd in heads:
        slope = slopes_ref[DIFF_HEADS_PER_STEP * pl.program_id(1) + hd]
        qpos = jnp.where(lane == 0, slope * float(POS_SPLIT), jnp.where(lane == 1, slope, 0.0)).astype(BF16)
        for sub in range(2):
            q = q_ref[0, sub * tile:(sub + 1) * tile, hd * DIFF_VDIM:(hd + 1) * DIFF_VDIM]
            zero = jnp.zeros_like(q)
            qs.append(jnp.concatenate([jnp.where(lane < HEAD_DIM, q, zero), qpos], axis=1))
            qs.append(jnp.concatenate([jnp.where(lane >= HEAD_DIM, q, zero), qpos], axis=1))
    chains = lambda subs: [(hd, 4 * hd + 2 * sub + mp) for hd in heads for sub in subs for mp in range(2)]
    sub_a, sub_b, both = chains((0,)), chains((1,)), chains((0, 1))

    m_ref[...] = jnp.full(m_ref.shape, NEG, F32)
    a_ref[...] = jnp.zeros(a_ref.shape, F32)

    def scores(j, slot, which):
        start = pl.multiple_of(j * tile, tile)
        kk = [jnp.concatenate([k_ref[0, pl.ds(start, tile), hd * DIFF_VDIM:(hd + 1) * DIFF_VDIM],
                               kpos_ref[pl.ds(start, tile), :]], axis=1) for hd in heads]
        for hd, c in which:
            s_ref[slot, c] = _dot_nt(qs[c], kk[hd])

    def consume(j, slot, which, diagonal):
        start = pl.multiple_of(j * tile, tile)
        vv = [jnp.concatenate([v_ref[0, pl.ds(start, tile), hd * DIFF_VDIM:(hd + 1) * DIFF_VDIM], ones_ref[...]],
                              axis=1) for hd in heads]
        for hd, c in which:
            s = s_ref[slot, c]
            if diagonal:
                row = lax.broadcasted_iota(I32, s.shape, 0)
                col = lax.broadcasted_iota(I32, s.shape, 1)
                s = jnp.where(col <= row, s, NEG)
            m_old = m_ref[c]
            m_new = jnp.maximum(m_old, jnp.max(s, axis=1, keepdims=True))
            p = jnp.exp(s - m_new)
            a_ref[c] = jnp.exp(m_old - m_new) * a_ref[c] + _dot(p.astype(BF16), vv[hd])
            m_ref[c] = m_new

    scores(0, 0, both)

    def pair(p, carry):
        j = 2 * p
        scores(j + 1, 1, both)
        consume(j, 0, both, False)
        scores(j + 2, 0, both)
        consume(j + 1, 1, both, False)
        return carry

    lax.fori_loop(0, n_full // 2, pair, 0)
    scores(n_full + 1, 1, sub_b)
    consume(n_full, 0, sub_a, True)
    consume(n_full, 0, sub_b, False)
    consume(n_full + 1, 1, sub_b, True)

    lv = lam_ref[...]
    lam = (jnp.exp(jnp.sum(lv[0:1] * lv[1:2], axis=1, keepdims=True))
           - jnp.exp(jnp.sum(lv[2:3] * lv[3:4], axis=1, keepdims=True)) + lam_init)
    n = DIFF_VDIM
    for hd in heads:
        for sub in range(2):
            a1, a2 = a_ref[4 * hd + 2 * sub], a_ref[4 * hd + 2 * sub + 1]
            o = (a1[:, :n] * (1.0 / a1[:, n:n + 1])) - lam * (a2[:, :n] * (1.0 / a2[:, n:n + 1]))
            o = _rms(o, g_ref[...]) * (1.0 - lam_init)
            o_ref[0, sub * tile:(sub + 1) * tile, hd * n:(hd + 1) * n] = o.astype(o_ref.dtype)


def _diff_attention(qd, kd, vd, slopes, lam_qk, subln_g, lam_init):
    b, s, _ = qd.shape
    tile = ATT_TILE
    assert s % (2 * tile) == 0
    pos = jnp.arange(s, dtype=I32)[:, None]
    lane = jnp.arange(V7X_LANES, dtype=I32)[None, :]
    kpos = jnp.where(lane == 0, pos // POS_SPLIT, jnp.where(lane == 1, pos % POS_SPLIT, 0)).astype(BF16)
    ones = jnp.broadcast_to(lane == 0, (tile, V7X_LANES)).astype(BF16)
    vec = lambda shape: pltpu.VMEM(shape, F32)
    const = lambda a: pl.BlockSpec(a.shape, lambda bb, hh, ii: (0, 0))
    width = DIFF_HEADS_PER_STEP * DIFF_VDIM
    n_chains = 4 * DIFF_HEADS_PER_STEP
    return pl.pallas_call(
        functools.partial(_diff_kernel, lam_init=lam_init, tile=tile),
        grid=(b, N_HEADS // DIFF_HEADS_PER_STEP, s // (2 * tile)),
        in_specs=[pl.BlockSpec(memory_space=pltpu.SMEM), const(lam_qk), const(subln_g),
                  pl.BlockSpec((1, 2 * tile, width), lambda bb, hh, ii: (bb, ii, hh)),
                  pl.BlockSpec((1, s, width), lambda bb, hh, ii: (bb, 0, hh)),
                  pl.BlockSpec((1, s, width), lambda bb, hh, ii: (bb, 0, hh)),
                  const(kpos), const(ones)],
        out_specs=pl.BlockSpec((1, 2 * tile, width), lambda bb, hh, ii: (bb, ii, hh)),
        out_shape=jax.ShapeDtypeStruct((b, s, DIFF_WIDTH), BF16),
        scratch_shapes=[vec((n_chains, tile, 1)), vec((n_chains, tile, 2 * DIFF_VDIM)),
                        vec((2, n_chains, tile, tile))],
        compiler_params=_cparams("arbitrary", "arbitrary", "arbitrary"),
        name="diff_attn",
    )(slopes, lam_qk, subln_g, qd, kd, vd, kpos, ones)


def _head_mask(lane, h):
    return (lane >= h * HEAD_DIM) & (lane < (h + 1) * HEAD_DIM)


def _split3(x):
    hi = x.astype(BF16)
    rem = x - hi.astype(F32)
    mid = rem.astype(BF16)
    return hi, mid, (rem - mid.astype(F32)).astype(BF16)


def _dil_kernel(q_ref, k_ref, v_ref, perm_ref, bias_ref, spread_ref, o_ref, qc, kc, vc, res_o, res_l, *, seq):
    tile, chunk = DIL_TILE, DIL_CHUNK
    n_blocks, n_chunks = seq // tile, seq // chunk
    lane = lax.broadcasted_iota(I32, (tile, GROUP_WIDTH), 1)
    tail_lane = lax.broadcasted_iota(I32, (tile, V7X_LANES), 1)

    def to_class_order(bi, r):
        n, sr = chunk // r, seq // r

        def one_chunk(t, carry):
            src0 = pl.multiple_of(t * chunk, chunk)
            for src, dst in ((q_ref, qc), (k_ref, kc), (v_ref, vc)):
                y = _dot(perm_ref[bi - 1, 0], src[0, pl.ds(src0, chunk), :]).astype(BF16)
                for rho in range(r):
                    dst[pl.ds(pl.multiple_of(rho * sr + t * n, n), n), :] = y[rho * n:(rho + 1) * n, :]
            return carry

        lax.fori_loop(0, n_chunks, one_chunk, 0)

    def attend(bi, r, load):
        blocks_per_class = (seq // r) // tile

        def one_block(g, carry):
            cur = pl.multiple_of(g * tile, tile)
            prev = pl.multiple_of(jnp.maximum(g - 1, 0) * tile, tile)
            has_prev = (g % blocks_per_class) != 0
            q = load(0, cur)
            qs = jnp.concatenate([jnp.where(_head_mask(lane, h), q, jnp.zeros_like(q)) for h in range(N_HEADS)], axis=0)
            kk = jnp.concatenate([load(1, prev), load(1, cur)], axis=0)
            vv = jnp.concatenate([load(2, prev), load(2, cur)], axis=0)
            bias = jnp.where(has_prev, bias_ref[bi, 1], bias_ref[bi, 0])
            s = _dot_nt(qs, kk) + bias
            m = jnp.max(s, axis=1, keepdims=True)
            p = jnp.exp(s - m)
            l = jnp.sum(p, axis=1, keepdims=True)
            pv = _dot(p.astype(BF16), vv) * (1.0 / l)
            lse = m + jnp.log(l)
            out = jnp.zeros((tile, GROUP_WIDTH), F32)
            tail = jnp.zeros((tile, V7X_LANES), F32)
            for h in range(N_HEADS):
                rows = slice(h * tile, (h + 1) * tile)
                out = jnp.where(_head_mask(lane, h), pv[rows], out)
                tail = jnp.where(tail_lane == h, lse[rows], tail)
            res_o[bi, pl.ds(cur, tile), :] = out.astype(BF16)
            res_l[bi, pl.ds(cur, tile), :] = tail
            return carry

        lax.fori_loop(0, n_blocks, one_block, 0, unroll=8)

    refs = (q_ref, k_ref, v_ref)
    scratch = (qc, kc, vc)
    attend(0, 1, lambda which, start: refs[which][0, pl.ds(start, tile), :])
    for bi, r in enumerate(DIL_DILATIONS[1:], start=1):
        to_class_order(bi, r)
        attend(bi, r, lambda which, start: scratch[which][pl.ds(start, tile), :])

    def merge_chunk(t, carry):
        base = pl.multiple_of(t * chunk, chunk)
        def pack3(lse):
            hi, mid, lo = _split3(lse)
            return (hi.astype(F32) + pltpu.roll(mid.astype(F32), N_HEADS, 1)
                    + pltpu.roll(lo.astype(F32), 2 * N_HEADS, 1)).astype(BF16)

        outs, lses = [], []
        for bi, r in enumerate(DIL_DILATIONS):
            if r == 1:
                out = res_o[0, pl.ds(base, chunk), :].astype(F32)
                packed = pack3(res_l[0, pl.ds(base, chunk), :])
            else:
                n, sr = chunk // r, seq // r
                rows = [pl.ds(pl.multiple_of(rho * sr + t * n, n), n) for rho in range(r)]
                back = perm_ref[bi - 1, 1]
                out = _dot(back, jnp.concatenate([res_o[bi, rw, :] for rw in rows], axis=0))
                packed = _dot(back, pack3(jnp.concatenate([res_l[bi, rw, :] for rw in rows], axis=0))).astype(BF16)
            outs.append(out)
            lses.append(_dot(packed, spread_ref[...]))
        big = jnp.maximum(jnp.maximum(lses[0], lses[1]), lses[2])
        wts = [jnp.exp(z - big) for z in lses]
        o = (wts[0] * outs[0] + wts[1] * outs[1] + wts[2] * outs[2]) * (1.0 / (wts[0] + wts[1] + wts[2]))
        o_ref[0, pl.ds(base, chunk), :] = o.astype(o_ref.dtype)
        return carry

    lax.fori_loop(0, n_chunks, merge_chunk, 0)


def _dil_constants(slopes, seq):
    chunk, tile = DIL_CHUNK, DIL_TILE
    perms = []
    for r in DIL_DILATIONS[1:]:
        n = chunk // r
        dst = jnp.arange(chunk, dtype=I32)
        src = r * (dst % n) + dst // n
        fwd = (src[:, None] == jnp.arange(chunk, dtype=I32)[None, :]).astype(BF16)
        perms.append(jnp.stack([fwd, fwd.T]))
    m = jnp.arange(tile, dtype=I32)[:, None]
    j = jnp.arange(2 * tile, dtype=I32)[None, :]
    dist = jnp.where(j < tile, m - j + tile, m - (j - tile))
    ok_cur = (j >= tile) & (dist >= 0)
    ok_prev = (j < tile) & (dist <= DIL_WINDOW_KEYS)
    biases = []
    for r in DIL_DILATIONS:
        per_head = [-(slopes[h] * r) * dist.astype(F32) for h in range(N_HEADS)]
        with_prev = jnp.concatenate([jnp.where(ok_cur | ok_prev, b, NEG) for b in per_head], axis=0)
        no_prev = jnp.concatenate([jnp.where(ok_cur, b, NEG) for b in per_head], axis=0)
        biases.append(jnp.stack([no_prev, with_prev]))
    src_lane = jnp.arange(V7X_LANES, dtype=I32)[:, None]
    spread = ((src_lane < 3 * N_HEADS)
              & (src_lane % N_HEADS == jnp.arange(GROUP_WIDTH, dtype=I32)[None, :] // HEAD_DIM)).astype(BF16)
    return jnp.stack(perms), jnp.stack(biases), spread


def _dil_attention(ql, kl, vl, slopes):
    b, s, w = ql.shape
    perms, biases, spread = _dil_constants(slopes, s)
    row = pl.BlockSpec((1, s, w), lambda i: (i, 0, 0))
    const = lambda a: pl.BlockSpec(a.shape, lambda i: (0,) * a.ndim)
    return pl.pallas_call(
        functools.partial(_dil_kernel, seq=s),
        grid=(b,),
        in_specs=[row, row, row, const(perms), const(biases), const(spread)],
        out_specs=row,
        out_shape=jax.ShapeDtypeStruct((b, s, w), BF16),
        scratch_shapes=[pltpu.VMEM((s, w), BF16)] * 3
                       + [pltpu.VMEM((len(DIL_DILATIONS), s, w), BF16),
                          pltpu.VMEM((len(DIL_DILATIONS), s, V7X_LANES), F32)],
        compiler_params=_cparams("arbitrary"),
        name="dil_attn",
    )(ql, kl, vl, perms, biases, spread)


def _memattn_kernel(q_ref, mk_ref, mv_ref, o_ref):
    q = q_ref[0]
    tile = q.shape[0]
    lane = lax.broadcasted_iota(I32, q.shape, 1)
    qs = jnp.concatenate([jnp.where(_head_mask(lane, h), q, jnp.zeros_like(q)) for h in range(N_HEADS)], axis=0)
    s = _dot_nt(qs, mk_ref[0])
    m = jnp.max(s, axis=1, keepdims=True)
    p = jnp.exp(s - m)
    l = jnp.sum(p, axis=1, keepdims=True)
    pv = _dot(p.astype(BF16), mv_ref[0]) * (1.0 / l)
    out = jnp.zeros(q.shape, F32)
    for h in range(N_HEADS):
        out = jnp.where(_head_mask(lane, h), pv[h * tile:(h + 1) * tile], out)
    o_ref[0] = out.astype(o_ref.dtype)


def _mem_attention(mq, mk, mv):
    b, s, w = mq.shape
    m = mk.shape[1]
    tile = min(TOK_TILE, s)
    return pl.pallas_call(
        _memattn_kernel,
        grid=(b, s // tile),
        in_specs=[pl.BlockSpec((1, tile, w), lambda bb, ii: (bb, ii, 0)),
                  pl.BlockSpec((1, m, w), lambda bb, ii: (bb, 0, 0)),
                  pl.BlockSpec((1, m, w), lambda bb, ii: (bb, 0, 0))],
        out_specs=pl.BlockSpec((1, tile, w), lambda bb, ii: (bb, ii, 0)),
        out_shape=jax.ShapeDtypeStruct((b, s, w), BF16),
        compiler_params=_cparams("arbitrary", "arbitrary"),
        name="mem_attn",
    )(mq, mk, mv)


def _outrouter_kernel(x_ref, od_ref, ol_ref, om_ref, wo_ref, g_ref, wr_ref, br_ref,
                      x2_ref, h2_ref, route_ref, gate_ref, cnt_ref, run_ref, *, tm, n_exp):
    step = pl.program_id(0)

    @pl.when(step == 0)
    def _():
        run_ref[...] = jnp.zeros(run_ref.shape, F32)

    c1, c2 = DIFF_WIDTH, DIFF_WIDTH + GROUP_WIDTH
    mix = (_dot(od_ref[...], wo_ref[0:c1, :]) + _dot(ol_ref[...], wo_ref[c1:c2, :])
           + _dot(om_ref[...], wo_ref[c2:, :]))
    x2 = x_ref[...] + mix
    x2_ref[...] = x2
    h2 = _rms(x2, g_ref[...])
    h2_ref[...] = h2

    h_hi = h2.astype(BF16)
    h_lo = (h2 - h_hi.astype(F32)).astype(BF16)
    logits = (_dot(h_hi, wr_ref[0]) + (_dot(h_hi, wr_ref[1]) + _dot(h_lo, wr_ref[0]))) + br_ref[...]
    work = logits.T[:n_exp]
    erow = lax.broadcasted_iota(I32, work.shape, 0)
    erow_rev = (n_exp - 1 - erow).astype(F32)
    sel = jnp.zeros(work.shape, F32)
    vals, idxs = [], []
    for _ in range(TOP_K):
        mx = jnp.max(work, axis=0, keepdims=True)
        key = jnp.max(jnp.where(work == mx, erow_rev, -1.0), axis=0, keepdims=True)
        hit = erow_rev == key
        sel = jnp.where(hit, 1.0, sel)
        work = jnp.where(hit, -jnp.inf, work)
        vals.append(mx)
        idxs.append(n_exp - 1 - key.astype(I32))

    r_io = lax.broadcasted_iota(I32, (tm, tm), 0)
    c_io = lax.broadcasted_iota(I32, (tm, tm), 1)
    earlier = jnp.where(r_io < c_io, 1.0, 0.0).astype(BF16)
    before = _dot(sel.astype(BF16), earlier) + run_ref[...]
    run_ref[...] = run_ref[...] + jnp.sum(sel, axis=1, keepdims=True)
    cnt_ref[...] = jnp.broadcast_to(run_ref[...], cnt_ref.shape).astype(I32)

    exps = [jnp.exp(v - vals[0]) for v in vals]
    inv = 1.0 / (exps[0] + exps[1] + exps[2] + exps[3])
    krow = lax.broadcasted_iota(I32, (2 * TOP_K, tm), 0)
    route = jnp.zeros((2 * TOP_K, tm), I32)
    gate_t = jnp.zeros((V7X_LANES, tm), F32)
    grow = lax.broadcasted_iota(I32, gate_t.shape, 0)
    for k in range(TOP_K):
        rk = jnp.sum(jnp.where(erow == idxs[k], before, 0.0), axis=0, keepdims=True)
        route = jnp.where(krow == k, idxs[k], route)
        route = jnp.where(krow == TOP_K + k, rk.astype(I32), route)
        gate_t = jnp.where(grow == k, exps[k] * inv, gate_t)
    route_ref[...] = route
    gate_ref[...] = gate_t.T


def _out_router(x2d, o_diff, o_dil, o_mem, w_out, g, w_router_p, b_router_p, n_exp):
    t, d = x2d.shape
    tm = min(TOK_TILE, t)
    row = lambda n: pl.BlockSpec((tm, n), lambda i: (i, 0))
    full = lambda a: pl.BlockSpec(a.shape, lambda i: (0,) * a.ndim)
    return pl.pallas_call(
        functools.partial(_outrouter_kernel, tm=tm, n_exp=n_exp),
        grid=(t // tm,),
        in_specs=[row(d), row(DIFF_WIDTH), row(GROUP_WIDTH), row(GROUP_WIDTH),
                  full(w_out), full(g), full(w_router_p), full(b_router_p)],
        out_specs=[row(d), row(d), pl.BlockSpec((2 * TOP_K, tm), lambda i: (0, i)), row(V7X_LANES),
                   pl.BlockSpec((n_exp, V7X_LANES), lambda i: (0, 0))],
        out_shape=[jax.ShapeDtypeStruct((t, d), F32), jax.ShapeDtypeStruct((t, d), F32),
                   jax.ShapeDtypeStruct((2 * TOP_K, t), I32), jax.ShapeDtypeStruct((t, V7X_LANES), F32),
                   jax.ShapeDtypeStruct((n_exp, V7X_LANES), I32)],
        scratch_shapes=[pltpu.VMEM((n_exp, 1), F32)],
        compiler_params=_cparams("arbitrary"),
        name="out_router",
    )(x2d, o_diff, o_dil, o_mem, w_out, g, w_router_p, b_router_p)


def _row_copy(src, dst, sem):
    return pltpu.make_async_copy(src, dst, sem)


def _dispatch_kernel(dest_ref, h_ref, xs_hbm, sem, *, tm):
    def issue(t, carry):
        for k in range(TOP_K):
            d = dest_ref[k, t]
            _row_copy(h_ref.at[pl.ds(t, 1), :], xs_hbm.at[pl.ds(d, 1), :], sem).start()
        return carry

    lax.fori_loop(0, tm, issue, 0, unroll=DMA_UNROLL)
    for _ in range(TOP_K):
        _row_copy(h_ref, xs_hbm.at[pl.ds(0, tm), :], sem).wait()


def _dispatch(h2, dest):
    t, d = h2.shape
    tm = min(DISPATCH_TILE, t)
    return pl.pallas_call(
        functools.partial(_dispatch_kernel, tm=tm),
        grid=(t // tm,),
        in_specs=[pl.BlockSpec((TOP_K, tm), lambda i: (0, i), memory_space=pltpu.SMEM),
                  pl.BlockSpec((tm, d), lambda i: (i, 0))],
        out_specs=pl.BlockSpec(memory_space=pl.ANY),
        out_shape=jax.ShapeDtypeStruct((t * TOP_K, d), h2.dtype),
        scratch_shapes=[pltpu.SemaphoreType.DMA],
        compiler_params=_cparams("arbitrary"),
        name="dispatch",
    )(dest, h2)


def _split_gate_up_kernel(w_ref, sel_ref, wg_ref, wu_ref):
    x = w_ref[0].astype(BF16)
    k, n = sel_ref.shape[1], sel_ref.shape[2]
    for j in range(wg_ref.shape[2] // n):
        blk = x[:, j * k:(j + 1) * k]
        wg_ref[0, :, j * n:(j + 1) * n] = _dot(blk, sel_ref[0]).astype(BF16)
        wu_ref[0, :, j * n:(j + 1) * n] = _dot(blk, sel_ref[1]).astype(BF16)


def _split_gate_up(w):
    e, d, f2 = w.shape
    f = f2 // 2
    rows, n = min(TOK_TILE, d), V7X_MXU_DIM
    src = jnp.arange(2 * n, dtype=I32)[:, None]
    dst = jnp.arange(n, dtype=I32)[None, :]
    sel = jnp.stack([src == 2 * dst, src == 2 * dst + 1]).astype(BF16)
    out = jax.ShapeDtypeStruct((e, d, f), BF16)
    return pl.pallas_call(
        _split_gate_up_kernel,
        grid=(e, d // rows),
        in_specs=[pl.BlockSpec((1, rows, f2), lambda i, j: (i, j, 0)),
                  pl.BlockSpec(sel.shape, lambda i, j: (0, 0, 0))],
        out_specs=[pl.BlockSpec((1, rows, f), lambda i, j: (i, j, 0))] * 2,
        out_shape=[out, out],
        compiler_params=_cparams("arbitrary", "arbitrary"),
        name="split_gate_up",
    )(w, sel)


def _experts_kernel(tile_ref, exp_ref, lo_ref, hi_ref, act_ref, fresh_ref,
                    xs_ref, wg_ref, wu_ref, bg_ref, bu_ref, wd_ref, bd_ref, y_ref, wd_bf16, *, tm):
    w = pl.program_id(0)

    @pl.when(fresh_ref[w] == 1)
    def _():
        wd_bf16[...] = wd_ref[0].astype(BF16)

    lo, hi, half = lo_ref[w], hi_ref[w], tm // 2
    active = act_ref[w] == 1
    top_only = active & (hi <= half)
    bottom_only = active & (lo >= half)

    def mlp(r0, nr):
        x = xs_ref[r0:r0 + nr, :].astype(BF16)
        gate = jnp.minimum(_dot(x, wg_ref[0]) + bg_ref[0], SWIGLU_LIMIT)
        up = jnp.clip(_dot(x, wu_ref[0]) + bu_ref[0], -SWIGLU_LIMIT, SWIGLU_LIMIT)
        act = (up + 1.0) * (gate * jax.nn.sigmoid(gate * SWIGLU_ALPHA))
        y = _dot(act.astype(BF16), wd_bf16[...]) + bd_ref[0]
        row = r0 + lax.broadcasted_iota(I32, (nr, V7X_LANES), 0)
        mine = (row >= lo) & (row < hi)
        for s in range(ROW_SUBLANES):
            y_ref[pl.ds(r0 * ROW_SUBLANES + s, nr, stride=ROW_SUBLANES), :] = jnp.where(
                mine, y[:, s * V7X_LANES:(s + 1) * V7X_LANES], 0.0)

    def zero(r0, nr):
        y_ref[r0 * ROW_SUBLANES:(r0 + nr) * ROW_SUBLANES, :] = jnp.zeros((nr * ROW_SUBLANES, V7X_LANES), F32)

    @pl.when(active & jnp.logical_not(top_only | bottom_only))
    def _():
        mlp(0, tm)

    @pl.when(top_only)
    def _():
        mlp(0, half)
        zero(half, half)

    @pl.when(bottom_only)
    def _():
        zero(0, half)
        mlp(half, half)

    @pl.when(jnp.logical_not(active))
    def _():
        zero(0, tm)


def _experts(xs, meta, wg, wu, bg, bu, wd, bd):
    n, d = xs.shape
    e, _, f = wg.shape
    tm = min(ROW_TILE, n)
    n_items = meta[0].shape[0]
    xmap = lambda w, tile, ex, *_: (tile[w], 0)
    wmap = lambda w, tile, ex, *_: (ex[w], 0, 0)
    grid_spec = pltpu.PrefetchScalarGridSpec(
        num_scalar_prefetch=len(meta),
        grid=(n_items,),
        in_specs=[pl.BlockSpec((tm, d), xmap),
                  pl.BlockSpec((1, d, f), wmap), pl.BlockSpec((1, d, f), wmap),
                  pl.BlockSpec((1, 1, f), wmap), pl.BlockSpec((1, 1, f), wmap),
                  pl.BlockSpec((1, f, d), wmap), pl.BlockSpec((1, 1, d), wmap)],
        out_specs=pl.BlockSpec((tm * ROW_SUBLANES, V7X_LANES), lambda w, *_: (w, 0)),
        scratch_shapes=[pltpu.VMEM((f, d), BF16)],
    )
    assert d == ROW_SUBLANES * V7X_LANES
    return pl.pallas_call(
        functools.partial(_experts_kernel, tm=tm),
        grid_spec=grid_spec,
        out_shape=jax.ShapeDtypeStruct((n_items * tm * ROW_SUBLANES, V7X_LANES), F32),
        compiler_params=_cparams("arbitrary"),
        name="experts",
    )(*meta, xs, wg, wu, bg, bu, wd, bd)


def _expert_work_items(counts, n_rows, tm):
    e = counts.shape[0]
    n_tiles = n_rows // tm
    n_items = n_tiles + e - 1
    ends = jnp.cumsum(counts)
    starts = ends - counts
    first_tile = starts // tm
    n_t = jnp.where(counts > 0, (ends - 1) // tm - first_tile + 1, 0)
    item_end = jnp.cumsum(n_t)
    item_start = item_end - n_t
    total = item_end[-1]
    w = jnp.arange(n_items, dtype=I32)
    active = w < total
    w_eff = jnp.minimum(w, total - 1)
    ex = jnp.minimum(jnp.sum((item_end[None, :] <= w_eff[:, None]).astype(I32), axis=1), e - 1)
    mine = ex[:, None] == jnp.arange(e, dtype=I32)[None, :]
    pick = lambda v: jnp.sum(jnp.where(mine, v[None, :], 0), axis=1)
    tile = (pick(first_tile) + (w_eff - pick(item_start))).astype(I32)
    lo = jnp.clip(pick(starts) - tile * tm, 0, tm).astype(I32)
    hi = jnp.clip(pick(ends) - tile * tm, 0, tm).astype(I32)
    fresh = jnp.concatenate([jnp.ones((1,), I32), (ex[1:] != ex[:-1]).astype(I32)])
    return (tile, ex, lo, hi, active.astype(I32), fresh), first_tile, item_start


def _combine_kernel(dest_ref, dest_next_ref, x2_ref, gate_ref, g_ref, y_hbm, o_ref, ybuf, sems, *, tm, n_steps):
    i = pl.program_id(0)
    slot = i % 2

    def gather(dref, into):
        def issue(t, carry):
            for k in range(TOP_K):
                src = pl.multiple_of(dref[k, t] * ROW_SUBLANES, ROW_SUBLANES)
                dst = pl.multiple_of(t * ROW_SUBLANES, ROW_SUBLANES)
                _row_copy(y_hbm.at[pl.ds(src, ROW_SUBLANES), :], ybuf.at[into, k, pl.ds(dst, ROW_SUBLANES), :],
                          sems.at[into]).start()
            return carry
        lax.fori_loop(0, tm, issue, 0, unroll=DMA_UNROLL)

    @pl.when(i == 0)
    def _():
        gather(dest_ref, 0)

    @pl.when(i + 1 < n_steps)
    def _():
        gather(dest_next_ref, 1 - slot)

    for k in range(TOP_K):
        _row_copy(y_hbm.at[pl.ds(0, tm * ROW_SUBLANES), :], ybuf.at[slot, k], sems.at[slot]).wait()

    gates = gate_ref[...]
    cols = []
    for s in range(ROW_SUBLANES):
        acc = x2_ref[:, s * V7X_LANES:(s + 1) * V7X_LANES]
        for k in range(TOP_K):
            acc = acc + ybuf[slot, k, pl.ds(s, tm, stride=ROW_SUBLANES), :] * gates[:, k:k + 1]
        cols.append(acc)
    o_ref[...] = _rms(jnp.concatenate(cols, axis=1), g_ref[...])


def _combine(y, y_row, x2, gates, g_final):
    t, d = x2.shape
    tm = min(DMA_TILE, t)
    n_steps = t // tm
    rows_spec = lambda f: pl.BlockSpec((TOP_K, tm), f, memory_space=pltpu.SMEM)
    return pl.pallas_call(
        functools.partial(_combine_kernel, tm=tm, n_steps=n_steps),
        grid=(n_steps,),
        in_specs=[rows_spec(lambda i: (0, i)),
                  rows_spec(lambda i: (0, jnp.minimum(i + 1, n_steps - 1))),
                  pl.BlockSpec((tm, d), lambda i: (i, 0)),
                  pl.BlockSpec((tm, V7X_LANES), lambda i: (i, 0)),
                  pl.BlockSpec((1, d), lambda i: (0, 0)),
                  pl.BlockSpec(memory_space=pl.ANY)],
        out_specs=pl.BlockSpec((tm, d), lambda i: (i, 0)),
        out_shape=jax.ShapeDtypeStruct((t, d), F32),
        scratch_shapes=[pltpu.VMEM((2, TOP_K, tm * ROW_SUBLANES, V7X_LANES), F32), pltpu.SemaphoreType.DMA((2,))],
        compiler_params=_cparams("arbitrary"),
        name="combine",
    )(y_row, y_row, x2, gates, g_final, y)


def _alibi_slopes():
    n = 2 * N_HEADS
    s = 2.0 ** (-8.0 * (jnp.arange(n, dtype=F32) + 1.0) / n)
    return s[0::2], s[1::2]


def kernel(x, mem, attn_norm_g, mem_norm_g, w_in, diff_lambda_qk, diff_subln_g, w_mem_kv, w_out,
           ffn_norm_g, w_router, b_router, w_gate_up, b_gate_up, w_down, b_down, final_norm_g):
    b, s, d = x.shape
    t = b * s
    assert w_in.shape[0] == 1, "the final norm is fused into the layer's combine step: one layer only"
    assert s % (DIL_DILATIONS[-1] * DIL_TILE) == 0 and t % TOK_TILE == 0
    n_exp = w_router.shape[-1]
    diff_slopes, dil_slopes = _alibi_slopes()
    x2d = x.reshape(t, d)

    for l in range(1):
        lam_init = 0.8 - 0.6 * math.exp(-0.3 * l)
        mk, mv = _mem_kv(mem, mem_norm_g[l][None], w_mem_kv[l].astype(BF16))
        qd, kd, vd, ql, kl, vl, mq = _in_proj(x2d, attn_norm_g[l][None], w_in[l])
        sh = lambda a: a.reshape(b, s, a.shape[-1])
        o_diff = _diff_attention(sh(qd), sh(kd), sh(vd), diff_slopes, diff_lambda_qk[l],
                                 diff_subln_g[l][None], lam_init)
        o_dil = _dil_attention(sh(ql), sh(kl), sh(vl), dil_slopes)
        o_mem = _mem_attention(sh(mq), mk, mv)

        w_router_f = jnp.zeros((d, V7X_LANES), F32).at[:, :n_exp].set(w_router[l])
        w_router_hi = w_router_f.astype(BF16)
        w_router_p = jnp.stack([w_router_hi, (w_router_f - w_router_hi.astype(F32)).astype(BF16)])
        b_router_p = jnp.full((1, V7X_LANES), NEG, F32).at[0, :n_exp].set(b_router[l])
        x2, h2, route, gates, cnt = _out_router(
            x2d, o_diff.reshape(t, DIFF_WIDTH), o_dil.reshape(t, GROUP_WIDTH), o_mem.reshape(t, GROUP_WIDTH),
            w_out[l].astype(BF16), ffn_norm_g[l][None], w_router_p, b_router_p, n_exp)

        counts = cnt[:, 0]
        starts = jnp.cumsum(counts) - counts
        onehot = route[:TOP_K, :, None] == jnp.arange(n_exp, dtype=I32)[None, None, :]
        pick = lambda v: jnp.sum(jnp.where(onehot, v[None, None, :], 0), axis=-1)
        dest = (route[TOP_K:] + pick(starts)).astype(I32)

        xs = _dispatch(h2, dest)

        f = w_down.shape[2]
        tm_rows = min(ROW_TILE, t * TOP_K)
        w_gate, w_up = _split_gate_up(w_gate_up[l])
        bgu = b_gate_up[l].reshape(n_exp, 1, f, 2)
        meta, first_tile, item_start = _expert_work_items(counts, t * TOP_K, tm_rows)
        y = _experts(xs, meta, w_gate, w_up, bgu[..., 0], bgu[..., 1], w_down[l], b_down[l][:, None, :])

        item = pick(item_start) + dest // tm_rows - pick(first_tile)
        y_row = (item * tm_rows + dest % tm_rows).astype(I32)
        out = _combine(y, y_row, x2, gates, final_norm_g[None])
    return out.reshape(b, s, d)
```

```python
import functools
import math

import jax
import jax.numpy as jnp
from jax import lax
from jax.experimental import pallas as pl
from jax.experimental.pallas import tpu as pltpu

F32 = jnp.float32
BF16 = jnp.bfloat16
I32 = jnp.int32

NORM_EPS = 1e-5
HEAD_DIM = 64
N_HEADS = 4
DIFF_VDIM = 2 * HEAD_DIM
DIFF_WIDTH = N_HEADS * DIFF_VDIM
GROUP_WIDTH = N_HEADS * HEAD_DIM
DIL_DILATIONS = (1, 4, 16)
DIL_WINDOW_KEYS = 128
TOP_K = 4
SWIGLU_ALPHA = 1.702
SWIGLU_LIMIT = 7.0
NEG = -1e30

V7X_LANES = 128
V7X_SUBLANES = 8
V7X_MXU_DIM = 256
V7X_VMEM_LIMIT = 56 * 1024 * 1024

TOK_TILE = 512
ATT_TILE = 512
POS_SPLIT = 64
DIFF_HEADS_PER_STEP = 1
DIL_TILE = DIL_WINDOW_KEYS
DIL_CHUNK = 256
DIL_UNROLL = 8
ROW_TILE = 512
ROW_SUBLANES = V7X_SUBLANES
DMA_TILE = 256
DISPATCH_TILE = 1024
DMA_UNROLL = 4


def _cparams(*sem):
    return pltpu.CompilerParams(dimension_semantics=tuple(sem), vmem_limit_bytes=V7X_VMEM_LIMIT)


def _rms(x, g):
    return x * lax.rsqrt(jnp.mean(x * x, axis=-1, keepdims=True) + NORM_EPS) * g


def _dot_nt(a, b):
    return lax.dot_general(a, b, (((1,), (1,)), ((), ())), preferred_element_type=F32)


def _dot(a, b):
    return jnp.dot(a, b, preferred_element_type=F32)


def _memkv_kernel(mem_ref, g_ref, w_ref, mk_ref, mv_ref):
    h = _rms(mem_ref[0], g_ref[...]).astype(BF16)
    kv = _dot(h, w_ref[...])
    mk_ref[0] = kv[:, :GROUP_WIDTH].astype(BF16)
    mv_ref[0] = kv[:, GROUP_WIDTH:].astype(BF16)


def _mem_kv(mem, g, w):
    b, m, d = mem.shape
    out = jax.ShapeDtypeStruct((b, m, GROUP_WIDTH), BF16)
    return pl.pallas_call(
        _memkv_kernel,
        grid=(b,),
        in_specs=[pl.BlockSpec((1, m, d), lambda i: (i, 0, 0)),
                  pl.BlockSpec((1, d), lambda i: (0, 0)),
                  pl.BlockSpec((d, 2 * GROUP_WIDTH), lambda i: (0, 0))],
        out_specs=[pl.BlockSpec((1, m, GROUP_WIDTH), lambda i: (i, 0, 0))] * 2,
        out_shape=[out, out],
        compiler_params=_cparams("arbitrary"),
        name="mem_kv",
    )(mem, g, w)


_IN_SEGMENTS = ((DIFF_WIDTH, HEAD_DIM ** -0.5), (DIFF_WIDTH, 1.0), (DIFF_WIDTH, 1.0),
                (GROUP_WIDTH, HEAD_DIM ** -0.5), (GROUP_WIDTH, 1.0), (GROUP_WIDTH, 1.0),
                (GROUP_WIDTH, HEAD_DIM ** -0.5))


def _inproj_kernel(x_ref, g_ref, w_ref, *refs):
    out_refs, w_bf16 = refs[:-1], refs[-1]

    @pl.when(pl.program_id(0) == 0)
    def _():
        w_bf16[...] = w_ref[...].astype(BF16)

    h = _rms(x_ref[...], g_ref[...]).astype(BF16)
    c0 = 0
    for ref, (n, scale) in zip(out_refs, _IN_SEGMENTS):
        p = _dot(h, w_bf16[:, c0:c0 + n])
        if scale != 1.0:
            p = p * scale
        ref[...] = p.astype(BF16)
        c0 += n


def _in_proj(x2d, g, w):
    t, d = x2d.shape
    tm = min(TOK_TILE, t)
    return pl.pallas_call(
        _inproj_kernel,
        grid=(t // tm,),
        in_specs=[pl.BlockSpec((tm, d), lambda i: (i, 0)),
                  pl.BlockSpec((1, d), lambda i: (0, 0)),
                  pl.BlockSpec(w.shape, lambda i: (0, 0))],
        out_specs=[pl.BlockSpec((tm, n), lambda i: (i, 0)) for n, _ in _IN_SEGMENTS],
        out_shape=[jax.ShapeDtypeStruct((t, n), BF16) for n, _ in _IN_SEGMENTS],
        scratch_shapes=[pltpu.VMEM(w.shape, BF16)],
        compiler_params=_cparams("arbitrary"),
        name="in_proj",
    )(x2d, g, w)


def _diff_kernel(slopes_ref, lam_ref, g_ref, q_ref, k_ref, v_ref, kpos_ref, ones_ref, o_ref,
                 m_ref, a_ref, s_ref, *, lam_init, tile):
    n_full = 2 * pl.program_id(2)
    lane = lax.broadcasted_iota(I32, (tile, DIFF_VDIM), 1)
    heads = range(DIFF_HEADS_PER_STEP)
    qs = []
    for hd in heads:
        slope = slopes_ref[DIFF_HEADS_PER_STEP * pl.program_id(1) + hd]
        qpos = jnp.where(lane == 0, slope * float(POS_SPLIT), jnp.where(lane == 1, slope, 0.0)).astype(BF16)
        for sub in range(2):
            q = q_ref[0, sub * tile:(sub + 1) * tile, hd * DIFF_VDIM:(hd + 1) * DIFF_VDIM]
            zero = jnp.zeros_like(q)
            qs.append(jnp.concatenate([jnp.where(lane < HEAD_DIM, q, zero), qpos], axis=1))
            qs.append(jnp.concatenate([jnp.where(lane >= HEAD_DIM, q, zero), qpos], axis=1))
    chains = lambda subs: [(hd, 4 * hd + 2 * sub + mp) for hd in heads for sub in subs for mp in range(2)]
    sub_a, sub_b, both = chains((0,)), chains((1,)), chains((0, 1))

    m_ref[...] = jnp.full(m_ref.shape, NEG, F32)
    a_ref[...] = jnp.zeros(a_ref.shape, F32)

    def scores(j, slot, which):
        start = pl.multiple_of(j * tile, tile)
        kk = [jnp.concatenate([k_ref[0, pl.ds(start, tile), hd * DIFF_VDIM:(hd + 1) * DIFF_VDIM],
                               kpos_ref[pl.ds(start, tile), :]], axis=1) for hd in heads]
        for hd, c in which:
            s_ref[slot, c] = _dot_nt(qs[c], kk[hd])

    def consume(j, slot, which, diagonal):
        start = pl.multiple_of(j * tile, tile)
        vv = [jnp.concatenate([v_ref[0, pl.ds(start, tile), hd * DIFF_VDIM:(hd + 1) * DIFF_VDIM], ones_ref[...]],
                              axis=1) for hd in heads]
        for hd, c in which:
            s = s_ref[slot, c]
            if diagonal:
                row = lax.broadcasted_iota(I32, s.shape, 0)
                col = lax.broadcasted_iota(I32, s.shape, 1)
                s = jnp.where(col <= row, s, NEG)
            m_old = m_ref[c]
            m_new = jnp.maximum(m_old, jnp.max(s, axis=1, keepdims=True))
            p = jnp.exp(s - m_new)
            a_ref[c] = jnp.exp(m_old - m_new) * a_ref[c] + _dot(p.astype(BF16), vv[hd])
            m_ref[c] = m_new

    scores(0, 0, both)

    def pair(p, carry):
        j = 2 * p
        scores(j + 1, 1, both)
        consume(j, 0, both, False)
        scores(j + 2, 0, both)
        consume(j + 1, 1, both, False)
        return carry

    lax.fori_loop(0, n_full // 2, pair, 0)
    scores(n_full + 1, 1, sub_b)
    consume(n_full, 0, sub_a, True)
    consume(n_full, 0, sub_b, False)
    consume(n_full + 1, 1, sub_b, True)

    lv = lam_ref[...]
    lam = (jnp.exp(jnp.sum(lv[0:1] * lv[1:2], axis=1, keepdims=True))
           - jnp.exp(jnp.sum(lv[2:3] * lv[3:4], axis=1, keepdims=True)) + lam_init)
    n = DIFF_VDIM
    for hd in heads:
        for sub in range(2):
            a1, a2 = a_ref[4 * hd + 2 * sub], a_ref[4 * hd + 2 * sub + 1]
            o = (a1[:, :n] * (1.0 / a1[:, n:n + 1])) - lam * (a2[:, :n] * (1.0 / a2[:, n:n + 1]))
            o = _rms(o, g_ref[...]) * (1.0 - lam_init)
            o_ref[0, sub * tile:(sub + 1) * tile, hd * n:(hd + 1) * n] = o.astype(o_ref.dtype)


def _diff_attention(qd, kd, vd, slopes, lam_qk, subln_g, lam_init):
    b, s, _ = qd.shape
    tile = ATT_TILE
    assert s % (2 * tile) == 0
    pos = jnp.arange(s, dtype=I32)[:, None]
    lane = jnp.arange(V7X_LANES, dtype=I32)[None, :]
    kpos = jnp.where(lane == 0, pos // POS_SPLIT, jnp.where(lane == 1, pos % POS_SPLIT, 0)).astype(BF16)
    ones = jnp.broadcast_to(lane == 0, (tile, V7X_LANES)).astype(BF16)
    vec = lambda shape: pltpu.VMEM(shape, F32)
    const = lambda a: pl.BlockSpec(a.shape, lambda bb, hh, ii: (0, 0))
    width = DIFF_HEADS_PER_STEP * DIFF_VDIM
    n_chains = 4 * DIFF_HEADS_PER_STEP
    return pl.pallas_call(
        functools.partial(_diff_kernel, lam_init=lam_init, tile=tile),
        grid=(b, N_HEADS // DIFF_HEADS_PER_STEP, s // (2 * tile)),
        in_specs=[pl.BlockSpec(memory_space=pltpu.SMEM), const(lam_qk), const(subln_g),
                  pl.BlockSpec((1, 2 * tile, width), lambda bb, hh, ii: (bb, ii, hh)),
                  pl.BlockSpec((1, s, width), lambda bb, hh, ii: (bb, 0, hh)),
                  pl.BlockSpec((1, s, width), lambda bb, hh, ii: (bb, 0, hh)),
                  const(kpos), const(ones)],
        out_specs=pl.BlockSpec((1, 2 * tile, width), lambda bb, hh, ii: (bb, ii, hh)),
        out_shape=jax.ShapeDtypeStruct((b, s, DIFF_WIDTH), BF16),
        scratch_shapes=[vec((n_chains, tile, 1)), vec((n_chains, tile, 2 * DIFF_VDIM)),
                        vec((2, n_chains, tile, tile))],
        compiler_params=_cparams("arbitrary", "arbitrary", "arbitrary"),
        name="diff_attn",
    )(slopes, lam_qk, subln_g, qd, kd, vd, kpos, ones)


def _head_mask(lane, h):
    return (lane >= h * HEAD_DIM) & (lane < (h + 1) * HEAD_DIM)


def _split3(x):
    hi = x.astype(BF16)
    rem = x - hi.astype(F32)
    mid = rem.astype(BF16)
    return hi, mid, (rem - mid.astype(F32)).astype(BF16)


def _dil_kernel(q_ref, k_ref, v_ref, perm_ref, bias_ref, spread_ref, o_ref, qc, kc, vc, res_o, res_l, *, seq):
    tile, chunk = DIL_TILE, DIL_CHUNK
    n_blocks, n_chunks = seq // tile, seq // chunk
    lane = lax.broadcasted_iota(I32, (tile, GROUP_WIDTH), 1)
    tail_lane = lax.broadcasted_iota(I32, (tile, V7X_LANES), 1)

    def to_class_order(bi, r):
        n, sr = chunk // r, seq // r

        def one_chunk(t, carry):
            src0 = pl.multiple_of(t * chunk, chunk)
            for src, dst in ((q_ref, qc), (k_ref, kc), (v_ref, vc)):
                y = _dot(perm_ref[bi - 1, 0], src[0, pl.ds(src0, chunk), :]).astype(BF16)
                for rho in range(r):
                    dst[pl.ds(pl.multiple_of(rho * sr + t * n, n), n), :] = y[rho * n:(rho + 1) * n, :]
            return carry

        lax.fori_loop(0, n_chunks, one_chunk, 0)

    def attend(bi, r, load):
        blocks_per_class = (seq // r) // tile

        def one_block(g, in_class):
            cur = pl.multiple_of(g * tile, tile)
            prev = pl.multiple_of(jnp.maximum(g - 1, 0) * tile, tile)
            q = load(0, cur)
            qs = jnp.concatenate([jnp.where(_head_mask(lane, h), q, jnp.zeros_like(q)) for h in range(N_HEADS)], axis=0)
            kk = jnp.concatenate([load(1, prev), load(1, cur)], axis=0)
            vv = jnp.concatenate([load(2, prev), load(2, cur)], axis=0)
            if in_class is None:
                bias = jnp.where((g % blocks_per_class) != 0, bias_ref[bi, 1], bias_ref[bi, 0])
            else:
                bias = bias_ref[bi, 1 if in_class else 0]
            s = _dot_nt(qs, kk) + bias
            m = jnp.max(s, axis=1, keepdims=True)
            p = jnp.exp(s - m)
            l = jnp.sum(p, axis=1, keepdims=True)
            pv = _dot(p.astype(BF16), vv) * (1.0 / l)
            lse = m + jnp.log(l)
            out = jnp.zeros((tile, GROUP_WIDTH), F32)
            tail = jnp.zeros((tile, V7X_LANES), F32)
            for h in range(N_HEADS):
                rows = slice(h * tile, (h + 1) * tile)
                out = jnp.where(_head_mask(lane, h), pv[rows], out)
                tail = jnp.where(tail_lane == h, lse[rows], tail)
            res_o[bi, pl.ds(cur, tile), :] = out.astype(BF16)
            res_l[bi, pl.ds(cur, tile), :] = tail

        def block_group(outer, carry):
            for u in range(DIL_UNROLL):
                if blocks_per_class <= DIL_UNROLL:
                    in_class = u % blocks_per_class
                else:
                    in_class = u if u else None
                one_block(outer * DIL_UNROLL + u, in_class)
            return carry

        assert n_blocks % DIL_UNROLL == 0 and max(blocks_per_class, DIL_UNROLL) % min(blocks_per_class, DIL_UNROLL) == 0
        lax.fori_loop(0, n_blocks // DIL_UNROLL, block_group, 0)

    refs = (q_ref, k_ref, v_ref)
    scratch = (qc, kc, vc)
    attend(0, 1, lambda which, start: refs[which][0, pl.ds(start, tile), :])
    for bi, r in enumerate(DIL_DILATIONS[1:], start=1):
        to_class_order(bi, r)
        attend(bi, r, lambda which, start: scratch[which][pl.ds(start, tile), :])

    def merge_chunk(t, carry):
        base = pl.multiple_of(t * chunk, chunk)
        def pack3(lse):
            hi, mid, lo = _split3(lse)
            return (hi.astype(F32) + pltpu.roll(mid.astype(F32), N_HEADS, 1)
                    + pltpu.roll(lo.astype(F32), 2 * N_HEADS, 1)).astype(BF16)

        outs, lses = [], []
        for bi, r in enumerate(DIL_DILATIONS):
            if r == 1:
                out = res_o[0, pl.ds(base, chunk), :].astype(F32)
                packed = pack3(res_l[0, pl.ds(base, chunk), :])
            else:
                n, sr = chunk // r, seq // r
                rows = [pl.ds(pl.multiple_of(rho * sr + t * n, n), n) for rho in range(r)]
                back = perm_ref[bi - 1, 1]
                out = _dot(back, jnp.concatenate([res_o[bi, rw, :] for rw in rows], axis=0))
                packed = _dot(back, pack3(jnp.concatenate([res_l[bi, rw, :] for rw in rows], axis=0))).astype(BF16)
            outs.append(out)
            lses.append(_dot(packed, spread_ref[...]))
        big = jnp.maximum(jnp.maximum(lses[0], lses[1]), lses[2])
        wts = [jnp.exp(z - big) for z in lses]
        o = (wts[0] * outs[0] + wts[1] * outs[1] + wts[2] * outs[2]) * (1.0 / (wts[0] + wts[1] + wts[2]))
        o_ref[0, pl.ds(base, chunk), :] = o.astype(o_ref.dtype)
        return carry

    lax.fori_loop(0, n_chunks, merge_chunk, 0)


def _dil_constants(slopes, seq):
    chunk, tile = DIL_CHUNK, DIL_TILE
    perms = []
    for r in DIL_DILATIONS[1:]:
        n = chunk // r
        dst = jnp.arange(chunk, dtype=I32)
        src = r * (dst % n) + dst // n
        fwd = (src[:, None] == jnp.arange(chunk, dtype=I32)[None, :]).astype(BF16)
        perms.append(jnp.stack([fwd, fwd.T]))
    m = jnp.arange(tile, dtype=I32)[:, None]
    j = jnp.arange(2 * tile, dtype=I32)[None, :]
    dist = jnp.where(j < tile, m - j + tile, m - (j - tile))
    ok_cur = (j >= tile) & (dist >= 0)
    ok_prev = (j < tile) & (dist <= DIL_WINDOW_KEYS)
    biases = []
    for r in DIL_DILATIONS:
        per_head = [-(slopes[h] * r) * dist.astype(F32) for h in range(N_HEADS)]
        with_prev = jnp.concatenate([jnp.where(ok_cur | ok_prev, b, NEG) for b in per_head], axis=0)
        no_prev = jnp.concatenate([jnp.where(ok_cur, b, NEG) for b in per_head], axis=0)
        biases.append(jnp.stack([no_prev, with_prev]))
    src_lane = jnp.arange(V7X_LANES, dtype=I32)[:, None]
    spread = ((src_lane < 3 * N_HEADS)
              & (src_lane % N_HEADS == jnp.arange(GROUP_WIDTH, dtype=I32)[None, :] // HEAD_DIM)).astype(BF16)
    return jnp.stack(perms), jnp.stack(biases), spread


def _dil_attention(ql, kl, vl, slopes):
    b, s, w = ql.shape
    perms, biases, spread = _dil_constants(slopes, s)
    row = pl.BlockSpec((1, s, w), lambda i: (i, 0, 0))
    const = lambda a: pl.BlockSpec(a.shape, lambda i: (0,) * a.ndim)
    return pl.pallas_call(
        functools.partial(_dil_kernel, seq=s),
        grid=(b,),
        in_specs=[row, row, row, const(perms), const(biases), const(spread)],
        out_specs=row,
        out_shape=jax.ShapeDtypeStruct((b, s, w), BF16),
        scratch_shapes=[pltpu.VMEM((s, w), BF16)] * 3
                       + [pltpu.VMEM((len(DIL_DILATIONS), s, w), BF16),
                          pltpu.VMEM((len(DIL_DILATIONS), s, V7X_LANES), F32)],
        compiler_params=_cparams("arbitrary"),
        name="dil_attn",
    )(ql, kl, vl, perms, biases, spread)


def _memattn_kernel(q_ref, mk_ref, mv_ref, o_ref):
    q = q_ref[0]
    tile = q.shape[0]
    lane = lax.broadcasted_iota(I32, q.shape, 1)
    qs = jnp.concatenate([jnp.where(_head_mask(lane, h), q, jnp.zeros_like(q)) for h in range(N_HEADS)], axis=0)
    s = _dot_nt(qs, mk_ref[0])
    m = jnp.max(s, axis=1, keepdims=True)
    p = jnp.exp(s - m)
    l = jnp.sum(p, axis=1, keepdims=True)
    pv = _dot(p.astype(BF16), mv_ref[0]) * (1.0 / l)
    out = jnp.zeros(q.shape, F32)
    for h in range(N_HEADS):
        out = jnp.where(_head_mask(lane, h), pv[h * tile:(h + 1) * tile], out)
    o_ref[0] = out.astype(o_ref.dtype)


def _mem_attention(mq, mk, mv):
    b, s, w = mq.shape
    m = mk.shape[1]
    tile = min(TOK_TILE, s)
    return pl.pallas_call(
        _memattn_kernel,
        grid=(b, s // tile),
        in_specs=[pl.BlockSpec((1, tile, w), lambda bb, ii: (bb, ii, 0)),
                  pl.BlockSpec((1, m, w), lambda bb, ii: (bb, 0, 0)),
                  pl.BlockSpec((1, m, w), lambda bb, ii: (bb, 0, 0))],
        out_specs=pl.BlockSpec((1, tile, w), lambda bb, ii: (bb, ii, 0)),
        out_shape=jax.ShapeDtypeStruct((b, s, w), BF16),
        compiler_params=_cparams("arbitrary", "arbitrary"),
        name="mem_attn",
    )(mq, mk, mv)


def _outrouter_kernel(x_ref, od_ref, ol_ref, om_ref, wo_ref, g_ref, wr_ref, br_ref,
                      x2_ref, h2_ref, route_ref, gate_ref, cnt_ref, run_ref, *, tm, n_exp):
    step = pl.program_id(0)

    @pl.when(step == 0)
    def _():
        run_ref[...] = jnp.zeros(run_ref.shape, F32)

    c1, c2 = DIFF_WIDTH, DIFF_WIDTH + GROUP_WIDTH
    mix = (_dot(od_ref[...], wo_ref[0:c1, :]) + _dot(ol_ref[...], wo_ref[c1:c2, :])
           + _dot(om_ref[...], wo_ref[c2:, :]))
    x2 = x_ref[...] + mix
    x2_ref[...] = x2
    h2 = _rms(x2, g_ref[...])
    h2_ref[...] = h2

    h_hi = h2.astype(BF16)
    h_lo = (h2 - h_hi.astype(F32)).astype(BF16)
    logits = (_dot(h_hi, wr_ref[0]) + (_dot(h_hi, wr_ref[1]) + _dot(h_lo, wr_ref[0]))) + br_ref[...]
    work = logits.T[:n_exp]
    erow = lax.broadcasted_iota(I32, work.shape, 0)
    erow_rev = (n_exp - 1 - erow).astype(F32)
    sel = jnp.zeros(work.shape, F32)
    vals, idxs = [], []
    for _ in range(TOP_K):
        mx = jnp.max(work, axis=0, keepdims=True)
        key = jnp.max(jnp.where(work == mx, erow_rev, -1.0), axis=0, keepdims=True)
        hit = erow_rev == key
        sel = jnp.where(hit, 1.0, sel)
        work = jnp.where(hit, -jnp.inf, work)
        vals.append(mx)
        idxs.append(n_exp - 1 - key.astype(I32))

    r_io = lax.broadcasted_iota(I32, (tm, tm), 0)
    c_io = lax.broadcasted_iota(I32, (tm, tm), 1)
    earlier = jnp.where(r_io < c_io, 1.0, 0.0).astype(BF16)
    before = _dot(sel.astype(BF16), earlier) + run_ref[...]
    run_ref[...] = run_ref[...] + jnp.sum(sel, axis=1, keepdims=True)
    cnt_ref[...] = jnp.broadcast_to(run_ref[...], cnt_ref.shape).astype(I32)

    exps = [jnp.exp(v - vals[0]) for v in vals]
    inv = 1.0 / (exps[0] + exps[1] + exps[2] + exps[3])
    krow = lax.broadcasted_iota(I32, (2 * TOP_K, tm), 0)
    route = jnp.zeros((2 * TOP_K, tm), I32)
    gate_t = jnp.zeros((V7X_LANES, tm), F32)
    grow = lax.broadcasted_iota(I32, gate_t.shape, 0)
    for k in range(TOP_K):
        rk = jnp.sum(jnp.where(erow == idxs[k], before, 0.0), axis=0, keepdims=True)
        route = jnp.where(krow == k, idxs[k], route)
        route = jnp.where(krow == TOP_K + k, rk.astype(I32), route)
        gate_t = jnp.where(grow == k, exps[k] * inv, gate_t)
    route_ref[...] = route
    gate_ref[...] = gate_t.T


def _out_router(x2d, o_diff, o_dil, o_mem, w_out, g, w_router_p, b_router_p, n_exp):
    t, d = x2d.shape
    tm = min(TOK_TILE, t)
    row = lambda n: pl.BlockSpec((tm, n), lambda i: (i, 0))
    full = lambda a: pl.BlockSpec(a.shape, lambda i: (0,) * a.ndim)
    return pl.pallas_call(
        functools.partial(_outrouter_kernel, tm=tm, n_exp=n_exp),
        grid=(t // tm,),
        in_specs=[row(d), row(DIFF_WIDTH), row(GROUP_WIDTH), row(GROUP_WIDTH),
                  full(w_out), full(g), full(w_router_p), full(b_router_p)],
        out_specs=[row(d), row(d), pl.BlockSpec((2 * TOP_K, tm), lambda i: (0, i)), row(V7X_LANES),
                   pl.BlockSpec((n_exp, V7X_LANES), lambda i: (0, 0))],
        out_shape=[jax.ShapeDtypeStruct((t, d), F32), jax.ShapeDtypeStruct((t, d), F32),
                   jax.ShapeDtypeStruct((2 * TOP_K, t), I32), jax.ShapeDtypeStruct((t, V7X_LANES), F32),
                   jax.ShapeDtypeStruct((n_exp, V7X_LANES), I32)],
        scratch_shapes=[pltpu.VMEM((n_exp, 1), F32)],
        compiler_params=_cparams("arbitrary"),
        name="out_router",
    )(x2d, o_diff, o_dil, o_mem, w_out, g, w_router_p, b_router_p)


def _row_copy(src, dst, sem):
    return pltpu.make_async_copy(src, dst, sem)


def _dispatch_kernel(dest_ref, h_ref, xs_hbm, sem, *, tm):
    def issue(t, carry):
        for k in range(TOP_K):
            d = dest_ref[k, t]
            _row_copy(h_ref.at[pl.ds(t, 1), :], xs_hbm.at[pl.ds(d, 1), :], sem).start()
        return carry

    lax.fori_loop(0, tm, issue, 0, unroll=DMA_UNROLL)
    for _ in range(TOP_K):
        _row_copy(h_ref, xs_hbm.at[pl.ds(0, tm), :], sem).wait()


def _dispatch(h2, dest):
    t, d = h2.shape
    tm = min(DISPATCH_TILE, t)
    return pl.pallas_call(
        functools.partial(_dispatch_kernel, tm=tm),
        grid=(t // tm,),
        in_specs=[pl.BlockSpec((TOP_K, tm), lambda i: (0, i), memory_space=pltpu.SMEM),
                  pl.BlockSpec((tm, d), lambda i: (i, 0))],
        out_specs=pl.BlockSpec(memory_space=pl.ANY),
        out_shape=jax.ShapeDtypeStruct((t * TOP_K, d), h2.dtype),
        scratch_shapes=[pltpu.SemaphoreType.DMA],
        compiler_params=_cparams("arbitrary"),
        name="dispatch",
    )(dest, h2)


def _split_gate_up_kernel(w_ref, sel_ref, wg_ref, wu_ref):
    x = w_ref[0].astype(BF16)
    k, n = sel_ref.shape[1], sel_ref.shape[2]
    for j in range(wg_ref.shape[2] // n):
        blk = x[:, j * k:(j + 1) * k]
        wg_ref[0, :, j * n:(j + 1) * n] = _dot(blk, sel_ref[0]).astype(BF16)
        wu_ref[0, :, j * n:(j + 1) * n] = _dot(blk, sel_ref[1]).astype(BF16)


def _split_gate_up(w):
    e, d, f2 = w.shape
    f = f2 // 2
    rows, n = min(TOK_TILE, d), V7X_MXU_DIM
    src = jnp.arange(2 * n, dtype=I32)[:, None]
    dst = jnp.arange(n, dtype=I32)[None, :]
    sel = jnp.stack([src == 2 * dst, src == 2 * dst + 1]).astype(BF16)
    out = jax.ShapeDtypeStruct((e, d, f), BF16)
    return pl.pallas_call(
        _split_gate_up_kernel,
        grid=(e, d // rows),
        in_specs=[pl.BlockSpec((1, rows, f2), lambda i, j: (i, j, 0)),
                  pl.BlockSpec(sel.shape, lambda i, j: (0, 0, 0))],
        out_specs=[pl.BlockSpec((1, rows, f), lambda i, j: (i, j, 0))] * 2,
        out_shape=[out, out],
        compiler_params=_cparams("arbitrary", "arbitrary"),
        name="split_gate_up",
    )(w, sel)


def _experts_kernel(tile_ref, exp_ref, lo_ref, hi_ref, act_ref, fresh_ref,
                    xs_ref, wg_ref, wu_ref, bg_ref, bu_ref, wd_ref, bd_ref, y_ref, wd_bf16, *, tm):
    w = pl.program_id(0)

    @pl.when(fresh_ref[w] == 1)
    def _():
        wd_bf16[...] = wd_ref[0].astype(BF16)

    lo, hi, half = lo_ref[w], hi_ref[w], tm // 2
    active = act_ref[w] == 1
    top_only = active & (hi <= half)
    bottom_only = active & (lo >= half)

    def mlp(r0, nr):
        x = xs_ref[r0:r0 + nr, :].astype(BF16)
        gate = jnp.minimum(_dot(x, wg_ref[0]) + bg_ref[0], SWIGLU_LIMIT)
        up = jnp.clip(_dot(x, wu_ref[0]) + bu_ref[0], -SWIGLU_LIMIT, SWIGLU_LIMIT)
        act = (up + 1.0) * (gate * jax.nn.sigmoid(gate * SWIGLU_ALPHA))
        y = _dot(act.astype(BF16), wd_bf16[...]) + bd_ref[0]
        row = r0 + lax.broadcasted_iota(I32, (nr, V7X_LANES), 0)
        mine = (row >= lo) & (row < hi)
        for s in range(ROW_SUBLANES):
            y_ref[pl.ds(r0 * ROW_SUBLANES + s, nr, stride=ROW_SUBLANES), :] = jnp.where(
                mine, y[:, s * V7X_LANES:(s + 1) * V7X_LANES], 0.0)

    def zero(r0, nr):
        y_ref[r0 * ROW_SUBLANES:(r0 + nr) * ROW_SUBLANES, :] = jnp.zeros((nr * ROW_SUBLANES, V7X_LANES), F32)

    @pl.when(active & jnp.logical_not(top_only | bottom_only))
    def _():
        mlp(0, tm)

    @pl.when(top_only)
    def _():
        mlp(0, half)
        zero(half, half)

    @pl.when(bottom_only)
    def _():
        zero(0, half)
        mlp(half, half)

    @pl.when(jnp.logical_not(active))
    def _():
        zero(0, tm)


def _experts(xs, meta, wg, wu, bg, bu, wd, bd):
    n, d = xs.shape
    e, _, f = wg.shape
    tm = min(ROW_TILE, n)
    n_items = meta[0].shape[0]
    xmap = lambda w, tile, ex, *_: (tile[w], 0)
    wmap = lambda w, tile, ex, *_: (ex[w], 0, 0)
    grid_spec = pltpu.PrefetchScalarGridSpec(
        num_scalar_prefetch=len(meta),
        grid=(n_items,),
        in_specs=[pl.BlockSpec((tm, d), xmap),
                  pl.BlockSpec((1, d, f), wmap), pl.BlockSpec((1, d, f), wmap),
                  pl.BlockSpec((1, 1, f), wmap), pl.BlockSpec((1, 1, f), wmap),
                  pl.BlockSpec((1, f, d), wmap), pl.BlockSpec((1, 1, d), wmap)],
        out_specs=pl.BlockSpec((tm * ROW_SUBLANES, V7X_LANES), lambda w, *_: (w, 0)),
        scratch_shapes=[pltpu.VMEM((f, d), BF16)],
    )
    assert d == ROW_SUBLANES * V7X_LANES
    return pl.pallas_call(
        functools.partial(_experts_kernel, tm=tm),
        grid_spec=grid_spec,
        out_shape=jax.ShapeDtypeStruct((n_items * tm * ROW_SUBLANES, V7X_LANES), F32),
        compiler_params=_cparams("arbitrary"),
        name="experts",
    )(*meta, xs, wg, wu, bg, bu, wd, bd)


def _expert_work_items(counts, n_rows, tm):
    e = counts.shape[0]
    n_tiles = n_rows // tm
    n_items = n_tiles + e - 1
    ends = jnp.cumsum(counts)
    starts = ends - counts
    first_tile = starts // tm
    n_t = jnp.where(counts > 0, (ends - 1) // tm - first_tile + 1, 0)
    item_end = jnp.cumsum(n_t)
    item_start = item_end - n_t
    total = item_end[-1]
    w = jnp.arange(n_items, dtype=I32)
    active = w < total
    w_eff = jnp.minimum(w, total - 1)
    ex = jnp.minimum(jnp.sum((item_end[None, :] <= w_eff[:, None]).astype(I32), axis=1), e - 1)
    mine = ex[:, None] == jnp.arange(e, dtype=I32)[None, :]
    pick = lambda v: jnp.sum(jnp.where(mine, v[None, :], 0), axis=1)
    tile = (pick(first_tile) + (w_eff - pick(item_start))).astype(I32)
    lo = jnp.clip(pick(starts) - tile * tm, 0, tm).astype(I32)
    hi = jnp.clip(pick(ends) - tile * tm, 0, tm).astype(I32)
    fresh = jnp.concatenate([jnp.ones((1,), I32), (ex[1:] != ex[:-1]).astype(I32)])
    return (tile, ex, lo, hi, active.astype(I32), fresh), first_tile, item_start


def _combine_kernel(dest_ref, dest_next_ref, x2_ref, gate_ref, g_ref, y_hbm, o_ref, ybuf, sems, *, tm, n_steps):
    i = pl.program_id(0)
    slot = i % 2

    def gather(dref, into):
        def issue(t, carry):
            for k in range(TOP_K):
                src = pl.multiple_of(dref[k, t] * ROW_SUBLANES, ROW_SUBLANES)
                dst = pl.multiple_of(t * ROW_SUBLANES, ROW_SUBLANES)
                _row_copy(y_hbm.at[pl.ds(src, ROW_SUBLANES), :], ybuf.at[into, k, pl.ds(dst, ROW_SUBLANES), :],
                          sems.at[into]).start()
            return carry
        lax.fori_loop(0, tm, issue, 0, unroll=DMA_UNROLL)

    @pl.when(i == 0)
    def _():
        gather(dest_ref, 0)

    @pl.when(i + 1 < n_steps)
    def _():
        gather(dest_next_ref, 1 - slot)

    for k in range(TOP_K):
        _row_copy(y_hbm.at[pl.ds(0, tm * ROW_SUBLANES), :], ybuf.at[slot, k], sems.at[slot]).wait()

    gates = gate_ref[...]
    cols = []
    for s in range(ROW_SUBLANES):
        acc = x2_ref[:, s * V7X_LANES:(s + 1) * V7X_LANES]
        for k in range(TOP_K):
            acc = acc + ybuf[slot, k, pl.ds(s, tm, stride=ROW_SUBLANES), :] * gates[:, k:k + 1]
        cols.append(acc)
    o_ref[...] = _rms(jnp.concatenate(cols, axis=1), g_ref[...])


def _combine(y, y_row, x2, gates, g_final):
    t, d = x2.shape
    tm = min(DMA_TILE, t)
    n_steps = t // tm
    rows_spec = lambda f: pl.BlockSpec((TOP_K, tm), f, memory_space=pltpu.SMEM)
    return pl.pallas_call(
        functools.partial(_combine_kernel, tm=tm, n_steps=n_steps),
        grid=(n_steps,),
        in_specs=[rows_spec(lambda i: (0, i)),
                  rows_spec(lambda i: (0, jnp.minimum(i + 1, n_steps - 1))),
                  pl.BlockSpec((tm, d), lambda i: (i, 0)),
                  pl.BlockSpec((tm, V7X_LANES), lambda i: (i, 0)),
                  pl.BlockSpec((1, d), lambda i: (0, 0)),
                  pl.BlockSpec(memory_space=pl.ANY)],
        out_specs=pl.BlockSpec((tm, d), lambda i: (i, 0)),
        out_shape=jax.ShapeDtypeStruct((t, d), F32),
        scratch_shapes=[pltpu.VMEM((2, TOP_K, tm * ROW_SUBLANES, V7X_LANES), F32), pltpu.SemaphoreType.DMA((2,))],
        compiler_params=_cparams("arbitrary"),
        name="combine",
    )(y_row, y_row, x2, gates, g_final, y)


def _alibi_slopes():
    n = 2 * N_HEADS
    s = 2.0 ** (-8.0 * (jnp.arange(n, dtype=F32) + 1.0) / n)
    return s[0::2], s[1::2]


def kernel(x, mem, attn_norm_g, mem_norm_g, w_in, diff_lambda_qk, diff_subln_g, w_mem_kv, w_out,
           ffn_norm_g, w_router, b_router, w_gate_up, b_gate_up, w_down, b_down, final_norm_g):
    b, s, d = x.shape
    t = b * s
    assert w_in.shape[0] == 1, "the final norm is fused into the layer's combine step: one layer only"
    assert s % (DIL_DILATIONS[-1] * DIL_TILE) == 0 and t % TOK_TILE == 0
    n_exp = w_router.shape[-1]
    diff_slopes, dil_slopes = _alibi_slopes()
    x2d = x.reshape(t, d)

    for l in range(1):
        lam_init = 0.8 - 0.6 * math.exp(-0.3 * l)
        mk, mv = _mem_kv(mem, mem_norm_g[l][None], w_mem_kv[l].astype(BF16))
        qd, kd, vd, ql, kl, vl, mq = _in_proj(x2d, attn_norm_g[l][None], w_in[l])
        sh = lambda a: a.reshape(b, s, a.shape[-1])
        o_diff = _diff_attention(sh(qd), sh(kd), sh(vd), diff_slopes, diff_lambda_qk[l],
                                 diff_subln_g[l][None], lam_init)
        o_dil = _dil_attention(sh(ql), sh(kl), sh(vl), dil_slopes)
        o_mem = _mem_attention(sh(mq), mk, mv)

        w_router_f = jnp.zeros((d, V7X_LANES), F32).at[:, :n_exp].set(w_router[l])
        w_router_hi = w_router_f.astype(BF16)
        w_router_p = jnp.stack([w_router_hi, (w_router_f - w_router_hi.astype(F32)).astype(BF16)])
        b_router_p = jnp.full((1, V7X_LANES), NEG, F32).at[0, :n_exp].set(b_router[l])
        x2, h2, route, gates, cnt = _out_router(
            x2d, o_diff.reshape(t, DIFF_WIDTH), o_dil.reshape(t, GROUP_WIDTH), o_mem.reshape(t, GROUP_WIDTH),
            w_out[l].astype(BF16), ffn_norm_g[l][None], w_router_p, b_router_p, n_exp)

        counts = cnt[:, 0]
        starts = jnp.cumsum(counts) - counts
        onehot = route[:TOP_K, :, None] == jnp.arange(n_exp, dtype=I32)[None, None, :]
        pick = lambda v: jnp.sum(jnp.where(onehot, v[None, None, :], 0), axis=-1)
        dest = (route[TOP_K:] + pick(starts)).astype(I32)

        xs = _dispatch(h2, dest)

        f = w_down.shape[2]
        tm_rows = min(ROW_TILE, t * TOP_K)
        w_gate, w_up = _split_gate_up(w_gate_up[l])
        bgu = b_gate_up[l].reshape(n_exp, 1, f, 2)
        meta, first_tile, item_start = _expert_work_items(counts, t * TOP_K, tm_rows)
        y = _experts(xs, meta, w_gate, w_up, bgu[..., 0], bgu[..., 1], w_down[l], b_down[l][:, None, :])

        item = pick(item_start) + dest // tm_rows - pick(first_tile)
        y_row = (item * tm_rows + dest % tm_rows).astype(I32)
        out = _combine(y, y_row, x2, gates, final_norm_g[None])
    return out.reshape(b, s, d)
```

```python
import functools
import math

import jax
import jax.numpy as jnp
from jax import lax
from jax.experimental import pallas as pl
from jax.experimental.pallas import tpu as pltpu

F32 = jnp.float32
BF16 = jnp.bfloat16
I32 = jnp.int32

NORM_EPS = 1e-5
HEAD_DIM = 64
N_HEADS = 4
DIFF_VDIM = 2 * HEAD_DIM
DIFF_WIDTH = N_HEADS * DIFF_VDIM
GROUP_WIDTH = N_HEADS * HEAD_DIM
DIL_DILATIONS = (1, 4, 16)
DIL_WINDOW_KEYS = 128
TOP_K = 4
SWIGLU_ALPHA = 1.702
SWIGLU_LIMIT = 7.0
NEG = -1e30

V7X_LANES = 128
V7X_SUBLANES = 8
V7X_MXU_DIM = 256
V7X_VMEM_LIMIT = 56 * 1024 * 1024

TOK_TILE = 512
ATT_TILE = 512
POS_SPLIT = 64
DIFF_HEADS_PER_STEP = 1
DIL_TILE = DIL_WINDOW_KEYS
DIL_CHUNK = 256
DIL_UNROLL = 8
ROW_TILE = 512
ROW_SUBLANES = V7X_SUBLANES
DMA_TILE = 256
DISPATCH_TILE = 1024
DMA_UNROLL = 4


def _cparams(*sem):
    return pltpu.CompilerParams(dimension_semantics=tuple(sem), vmem_limit_bytes=V7X_VMEM_LIMIT)


def _rms(x, g):
    return x * lax.rsqrt(jnp.mean(x * x, axis=-1, keepdims=True) + NORM_EPS) * g


def _dot_nt(a, b):
    return lax.dot_general(a, b, (((1,), (1,)), ((), ())), preferred_element_type=F32)


def _dot(a, b):
    return jnp.dot(a, b, preferred_element_type=F32)


def _memkv_kernel(mem_ref, g_ref, w_ref, mk_ref, mv_ref):
    h = _rms(mem_ref[0], g_ref[...]).astype(BF16)
    kv = _dot(h, w_ref[...])
    mk_ref[0] = kv[:, :GROUP_WIDTH].astype(BF16)
    mv_ref[0] = kv[:, GROUP_WIDTH:].astype(BF16)


def _mem_kv(mem, g, w):
    b, m, d = mem.shape
    out = jax.ShapeDtypeStruct((b, m, GROUP_WIDTH), BF16)
    return pl.pallas_call(
        _memkv_kernel,
        grid=(b,),
        in_specs=[pl.BlockSpec((1, m, d), lambda i: (i, 0, 0)),
                  pl.BlockSpec((1, d), lambda i: (0, 0)),
                  pl.BlockSpec((d, 2 * GROUP_WIDTH), lambda i: (0, 0))],
        out_specs=[pl.BlockSpec((1, m, GROUP_WIDTH), lambda i: (i, 0, 0))] * 2,
        out_shape=[out, out],
        compiler_params=_cparams("arbitrary"),
        name="mem_kv",
    )(mem, g, w)


_IN_SEGMENTS = ((DIFF_WIDTH, HEAD_DIM ** -0.5), (DIFF_WIDTH, 1.0), (DIFF_WIDTH, 1.0),
                (GROUP_WIDTH, HEAD_DIM ** -0.5), (GROUP_WIDTH, 1.0), (GROUP_WIDTH, 1.0),
                (GROUP_WIDTH, HEAD_DIM ** -0.5))


def _inproj_kernel(x_ref, g_ref, w_ref, *refs):
    out_refs, w_bf16 = refs[:-1], refs[-1]

    @pl.when(pl.program_id(0) == 0)
    def _():
        w_bf16[...] = w_ref[...].astype(BF16)

    h = _rms(x_ref[...], g_ref[...]).astype(BF16)
    c0 = 0
    for ref, (n, scale) in zip(out_refs, _IN_SEGMENTS):
        p = _dot(h, w_bf16[:, c0:c0 + n])
        if scale != 1.0:
            p = p * scale
        ref[...] = p.astype(BF16)
        c0 += n


def _in_proj(x2d, g, w):
    t, d = x2d.shape
    tm = min(TOK_TILE, t)
    return pl.pallas_call(
        _inproj_kernel,
        grid=(t // tm,),
        in_specs=[pl.BlockSpec((tm, d), lambda i: (i, 0)),
                  pl.BlockSpec((1, d), lambda i: (0, 0)),
                  pl.BlockSpec(w.shape, lambda i: (0, 0))],
        out_specs=[pl.BlockSpec((tm, n), lambda i: (i, 0)) for n, _ in _IN_SEGMENTS],
        out_shape=[jax.ShapeDtypeStruct((t, n), BF16) for n, _ in _IN_SEGMENTS],
        scratch_shapes=[pltpu.VMEM(w.shape, BF16)],
        compiler_params=_cparams("arbitrary"),
        name="in_proj",
    )(x2d, g, w)


def _diff_kernel(slopes_ref, lam_ref, g_ref, q_ref, k_ref, v_ref, kpos_ref, ones_ref, o_ref,
                 m_ref, a_ref, s_ref, *, lam_init, tile):
    n_full = 2 * pl.program_id(2)
    lane = lax.broadcasted_iota(I32, (tile, DIFF_VDIM), 1)
    heads = range(DIFF_HEADS_PER_STEP)
    qs = []
    for hd in heads:
        slope = slopes_ref[DIFF_HEADS_PER_STEP * pl.program_id(1) + hd]
        qpos = jnp.where(lane == 0, slope * float(POS_SPLIT), jnp.where(lane == 1, slope, 0.0)).astype(BF16)
        for sub in range(2):
            q = q_ref[0, sub * tile:(sub + 1) * tile, hd * DIFF_VDIM:(hd + 1) * DIFF_VDIM]
            zero = jnp.zeros_like(q)
            qs.append(jnp.concatenate([jnp.where(lane < HEAD_DIM, q, zero), qpos], axis=1))
            qs.append(jnp.concatenate([jnp.where(lane >= HEAD_DIM, q, zero), qpos], axis=1))
    chains = lambda subs: [(hd, 4 * hd + 2 * sub + mp) for hd in heads for sub in subs for mp in range(2)]
    sub_a, sub_b, both = chains((0,)), chains((1,)), chains((0, 1))

    m_ref[...] = jnp.full(m_ref.shape, NEG, F32)
    a_ref[...] = jnp.zeros(a_ref.shape, F32)

    def scores(j, slot, which):
        start = pl.multiple_of(j * tile, tile)
        kk = [jnp.concatenate([k_ref[0, pl.ds(start, tile), hd * DIFF_VDIM:(hd + 1) * DIFF_VDIM],
                               kpos_ref[pl.ds(start, tile), :]], axis=1) for hd in heads]
        for hd, c in which:
            s_ref[slot, c] = _dot_nt(qs[c], kk[hd])

    def consume(j, slot, which, diagonal):
        start = pl.multiple_of(j * tile, tile)
        vv = [jnp.concatenate([v_ref[0, pl.ds(start, tile), hd * DIFF_VDIM:(hd + 1) * DIFF_VDIM], ones_ref[...]],
                              axis=1) for hd in heads]
        for hd, c in which:
            s = s_ref[slot, c]
            if diagonal:
                row = lax.broadcasted_iota(I32, s.shape, 0)
                col = lax.broadcasted_iota(I32, s.shape, 1)
                s = jnp.where(col <= row, s, NEG)
            m_old = m_ref[c]
            m_new = jnp.maximum(m_old, jnp.max(s, axis=1, keepdims=True))
            p = jnp.exp(s - m_new)
            a_ref[c] = jnp.exp(m_old - m_new) * a_ref[c] + _dot(p.astype(BF16), vv[hd])
            m_ref[c] = m_new

    scores(0, 0, both)

    def pair(p, carry):
        j = 2 * p
        scores(j + 1, 1, both)
        consume(j, 0, both, False)
        scores(j + 2, 0, both)
        consume(j + 1, 1, both, False)
        return carry

    lax.fori_loop(0, n_full // 2, pair, 0)
    scores(n_full + 1, 1, sub_b)
    consume(n_full, 0, sub_a, True)
    consume(n_full, 0, sub_b, False)
    consume(n_full + 1, 1, sub_b, True)

    lv = lam_ref[...]
    lam = (jnp.exp(jnp.sum(lv[0:1] * lv[1:2], axis=1, keepdims=True))
           - jnp.exp(jnp.sum(lv[2:3] * lv[3:4], axis=1, keepdims=True)) + lam_init)
    n = DIFF_VDIM
    for hd in heads:
        for sub in range(2):
            a1, a2 = a_ref[4 * hd + 2 * sub], a_ref[4 * hd + 2 * sub + 1]
            o = (a1[:, :n] * (1.0 / a1[:, n:n + 1])) - lam * (a2[:, :n] * (1.0 / a2[:, n:n + 1]))
            o = _rms(o, g_ref[...]) * (1.0 - lam_init)
            o_ref[0, sub * tile:(sub + 1) * tile, hd * n:(hd + 1) * n] = o.astype(o_ref.dtype)


def _diff_attention(qd, kd, vd, slopes, lam_qk, subln_g, lam_init):
    b, s, _ = qd.shape
    tile = ATT_TILE
    assert s % (2 * tile) == 0
    pos = jnp.arange(s, dtype=I32)[:, None]
    lane = jnp.arange(V7X_LANES, dtype=I32)[None, :]
    kpos = jnp.where(lane == 0, pos // POS_SPLIT, jnp.where(lane == 1, pos % POS_SPLIT, 0)).astype(BF16)
    ones = jnp.broadcast_to(lane == 0, (tile, V7X_LANES)).astype(BF16)
    vec = lambda shape: pltpu.VMEM(shape, F32)
    const = lambda a: pl.BlockSpec(a.shape, lambda bb, hh, ii: (0, 0))
    width = DIFF_HEADS_PER_STEP * DIFF_VDIM
    n_chains = 4 * DIFF_HEADS_PER_STEP
    return pl.pallas_call(
        functools.partial(_diff_kernel, lam_init=lam_init, tile=tile),
        grid=(b, N_HEADS // DIFF_HEADS_PER_STEP, s // (2 * tile)),
        in_specs=[pl.BlockSpec(memory_space=pltpu.SMEM), const(lam_qk), const(subln_g),
                  pl.BlockSpec((1, 2 * tile, width), lambda bb, hh, ii: (bb, ii, hh)),
                  pl.BlockSpec((1, s, width), lambda bb, hh, ii: (bb, 0, hh)),
                  pl.BlockSpec((1, s, width), lambda bb, hh, ii: (bb, 0, hh)),
                  const(kpos), const(ones)],
        out_specs=pl.BlockSpec((1, 2 * tile, width), lambda bb, hh, ii: (bb, ii, hh)),
        out_shape=jax.ShapeDtypeStruct((b, s, DIFF_WIDTH), BF16),
        scratch_shapes=[vec((n_chains, tile, 1)), vec((n_chains, tile, 2 * DIFF_VDIM)),
                        vec((2, n_chains, tile, tile))],
        compiler_params=_cparams("arbitrary", "arbitrary", "arbitrary"),
        name="diff_attn",
    )(slopes, lam_qk, subln_g, qd, kd, vd, kpos, ones)


def _head_mask(lane, h):
    return (lane >= h * HEAD_DIM) & (lane < (h + 1) * HEAD_DIM)


def _split3(x):
    hi = x.astype(BF16)
    rem = x - hi.astype(F32)
    mid = rem.astype(BF16)
    return hi, mid, (rem - mid.astype(F32)).astype(BF16)


def _dil_kernel(q_ref, k_ref, v_ref, perm_ref, bias_ref, spread_ref, o_ref, qc, kc, vc, res_o, res_l, *, seq):
    tile, chunk = DIL_TILE, DIL_CHUNK
    n_blocks, n_chunks = seq // tile, seq // chunk
    lane = lax.broadcasted_iota(I32, (tile, GROUP_WIDTH), 1)
    tail_lane = lax.broadcasted_iota(I32, (tile, V7X_LANES), 1)

    def to_class_order(bi, r):
        n, sr = chunk // r, seq // r

        def one_chunk(t, carry):
            src0 = pl.multiple_of(t * chunk, chunk)
            for src, dst in ((q_ref, qc), (k_ref, kc), (v_ref, vc)):
                y = _dot(perm_ref[bi - 1, 0], src[0, pl.ds(src0, chunk), :]).astype(BF16)
                for rho in range(r):
                    dst[pl.ds(pl.multiple_of(rho * sr + t * n, n), n), :] = y[rho * n:(rho + 1) * n, :]
            return carry

        lax.fori_loop(0, n_chunks, one_chunk, 0, unroll=4)

    def attend(bi, r, load):
        blocks_per_class = (seq // r) // tile

        def one_block(g, in_class):
            cur = pl.multiple_of(g * tile, tile)
            prev = pl.multiple_of(jnp.maximum(g - 1, 0) * tile, tile)
            q = load(0, cur)
            qs = jnp.concatenate([jnp.where(_head_mask(lane, h), q, jnp.zeros_like(q)) for h in range(N_HEADS)], axis=0)
            kk = jnp.concatenate([load(1, prev), load(1, cur)], axis=0)
            vv = jnp.concatenate([load(2, prev), load(2, cur)], axis=0)
            if in_class is None:
                bias = jnp.where((g % blocks_per_class) != 0, bias_ref[bi, 1], bias_ref[bi, 0])
            else:
                bias = bias_ref[bi, 1 if in_class else 0]
            s = _dot_nt(qs, kk) + bias
            m = jnp.max(s, axis=1, keepdims=True)
            p = jnp.exp(s - m)
            l = jnp.sum(p, axis=1, keepdims=True)
            pv = _dot(p.astype(BF16), vv) * (1.0 / l)
            lse = m + jnp.log(l)
            out = jnp.zeros((tile, GROUP_WIDTH), F32)
            tail = jnp.zeros((tile, V7X_LANES), F32)
            for h in range(N_HEADS):
                rows = slice(h * tile, (h + 1) * tile)
                out = jnp.where(_head_mask(lane, h), pv[rows], out)
                tail = jnp.where(tail_lane == h, lse[rows], tail)
            res_o[bi, pl.ds(cur, tile), :] = out.astype(BF16)
            res_l[bi, pl.ds(cur, tile), :] = tail

        def block_group(outer, carry):
            for u in range(DIL_UNROLL):
                if blocks_per_class <= DIL_UNROLL:
                    in_class = u % blocks_per_class
                else:
                    in_class = u if u else None
                one_block(outer * DIL_UNROLL + u, in_class)
            return carry

        assert n_blocks % DIL_UNROLL == 0 and max(blocks_per_class, DIL_UNROLL) % min(blocks_per_class, DIL_UNROLL) == 0
        lax.fori_loop(0, n_blocks // DIL_UNROLL, block_group, 0)

    refs = (q_ref, k_ref, v_ref)
    scratch = (qc, kc, vc)
    attend(0, 1, lambda which, start: refs[which][0, pl.ds(start, tile), :])
    for bi, r in enumerate(DIL_DILATIONS[1:], start=1):
        to_class_order(bi, r)
        attend(bi, r, lambda which, start: scratch[which][pl.ds(start, tile), :])

    def merge_chunk(t, carry):
        base = pl.multiple_of(t * chunk, chunk)
        def pack3(lse):
            hi, mid, lo = _split3(lse)
            return (hi.astype(F32) + pltpu.roll(mid.astype(F32), N_HEADS, 1)
                    + pltpu.roll(lo.astype(F32), 2 * N_HEADS, 1)).astype(BF16)

        outs, lses = [], []
        for bi, r in enumerate(DIL_DILATIONS):
            if r == 1:
                out = res_o[0, pl.ds(base, chunk), :].astype(F32)
                packed = pack3(res_l[0, pl.ds(base, chunk), :])
            else:
                n, sr = chunk // r, seq // r
                rows = [pl.ds(pl.multiple_of(rho * sr + t * n, n), n) for rho in range(r)]
                back = perm_ref[bi - 1, 1]
                out = _dot(back, jnp.concatenate([res_o[bi, rw, :] for rw in rows], axis=0))
                packed = _dot(back, pack3(jnp.concatenate([res_l[bi, rw, :] for rw in rows], axis=0))).astype(BF16)
            outs.append(out)
            lses.append(_dot(packed, spread_ref[...]))
        big = jnp.maximum(jnp.maximum(lses[0], lses[1]), lses[2])
        wts = [jnp.exp(z - big) for z in lses]
        o = (wts[0] * outs[0] + wts[1] * outs[1] + wts[2] * outs[2]) * (1.0 / (wts[0] + wts[1] + wts[2]))
        o_ref[0, pl.ds(base, chunk), :] = o.astype(o_ref.dtype)
        return carry

    lax.fori_loop(0, n_chunks, merge_chunk, 0, unroll=4)


def _dil_constants(slopes, seq):
    chunk, tile = DIL_CHUNK, DIL_TILE
    perms = []
    for r in DIL_DILATIONS[1:]:
        n = chunk // r
        dst = jnp.arange(chunk, dtype=I32)
        src = r * (dst % n) + dst // n
        fwd = (src[:, None] == jnp.arange(chunk, dtype=I32)[None, :]).astype(BF16)
        perms.append(jnp.stack([fwd, fwd.T]))
    m = jnp.arange(tile, dtype=I32)[:, None]
    j = jnp.arange(2 * tile, dtype=I32)[None, :]
    dist = jnp.where(j < tile, m - j + tile, m - (j - tile))
    ok_cur = (j >= tile) & (dist >= 0)
    ok_prev = (j < tile) & (dist <= DIL_WINDOW_KEYS)
    biases = []
    for r in DIL_DILATIONS:
        per_head = [-(slopes[h] * r) * dist.astype(F32) for h in range(N_HEADS)]
        with_prev = jnp.concatenate([jnp.where(ok_cur | ok_prev, b, NEG) for b in per_head], axis=0)
        no_prev = jnp.concatenate([jnp.where(ok_cur, b, NEG) for b in per_head], axis=0)
        biases.append(jnp.stack([no_prev, with_prev]))
    src_lane = jnp.arange(V7X_LANES, dtype=I32)[:, None]
    spread = ((src_lane < 3 * N_HEADS)
              & (src_lane % N_HEADS == jnp.arange(GROUP_WIDTH, dtype=I32)[None, :] // HEAD_DIM)).astype(BF16)
    return jnp.stack(perms), jnp.stack(biases), spread


def _dil_attention(ql, kl, vl, slopes):
    b, s, w = ql.shape
    perms, biases, spread = _dil_constants(slopes, s)
    row = pl.BlockSpec((1, s, w), lambda i: (i, 0, 0))
    const = lambda a: pl.BlockSpec(a.shape, lambda i: (0,) * a.ndim)
    return pl.pallas_call(
        functools.partial(_dil_kernel, seq=s),
        grid=(b,),
        in_specs=[row, row, row, const(perms), const(biases), const(spread)],
        out_specs=row,
        out_shape=jax.ShapeDtypeStruct((b, s, w), BF16),
        scratch_shapes=[pltpu.VMEM((s, w), BF16)] * 3
                       + [pltpu.VMEM((len(DIL_DILATIONS), s, w), BF16),
                          pltpu.VMEM((len(DIL_DILATIONS), s, V7X_LANES), F32)],
        compiler_params=_cparams("arbitrary"),
        name="dil_attn",
    )(ql, kl, vl, perms, biases, spread)


def _memattn_kernel(q_ref, mk_ref, mv_ref, o_ref):
    q = q_ref[0]
    tile = q.shape[0]
    lane = lax.broadcasted_iota(I32, q.shape, 1)
    qs = jnp.concatenate([jnp.where(_head_mask(lane, h), q, jnp.zeros_like(q)) for h in range(N_HEADS)], axis=0)
    s = _dot_nt(qs, mk_ref[0])
    m = jnp.max(s, axis=1, keepdims=True)
    p = jnp.exp(s - m)
    l = jnp.sum(p, axis=1, keepdims=True)
    pv = _dot(p.astype(BF16), mv_ref[0]) * (1.0 / l)
    out = jnp.zeros(q.shape, F32)
    for h in range(N_HEADS):
        out = jnp.where(_head_mask(lane, h), pv[h * tile:(h + 1) * tile], out)
    o_ref[0] = out.astype(o_ref.dtype)


def _mem_attention(mq, mk, mv):
    b, s, w = mq.shape
    m = mk.shape[1]
    tile = min(TOK_TILE, s)
    return pl.pallas_call(
        _memattn_kernel,
        grid=(b, s // tile),
        in_specs=[pl.BlockSpec((1, tile, w), lambda bb, ii: (bb, ii, 0)),
                  pl.BlockSpec((1, m, w), lambda bb, ii: (bb, 0, 0)),
                  pl.BlockSpec((1, m, w), lambda bb, ii: (bb, 0, 0))],
        out_specs=pl.BlockSpec((1, tile, w), lambda bb, ii: (bb, ii, 0)),
        out_shape=jax.ShapeDtypeStruct((b, s, w), BF16),
        compiler_params=_cparams("arbitrary", "arbitrary"),
        name="mem_attn",
    )(mq, mk, mv)


def _outrouter_kernel(x_ref, od_ref, ol_ref, om_ref, wo_ref, g_ref, wr_ref, br_ref,
                      x2_ref, h2_ref, route_ref, gate_ref, cnt_ref, run_ref, *, tm, n_exp):
    step = pl.program_id(0)

    @pl.when(step == 0)
    def _():
        run_ref[...] = jnp.zeros(run_ref.shape, F32)

    c1, c2 = DIFF_WIDTH, DIFF_WIDTH + GROUP_WIDTH
    mix = (_dot(od_ref[...], wo_ref[0:c1, :]) + _dot(ol_ref[...], wo_ref[c1:c2, :])
           + _dot(om_ref[...], wo_ref[c2:, :]))
    x2 = x_ref[...] + mix
    x2_ref[...] = x2
    h2 = _rms(x2, g_ref[...])
    h2_ref[...] = h2

    h_hi = h2.astype(BF16)
    h_lo = (h2 - h_hi.astype(F32)).astype(BF16)
    logits = (_dot(h_hi, wr_ref[0]) + (_dot(h_hi, wr_ref[1]) + _dot(h_lo, wr_ref[0]))) + br_ref[...]
    work = logits.T[:n_exp]
    erow = lax.broadcasted_iota(I32, work.shape, 0)
    erow_rev = (n_exp - 1 - erow).astype(F32)
    sel = jnp.zeros(work.shape, F32)
    vals, idxs = [], []
    for _ in range(TOP_K):
        mx = jnp.max(work, axis=0, keepdims=True)
        key = jnp.max(jnp.where(work == mx, erow_rev, -1.0), axis=0, keepdims=True)
        hit = erow_rev == key
        sel = jnp.where(hit, 1.0, sel)
        work = jnp.where(hit, -jnp.inf, work)
        vals.append(mx)
        idxs.append(n_exp - 1 - key.astype(I32))

    r_io = lax.broadcasted_iota(I32, (tm, tm), 0)
    c_io = lax.broadcasted_iota(I32, (tm, tm), 1)
    earlier = jnp.where(r_io < c_io, 1.0, 0.0).astype(BF16)
    before = _dot(sel.astype(BF16), earlier) + run_ref[...]
    run_ref[...] = run_ref[...] + jnp.sum(sel, axis=1, keepdims=True)
    cnt_ref[...] = jnp.broadcast_to(run_ref[...], cnt_ref.shape).astype(I32)

    exps = [jnp.exp(v - vals[0]) for v in vals]
    inv = 1.0 / (exps[0] + exps[1] + exps[2] + exps[3])
    krow = lax.broadcasted_iota(I32, (2 * TOP_K, tm), 0)
    route = jnp.zeros((2 * TOP_K, tm), I32)
    gate_t = jnp.zeros((V7X_LANES, tm), F32)
    grow = lax.broadcasted_iota(I32, gate_t.shape, 0)
    for k in range(TOP_K):
        rk = jnp.sum(jnp.where(erow == idxs[k], before, 0.0), axis=0, keepdims=True)
        route = jnp.where(krow == k, idxs[k], route)
        route = jnp.where(krow == TOP_K + k, rk.astype(I32), route)
        gate_t = jnp.where(grow == k, exps[k] * inv, gate_t)
    route_ref[...] = route
    gate_ref[...] = gate_t.T


def _out_router(x2d, o_diff, o_dil, o_mem, w_out, g, w_router_p, b_router_p, n_exp):
    t, d = x2d.shape
    tm = min(TOK_TILE, t)
    row = lambda n: pl.BlockSpec((tm, n), lambda i: (i, 0))
    full = lambda a: pl.BlockSpec(a.shape, lambda i: (0,) * a.ndim)
    return pl.pallas_call(
        functools.partial(_outrouter_kernel, tm=tm, n_exp=n_exp),
        grid=(t // tm,),
        in_specs=[row(d), row(DIFF_WIDTH), row(GROUP_WIDTH), row(GROUP_WIDTH),
                  full(w_out), full(g), full(w_router_p), full(b_router_p)],
        out_specs=[row(d), row(d), pl.BlockSpec((2 * TOP_K, tm), lambda i: (0, i)), row(V7X_LANES),
                   pl.BlockSpec((n_exp, V7X_LANES), lambda i: (0, 0))],
        out_shape=[jax.ShapeDtypeStruct((t, d), F32), jax.ShapeDtypeStruct((t, d), F32),
                   jax.ShapeDtypeStruct((2 * TOP_K, t), I32), jax.ShapeDtypeStruct((t, V7X_LANES), F32),
                   jax.ShapeDtypeStruct((n_exp, V7X_LANES), I32)],
        scratch_shapes=[pltpu.VMEM((n_exp, 1), F32)],
        compiler_params=_cparams("arbitrary"),
        name="out_router",
    )(x2d, o_diff, o_dil, o_mem, w_out, g, w_router_p, b_router_p)


def _row_copy(src, dst, sem):
    return pltpu.make_async_copy(src, dst, sem)


def _dispatch_kernel(dest_ref, h_ref, xs_hbm, sem, *, tm):
    def issue(t, carry):
        for k in range(TOP_K):
            d = dest_ref[k, t]
            _row_copy(h_ref.at[pl.ds(t, 1), :], xs_hbm.at[pl.ds(d, 1), :], sem).start()
        return carry

    lax.fori_loop(0, tm, issue, 0, unroll=DMA_UNROLL)
    for _ in range(TOP_K):
        _row_copy(h_ref, xs_hbm.at[pl.ds(0, tm), :], sem).wait()


def _dispatch(h2, dest):
    t, d = h2.shape
    tm = min(DISPATCH_TILE, t)
    return pl.pallas_call(
        functools.partial(_dispatch_kernel, tm=tm),
        grid=(t // tm,),
        in_specs=[pl.BlockSpec((TOP_K, tm), lambda i: (0, i), memory_space=pltpu.SMEM),
                  pl.BlockSpec((tm, d), lambda i: (i, 0))],
        out_specs=pl.BlockSpec(memory_space=pl.ANY),
        out_shape=jax.ShapeDtypeStruct((t * TOP_K, d), h2.dtype),
        scratch_shapes=[pltpu.SemaphoreType.DMA],
        compiler_params=_cparams("arbitrary"),
        name="dispatch",
    )(dest, h2)


def _split_gate_up_kernel(w_ref, sel_ref, wg_ref, wu_ref):
    x = w_ref[0].astype(BF16)
    k, n = sel_ref.shape[1], sel_ref.shape[2]
    for j in range(wg_ref.shape[2] // n):
        blk = x[:, j * k:(j + 1) * k]
        wg_ref[0, :, j * n:(j + 1) * n] = _dot(blk, sel_ref[0]).astype(BF16)
        wu_ref[0, :, j * n:(j + 1) * n] = _dot(blk, sel_ref[1]).astype(BF16)


def _split_gate_up(w):
    e, d, f2 = w.shape
    f = f2 // 2
    rows, n = min(TOK_TILE, d), V7X_MXU_DIM
    src = jnp.arange(2 * n, dtype=I32)[:, None]
    dst = jnp.arange(n, dtype=I32)[None, :]
    sel = jnp.stack([src == 2 * dst, src == 2 * dst + 1]).astype(BF16)
    out = jax.ShapeDtypeStruct((e, d, f), BF16)
    return pl.pallas_call(
        _split_gate_up_kernel,
        grid=(e, d // rows),
        in_specs=[pl.BlockSpec((1, rows, f2), lambda i, j: (i, j, 0)),
                  pl.BlockSpec(sel.shape, lambda i, j: (0, 0, 0))],
        out_specs=[pl.BlockSpec((1, rows, f), lambda i, j: (i, j, 0))] * 2,
        out_shape=[out, out],
        compiler_params=_cparams("arbitrary", "arbitrary"),
        name="split_gate_up",
    )(w, sel)


def _experts_kernel(tile_ref, exp_ref, lo_ref, hi_ref, act_ref, fresh_ref,
                    xs_ref, wg_ref, wu_ref, bg_ref, bu_ref, wd_ref, bd_ref, y_ref, wd_bf16, *, tm):
    w = pl.program_id(0)

    @pl.when(fresh_ref[w] == 1)
    def _():
        wd_bf16[...] = wd_ref[0].astype(BF16)

    lo, hi, half = lo_ref[w], hi_ref[w], tm // 2
    active = act_ref[w] == 1
    top_only = active & (hi <= half)
    bottom_only = active & (lo >= half)

    def mlp(r0, nr):
        x = xs_ref[r0:r0 + nr, :].astype(BF16)
        gate = jnp.minimum(_dot(x, wg_ref[0]) + bg_ref[0], SWIGLU_LIMIT)
        up = jnp.clip(_dot(x, wu_ref[0]) + bu_ref[0], -SWIGLU_LIMIT, SWIGLU_LIMIT)
        act = (up + 1.0) * (gate * jax.nn.sigmoid(gate * SWIGLU_ALPHA))
        y = _dot(act.astype(BF16), wd_bf16[...]) + bd_ref[0]
        row = r0 + lax.broadcasted_iota(I32, (nr, V7X_LANES), 0)
        mine = (row >= lo) & (row < hi)
        for s in range(ROW_SUBLANES):
            y_ref[pl.ds(r0 * ROW_SUBLANES + s, nr, stride=ROW_SUBLANES), :] = jnp.where(
                mine, y[:, s * V7X_LANES:(s + 1) * V7X_LANES], 0.0)

    def zero(r0, nr):
        y_ref[r0 * ROW_SUBLANES:(r0 + nr) * ROW_SUBLANES, :] = jnp.zeros((nr * ROW_SUBLANES, V7X_LANES), F32)

    @pl.when(active & jnp.logical_not(top_only | bottom_only))
    def _():
        mlp(0, tm)

    @pl.when(top_only)
    def _():
        mlp(0, half)
        zero(half, half)

    @pl.when(bottom_only)
    def _():
        zero(0, half)
        mlp(half, half)

    @pl.when(jnp.logical_not(active))
    def _():
        zero(0, tm)


def _experts(xs, meta, wg, wu, bg, bu, wd, bd):
    n, d = xs.shape
    e, _, f = wg.shape
    tm = min(ROW_TILE, n)
    n_items = meta[0].shape[0]
    xmap = lambda w, tile, ex, *_: (tile[w], 0)
    wmap = lambda w, tile, ex, *_: (ex[w], 0, 0)
    grid_spec = pltpu.PrefetchScalarGridSpec(
        num_scalar_prefetch=len(meta),
        grid=(n_items,),
        in_specs=[pl.BlockSpec((tm, d), xmap),
                  pl.BlockSpec((1, d, f), wmap), pl.BlockSpec((1, d, f), wmap),
                  pl.BlockSpec((1, 1, f), wmap), pl.BlockSpec((1, 1, f), wmap),
                  pl.BlockSpec((1, f, d), wmap), pl.BlockSpec((1, 1, d), wmap)],
        out_specs=pl.BlockSpec((tm * ROW_SUBLANES, V7X_LANES), lambda w, *_: (w, 0)),
        scratch_shapes=[pltpu.VMEM((f, d), BF16)],
    )
    assert d == ROW_SUBLANES * V7X_LANES
    return pl.pallas_call(
        functools.partial(_experts_kernel, tm=tm),
        grid_spec=grid_spec,
        out_shape=jax.ShapeDtypeStruct((n_items * tm * ROW_SUBLANES, V7X_LANES), F32),
        compiler_params=_cparams("arbitrary"),
        name="experts",
    )(*meta, xs, wg, wu, bg, bu, wd, bd)


def _expert_work_items(counts, n_rows, tm):
    e = counts.shape[0]
    n_tiles = n_rows // tm
    n_items = n_tiles + e - 1
    ends = jnp.cumsum(counts)
    starts = ends - counts
    first_tile = starts // tm
    n_t = jnp.where(counts > 0, (ends - 1) // tm - first_tile + 1, 0)
    item_end = jnp.cumsum(n_t)
    item_start = item_end - n_t
    total = item_end[-1]
    w = jnp.arange(n_items, dtype=I32)
    active = w < total
    w_eff = jnp.minimum(w, total - 1)
    ex = jnp.minimum(jnp.sum((item_end[None, :] <= w_eff[:, None]).astype(I32), axis=1), e - 1)
    mine = ex[:, None] == jnp.arange(e, dtype=I32)[None, :]
    pick = lambda v: jnp.sum(jnp.where(mine, v[None, :], 0), axis=1)
    tile = (pick(first_tile) + (w_eff - pick(item_start))).astype(I32)
    lo = jnp.clip(pick(starts) - tile * tm, 0, tm).astype(I32)
    hi = jnp.clip(pick(ends) - tile * tm, 0, tm).astype(I32)
    fresh = jnp.concatenate([jnp.ones((1,), I32), (ex[1:] != ex[:-1]).astype(I32)])
    return (tile, ex, lo, hi, active.astype(I32), fresh), first_tile, item_start


def _combine_kernel(dest_ref, dest_next_ref, x2_ref, gate_ref, g_ref, y_hbm, o_ref, ybuf, sems, *, tm, n_steps):
    i = pl.program_id(0)
    slot = i % 2

    def gather(dref, into):
        def issue(t, carry):
            for k in range(TOP_K):
                src = pl.multiple_of(dref[k, t] * ROW_SUBLANES, ROW_SUBLANES)
                dst = pl.multiple_of(t * ROW_SUBLANES, ROW_SUBLANES)
                _row_copy(y_hbm.at[pl.ds(src, ROW_SUBLANES), :], ybuf.at[into, k, pl.ds(dst, ROW_SUBLANES), :],
                          sems.at[into]).start()
            return carry
        lax.fori_loop(0, tm, issue, 0, unroll=DMA_UNROLL)

    @pl.when(i == 0)
    def _():
        gather(dest_ref, 0)

    @pl.when(i + 1 < n_steps)
    def _():
        gather(dest_next_ref, 1 - slot)

    for k in range(TOP_K):
        _row_copy(y_hbm.at[pl.ds(0, tm * ROW_SUBLANES), :], ybuf.at[slot, k], sems.at[slot]).wait()

    gates = gate_ref[...]
    cols = []
    for s in range(ROW_SUBLANES):
        acc = x2_ref[:, s * V7X_LANES:(s + 1) * V7X_LANES]
        for k in range(TOP_K):
            acc = acc + ybuf[slot, k, pl.ds(s, tm, stride=ROW_SUBLANES), :] * gates[:, k:k + 1]
        cols.append(acc)
    o_ref[...] = _rms(jnp.concatenate(cols, axis=1), g_ref[...])


def _combine(y, y_row, x2, gates, g_final):
    t, d = x2.shape
    tm = min(DMA_TILE, t)
    n_steps = t // tm
    rows_spec = lambda f: pl.BlockSpec((TOP_K, tm), f, memory_space=pltpu.SMEM)
    return pl.pallas_call(
        functools.partial(_combine_kernel, tm=tm, n_steps=n_steps),
        grid=(n_steps,),
        in_specs=[rows_spec(lambda i: (0, i)),
                  rows_spec(lambda i: (0, jnp.minimum(i + 1, n_steps - 1))),
                  pl.BlockSpec((tm, d), lambda i: (i, 0)),
                  pl.BlockSpec((tm, V7X_LANES), lambda i: (i, 0)),
                  pl.BlockSpec((1, d), lambda i: (0, 0)),
                  pl.BlockSpec(memory_space=pl.ANY)],
        out_specs=pl.BlockSpec((tm, d), lambda i: (i, 0)),
        out_shape=jax.ShapeDtypeStruct((t, d), F32),
        scratch_shapes=[pltpu.VMEM((2, TOP_K, tm * ROW_SUBLANES, V7X_LANES), F32), pltpu.SemaphoreType.DMA((2,))],
        compiler_params=_cparams("arbitrary"),
        name="combine",
    )(y_row, y_row, x2, gates, g_final, y)


def _alibi_slopes():
    n = 2 * N_HEADS
    s = 2.0 ** (-8.0 * (jnp.arange(n, dtype=F32) + 1.0) / n)
    return s[0::2], s[1::2]


def kernel(x, mem, attn_norm_g, mem_norm_g, w_in, diff_lambda_qk, diff_subln_g, w_mem_kv, w_out,
           ffn_norm_g, w_router, b_router, w_gate_up, b_gate_up, w_down, b_down, final_norm_g):
    b, s, d = x.shape
    t = b * s
    assert w_in.shape[0] == 1, "the final norm is fused into the layer's combine step: one layer only"
    assert s % (DIL_DILATIONS[-1] * DIL_TILE) == 0 and t % TOK_TILE == 0
    n_exp = w_router.shape[-1]
    diff_slopes, dil_slopes = _alibi_slopes()
    x2d = x.reshape(t, d)

    for l in range(1):
        lam_init = 0.8 - 0.6 * math.exp(-0.3 * l)
        mk, mv = _mem_kv(mem, mem_norm_g[l][None], w_mem_kv[l].astype(BF16))
        qd, kd, vd, ql, kl, vl, mq = _in_proj(x2d, attn_norm_g[l][None], w_in[l])
        sh = lambda a: a.reshape(b, s, a.shape[-1])
        o_diff = _diff_attention(sh(qd), sh(kd), sh(vd), diff_slopes, diff_lambda_qk[l],
                                 diff_subln_g[l][None], lam_init)
        o_dil = _dil_attention(sh(ql), sh(kl), sh(vl), dil_slopes)
        o_mem = _mem_attention(sh(mq), mk, mv)

        w_router_f = jnp.zeros((d, V7X_LANES), F32).at[:, :n_exp].set(w_router[l])
        w_router_hi = w_router_f.astype(BF16)
        w_router_p = jnp.stack([w_router_hi, (w_router_f - w_router_hi.astype(F32)).astype(BF16)])
        b_router_p = jnp.full((1, V7X_LANES), NEG, F32).at[0, :n_exp].set(b_router[l])
        x2, h2, route, gates, cnt = _out_router(
            x2d, o_diff.reshape(t, DIFF_WIDTH), o_dil.reshape(t, GROUP_WIDTH), o_mem.reshape(t, GROUP_WIDTH),
            w_out[l].astype(BF16), ffn_norm_g[l][None], w_router_p, b_router_p, n_exp)

        counts = cnt[:, 0]
        starts = jnp.cumsum(counts) - counts
        onehot = route[:TOP_K, :, None] == jnp.arange(n_exp, dtype=I32)[None, None, :]
        pick = lambda v: jnp.sum(jnp.where(onehot, v[None, None, :], 0), axis=-1)
        dest = (route[TOP_K:] + pick(starts)).astype(I32)

        xs = _dispatch(h2, dest)

        f = w_down.shape[2]
        tm_rows = min(ROW_TILE, t * TOP_K)
        w_gate, w_up = _split_gate_up(w_gate_up[l])
        bgu = b_gate_up[l].reshape(n_exp, 1, f, 2)
        meta, first_tile, item_start = _expert_work_items(counts, t * TOP_K, tm_rows)
        y = _experts(xs, meta, w_gate, w_up, bgu[..., 0], bgu[..., 1], w_down[l], b_down[l][:, None, :])

        item = pick(item_start) + dest // tm_rows - pick(first_tile)
        y_row = (item * tm_rows + dest % tm_rows).astype(I32)
        out = _combine(y, y_row, x2, gates, final_norm_g[None])
    return out.reshape(b, s, d)
```

```python
import functools
import math

import jax
import jax.numpy as jnp
from jax import lax
from jax.experimental import pallas as pl
from jax.experimental.pallas import tpu as pltpu

F32 = jnp.float32
BF16 = jnp.bfloat16
I32 = jnp.int32

NORM_EPS = 1e-5
HEAD_DIM = 64
N_HEADS = 4
DIFF_VDIM = 2 * HEAD_DIM
DIFF_WIDTH = N_HEADS * DIFF_VDIM
GROUP_WIDTH = N_HEADS * HEAD_DIM
DIL_DILATIONS = (1, 4, 16)
DIL_WINDOW_KEYS = 128
TOP_K = 4
SWIGLU_ALPHA = 1.702
SWIGLU_LIMIT = 7.0
NEG = -1e30

V7X_LANES = 128
V7X_SUBLANES = 8
V7X_MXU_DIM = 256
V7X_VMEM_LIMIT = 56 * 1024 * 1024

TOK_TILE = 512
ATT_TILE = 512
POS_SPLIT = 64
DIFF_HEADS_PER_STEP = 1
DIL_TILE = DIL_WINDOW_KEYS
DIL_CHUNK = 256
DIL_UNROLL = 16
ROW_TILE = 512
ROW_SUBLANES = V7X_SUBLANES
DMA_TILE = 256
DISPATCH_TILE = 1024
DMA_UNROLL = 4


def _cparams(*sem):
    return pltpu.CompilerParams(dimension_semantics=tuple(sem), vmem_limit_bytes=V7X_VMEM_LIMIT)


def _rms(x, g):
    return x * lax.rsqrt(jnp.mean(x * x, axis=-1, keepdims=True) + NORM_EPS) * g


def _dot_nt(a, b):
    return lax.dot_general(a, b, (((1,), (1,)), ((), ())), preferred_element_type=F32)


def _dot(a, b):
    return jnp.dot(a, b, preferred_element_type=F32)


def _memkv_kernel(mem_ref, g_ref, w_ref, mk_ref, mv_ref):
    h = _rms(mem_ref[0], g_ref[...]).astype(BF16)
    kv = _dot(h, w_ref[...])
    mk_ref[0] = kv[:, :GROUP_WIDTH].astype(BF16)
    mv_ref[0] = kv[:, GROUP_WIDTH:].astype(BF16)


def _mem_kv(mem, g, w):
    b, m, d = mem.shape
    out = jax.ShapeDtypeStruct((b, m, GROUP_WIDTH), BF16)
    return pl.pallas_call(
        _memkv_kernel,
        grid=(b,),
        in_specs=[pl.BlockSpec((1, m, d), lambda i: (i, 0, 0)),
                  pl.BlockSpec((1, d), lambda i: (0, 0)),
                  pl.BlockSpec((d, 2 * GROUP_WIDTH), lambda i: (0, 0))],
        out_specs=[pl.BlockSpec((1, m, GROUP_WIDTH), lambda i: (i, 0, 0))] * 2,
        out_shape=[out, out],
        compiler_params=_cparams("arbitrary"),
        name="mem_kv",
    )(mem, g, w)


_IN_SEGMENTS = ((DIFF_WIDTH, HEAD_DIM ** -0.5), (DIFF_WIDTH, 1.0), (DIFF_WIDTH, 1.0),
                (GROUP_WIDTH, HEAD_DIM ** -0.5), (GROUP_WIDTH, 1.0), (GROUP_WIDTH, 1.0),
                (GROUP_WIDTH, HEAD_DIM ** -0.5))


def _inproj_kernel(x_ref, g_ref, w_ref, *refs):
    out_refs, w_bf16 = refs[:-1], refs[-1]

    @pl.when(pl.program_id(0) == 0)
    def _():
        w_bf16[...] = w_ref[...].astype(BF16)

    h = _rms(x_ref[...], g_ref[...]).astype(BF16)
    c0 = 0
    for ref, (n, scale) in zip(out_refs, _IN_SEGMENTS):
        p = _dot(h, w_bf16[:, c0:c0 + n])
        if scale != 1.0:
            p = p * scale
        ref[...] = p.astype(BF16)
        c0 += n


def _in_proj(x2d, g, w):
    t, d = x2d.shape
    tm = min(TOK_TILE, t)
    return pl.pallas_call(
        _inproj_kernel,
        grid=(t // tm,),
        in_specs=[pl.BlockSpec((tm, d), lambda i: (i, 0)),
                  pl.BlockSpec((1, d), lambda i: (0, 0)),
                  pl.BlockSpec(w.shape, lambda i: (0, 0))],
        out_specs=[pl.BlockSpec((tm, n), lambda i: (i, 0)) for n, _ in _IN_SEGMENTS],
        out_shape=[jax.ShapeDtypeStruct((t, n), BF16) for n, _ in _IN_SEGMENTS],
        scratch_shapes=[pltpu.VMEM(w.shape, BF16)],
        compiler_params=_cparams("arbitrary"),
        name="in_proj",
    )(x2d, g, w)


def _diff_kernel(slopes_ref, lam_ref, g_ref, q_ref, k_ref, v_ref, kpos_ref, ones_ref, o_ref,
                 m_ref, a_ref, s_ref, *, lam_init, tile):
    n_full = 2 * pl.program_id(2)
    lane = lax.broadcasted_iota(I32, (tile, DIFF_VDIM), 1)
    heads = range(DIFF_HEADS_PER_STEP)
    qs = []
    for hd in heads:
        slope = slopes_ref[DIFF_HEADS_PER_STEP * pl.program_id(1) + hd]
        qpos = jnp.where(lane == 0, slope * float(POS_SPLIT), jnp.where(lane == 1, slope, 0.0)).astype(BF16)
        for sub in range(2):
            q = q_ref[0, sub * tile:(sub + 1) * tile, hd * DIFF_VDIM:(hd + 1) * DIFF_VDIM]
            zero = jnp.zeros_like(q)
            qs.append(jnp.concatenate([jnp.where(lane < HEAD_DIM, q, zero), qpos], axis=1))
            qs.append(jnp.concatenate([jnp.where(lane >= HEAD_DIM, q, zero), qpos], axis=1))
    chains = lambda subs: [(hd, 4 * hd + 2 * sub + mp) for hd in heads for sub in subs for mp in range(2)]
    sub_a, sub_b, both = chains((0,)), chains((1,)), chains((0, 1))

    m_ref[...] = jnp.full(m_ref.shape, NEG, F32)
    a_ref[...] = jnp.zeros(a_ref.shape, F32)

    def scores(j, slot, which):
        start = pl.multiple_of(j * tile, tile)
        kk = [jnp.concatenate([k_ref[0, pl.ds(start, tile), hd * DIFF_VDIM:(hd + 1) * DIFF_VDIM],
                               kpos_ref[pl.ds(start, tile), :]], axis=1) for hd in heads]
        for hd, c in which:
            s_ref[slot, c] = _dot_nt(qs[c], kk[hd])

    def consume(j, slot, which, diagonal):
        start = pl.multiple_of(j * tile, tile)
        vv = [jnp.concatenate([v_ref[0, pl.ds(start, tile), hd * DIFF_VDIM:(hd + 1) * DIFF_VDIM], ones_ref[...]],
                              axis=1) for hd in heads]
        for hd, c in which:
            s = s_ref[slot, c]
            if diagonal:
                row = lax.broadcasted_iota(I32, s.shape, 0)
                col = lax.broadcasted_iota(I32, s.shape, 1)
                s = jnp.where(col <= row, s, NEG)
            m_old = m_ref[c]
            m_new = jnp.maximum(m_old, jnp.max(s, axis=1, keepdims=True))
            p = jnp.exp(s - m_new)
            a_ref[c] = jnp.exp(m_old - m_new) * a_ref[c] + _dot(p.astype(BF16), vv[hd])
            m_ref[c] = m_new

    scores(0, 0, both)

    def pair(p, carry):
        j = 2 * p
        scores(j + 1, 1, both)
        consume(j, 0, both, False)
        scores(j + 2, 0, both)
        consume(j + 1, 1, both, False)
        return carry

    lax.fori_loop(0, n_full // 2, pair, 0)
    scores(n_full + 1, 1, sub_b)
    consume(n_full, 0, sub_a, True)
    consume(n_full, 0, sub_b, False)
    consume(n_full + 1, 1, sub_b, True)

    lv = lam_ref[...]
    lam = (jnp.exp(jnp.sum(lv[0:1] * lv[1:2], axis=1, keepdims=True))
           - jnp.exp(jnp.sum(lv[2:3] * lv[3:4], axis=1, keepdims=True)) + lam_init)
    n = DIFF_VDIM
    for hd in heads:
        for sub in range(2):
            a1, a2 = a_ref[4 * hd + 2 * sub], a_ref[4 * hd + 2 * sub + 1]
            o = (a1[:, :n] * (1.0 / a1[:, n:n + 1])) - lam * (a2[:, :n] * (1.0 / a2[:, n:n + 1]))
            o = _rms(o, g_ref[...]) * (1.0 - lam_init)
            o_ref[0, sub * tile:(sub + 1) * tile, hd * n:(hd + 1) * n] = o.astype(o_ref.dtype)


def _diff_attention(qd, kd, vd, slopes, lam_qk, subln_g, lam_init):
    b, s, _ = qd.shape
    tile = ATT_TILE
    assert s % (2 * tile) == 0
    pos = jnp.arange(s, dtype=I32)[:, None]
    lane = jnp.arange(V7X_LANES, dtype=I32)[None, :]
    kpos = jnp.where(lane == 0, pos // POS_SPLIT, jnp.where(lane == 1, pos % POS_SPLIT, 0)).astype(BF16)
    ones = jnp.broadcast_to(lane == 0, (tile, V7X_LANES)).astype(BF16)
    vec = lambda shape: pltpu.VMEM(shape, F32)
    const = lambda a: pl.BlockSpec(a.shape, lambda bb, hh, ii: (0, 0))
    width = DIFF_HEADS_PER_STEP * DIFF_VDIM
    n_chains = 4 * DIFF_HEADS_PER_STEP
    return pl.pallas_call(
        functools.partial(_diff_kernel, lam_init=lam_init, tile=tile),
        grid=(b, N_HEADS // DIFF_HEADS_PER_STEP, s // (2 * tile)),
        in_specs=[pl.BlockSpec(memory_space=pltpu.SMEM), const(lam_qk), const(subln_g),
                  pl.BlockSpec((1, 2 * tile, width), lambda bb, hh, ii: (bb, ii, hh)),
                  pl.BlockSpec((1, s, width), lambda bb, hh, ii: (bb, 0, hh)),
                  pl.BlockSpec((1, s, width), lambda bb, hh, ii: (bb, 0, hh)),
                  const(kpos), const(ones)],
        out_specs=pl.BlockSpec((1, 2 * tile, width), lambda bb, hh, ii: (bb, ii, hh)),
        out_shape=jax.ShapeDtypeStruct((b, s, DIFF_WIDTH), BF16),
        scratch_shapes=[vec((n_chains, tile, 1)), vec((n_chains, tile, 2 * DIFF_VDIM)),
                        vec((2, n_chains, tile, tile))],
        compiler_params=_cparams("arbitrary", "arbitrary", "arbitrary"),
        name="diff_attn",
    )(slopes, lam_qk, subln_g, qd, kd, vd, kpos, ones)


def _head_mask(lane, h):
    return (lane >= h * HEAD_DIM) & (lane < (h + 1) * HEAD_DIM)


def _split3(x):
    hi = x.astype(BF16)
    rem = x - hi.astype(F32)
    mid = rem.astype(BF16)
    return hi, mid, (rem - mid.astype(F32)).astype(BF16)


def _dil_kernel(q_ref, k_ref, v_ref, perm_ref, bias_ref, spread_ref, o_ref, qc, kc, vc, res_o, res_l, *, seq):
    tile, chunk = DIL_TILE, DIL_CHUNK
    n_blocks, n_chunks = seq // tile, seq // chunk
    lane = lax.broadcasted_iota(I32, (tile, GROUP_WIDTH), 1)
    tail_lane = lax.broadcasted_iota(I32, (tile, V7X_LANES), 1)

    def to_class_order(bi, r):
        n, sr = chunk // r, seq // r

        def one_chunk(t, carry):
            src0 = pl.multiple_of(t * chunk, chunk)
            for src, dst in ((q_ref, qc), (k_ref, kc), (v_ref, vc)):
                y = _dot(perm_ref[bi - 1, 0], src[0, pl.ds(src0, chunk), :]).astype(BF16)
                for rho in range(r):
                    dst[pl.ds(pl.multiple_of(rho * sr + t * n, n), n), :] = y[rho * n:(rho + 1) * n, :]
            return carry

        lax.fori_loop(0, n_chunks, one_chunk, 0, unroll=4)

    def attend(bi, r, load):
        blocks_per_class = (seq // r) // tile

        def one_block(g, in_class):
            cur = pl.multiple_of(g * tile, tile)
            prev = pl.multiple_of(jnp.maximum(g - 1, 0) * tile, tile)
            q = load(0, cur)
            qs = jnp.concatenate([jnp.where(_head_mask(lane, h), q, jnp.zeros_like(q)) for h in range(N_HEADS)], axis=0)
            kk = jnp.concatenate([load(1, prev), load(1, cur)], axis=0)
            vv = jnp.concatenate([load(2, prev), load(2, cur)], axis=0)
            if in_class is None:
                bias = jnp.where((g % blocks_per_class) != 0, bias_ref[bi, 1], bias_ref[bi, 0])
            else:
                bias = bias_ref[bi, 1 if in_class else 0]
            s = _dot_nt(qs, kk) + bias
            m = jnp.max(s, axis=1, keepdims=True)
            p = jnp.exp(s - m)
            l = jnp.sum(p, axis=1, keepdims=True)
            pv = _dot(p.astype(BF16), vv) * (1.0 / l)
            lse = m + jnp.log(l)
            out = jnp.zeros((tile, GROUP_WIDTH), F32)
            tail = jnp.zeros((tile, V7X_LANES), F32)
            for h in range(N_HEADS):
                rows = slice(h * tile, (h + 1) * tile)
                out = jnp.where(_head_mask(lane, h), pv[rows], out)
                tail = jnp.where(tail_lane == h, lse[rows], tail)
            res_o[bi, pl.ds(cur, tile), :] = out.astype(BF16)
            res_l[bi, pl.ds(cur, tile), :] = tail

        def block_group(outer, carry):
            for u in range(DIL_UNROLL):
                if blocks_per_class <= DIL_UNROLL:
                    in_class = u % blocks_per_class
                else:
                    in_class = u if u else None
                one_block(outer * DIL_UNROLL + u, in_class)
            return carry

        assert n_blocks % DIL_UNROLL == 0 and max(blocks_per_class, DIL_UNROLL) % min(blocks_per_class, DIL_UNROLL) == 0
        lax.fori_loop(0, n_blocks // DIL_UNROLL, block_group, 0)

    refs = (q_ref, k_ref, v_ref)
    scratch = (qc, kc, vc)
    attend(0, 1, lambda which, start: refs[which][0, pl.ds(start, tile), :])
    for bi, r in enumerate(DIL_DILATIONS[1:], start=1):
        to_class_order(bi, r)
        attend(bi, r, lambda which, start: scratch[which][pl.ds(start, tile), :])

    def merge_chunk(t, carry):
        base = pl.multiple_of(t * chunk, chunk)
        def pack3(lse):
            hi, mid, lo = _split3(lse)
            return (hi.astype(F32) + pltpu.roll(mid.astype(F32), N_HEADS, 1)
                    + pltpu.roll(lo.astype(F32), 2 * N_HEADS, 1)).astype(BF16)

        outs, lses = [], []
        for bi, r in enumerate(DIL_DILATIONS):
            if r == 1:
                out = res_o[0, pl.ds(base, chunk), :].astype(F32)
                packed = pack3(res_l[0, pl.ds(base, chunk), :])
            else:
                n, sr = chunk // r, seq // r
                rows = [pl.ds(pl.multiple_of(rho * sr + t * n, n), n) for rho in range(r)]
                back = perm_ref[bi - 1, 1]
                out = _dot(back, jnp.concatenate([res_o[bi, rw, :] for rw in rows], axis=0))
                packed = _dot(back, pack3(jnp.concatenate([res_l[bi, rw, :] for rw in rows], axis=0))).astype(BF16)
            outs.append(out)
            lses.append(_dot(packed, spread_ref[...]))
        big = jnp.maximum(jnp.maximum(lses[0], lses[1]), lses[2])
        wts = [jnp.exp(z - big) for z in lses]
        o = (wts[0] * outs[0] + wts[1] * outs[1] + wts[2] * outs[2]) * (1.0 / (wts[0] + wts[1] + wts[2]))
        o_ref[0, pl.ds(base, chunk), :] = o.astype(o_ref.dtype)
        return carry

    lax.fori_loop(0, n_chunks, merge_chunk, 0, unroll=4)


def _dil_constants(slopes, seq):
    chunk, tile = DIL_CHUNK, DIL_TILE
    perms = []
    for r in DIL_DILATIONS[1:]:
        n = chunk // r
        dst = jnp.arange(chunk, dtype=I32)
        src = r * (dst % n) + dst // n
        fwd = (src[:, None] == jnp.arange(chunk, dtype=I32)[None, :]).astype(BF16)
        perms.append(jnp.stack([fwd, fwd.T]))
    m = jnp.arange(tile, dtype=I32)[:, None]
    j = jnp.arange(2 * tile, dtype=I32)[None, :]
    dist = jnp.where(j < tile, m - j + tile, m - (j - tile))
    ok_cur = (j >= tile) & (dist >= 0)
    ok_prev = (j < tile) & (dist <= DIL_WINDOW_KEYS)
    biases = []
    for r in DIL_DILATIONS:
        per_head = [-(slopes[h] * r) * dist.astype(F32) for h in range(N_HEADS)]
        with_prev = jnp.concatenate([jnp.where(ok_cur | ok_prev, b, NEG) for b in per_head], axis=0)
        no_prev = jnp.concatenate([jnp.where(ok_cur, b, NEG) for b in per_head], axis=0)
        biases.append(jnp.stack([no_prev, with_prev]))
    src_lane = jnp.arange(V7X_LANES, dtype=I32)[:, None]
    spread = ((src_lane < 3 * N_HEADS)
              & (src_lane % N_HEADS == jnp.arange(GROUP_WIDTH, dtype=I32)[None, :] // HEAD_DIM)).astype(BF16)
    return jnp.stack(perms), jnp.stack(biases), spread


def _dil_attention(ql, kl, vl, slopes):
    b, s, w = ql.shape
    perms, biases, spread = _dil_constants(slopes, s)
    row = pl.BlockSpec((1, s, w), lambda i: (i, 0, 0))
    const = lambda a: pl.BlockSpec(a.shape, lambda i: (0,) * a.ndim)
    return pl.pallas_call(
        functools.partial(_dil_kernel, seq=s),
        grid=(b,),
        in_specs=[row, row, row, const(perms), const(biases), const(spread)],
        out_specs=row,
        out_shape=jax.ShapeDtypeStruct((b, s, w), BF16),
        scratch_shapes=[pltpu.VMEM((s, w), BF16)] * 3
                       + [pltpu.VMEM((len(DIL_DILATIONS), s, w), BF16),
                          pltpu.VMEM((len(DIL_DILATIONS), s, V7X_LANES), F32)],
        compiler_params=_cparams("arbitrary"),
        name="dil_attn",
    )(ql, kl, vl, perms, biases, spread)


def _memattn_kernel(q_ref, mk_ref, mv_ref, o_ref):
    q = q_ref[0]
    tile = q.shape[0]
    lane = lax.broadcasted_iota(I32, q.shape, 1)
    qs = jnp.concatenate([jnp.where(_head_mask(lane, h), q, jnp.zeros_like(q)) for h in range(N_HEADS)], axis=0)
    s = _dot_nt(qs, mk_ref[0])
    m = jnp.max(s, axis=1, keepdims=True)
    p = jnp.exp(s - m)
    l = jnp.sum(p, axis=1, keepdims=True)
    pv = _dot(p.astype(BF16), mv_ref[0]) * (1.0 / l)
    out = jnp.zeros(q.shape, F32)
    for h in range(N_HEADS):
        out = jnp.where(_head_mask(lane, h), pv[h * tile:(h + 1) * tile], out)
    o_ref[0] = out.astype(o_ref.dtype)


def _mem_attention(mq, mk, mv):
    b, s, w = mq.shape
    m = mk.shape[1]
    tile = min(TOK_TILE, s)
    return pl.pallas_call(
        _memattn_kernel,
        grid=(b, s // tile),
        in_specs=[pl.BlockSpec((1, tile, w), lambda bb, ii: (bb, ii, 0)),
                  pl.BlockSpec((1, m, w), lambda bb, ii: (bb, 0, 0)),
                  pl.BlockSpec((1, m, w), lambda bb, ii: (bb, 0, 0))],
        out_specs=pl.BlockSpec((1, tile, w), lambda bb, ii: (bb, ii, 0)),
        out_shape=jax.ShapeDtypeStruct((b, s, w), BF16),
        compiler_params=_cparams("arbitrary", "arbitrary"),
        name="mem_attn",
    )(mq, mk, mv)


def _outrouter_kernel(x_ref, od_ref, ol_ref, om_ref, wo_ref, g_ref, wr_ref, br_ref,
                      x2_ref, h2_ref, route_ref, gate_ref, cnt_ref, run_ref, *, tm, n_exp):
    step = pl.program_id(0)

    @pl.when(step == 0)
    def _():
        run_ref[...] = jnp.zeros(run_ref.shape, F32)

    c1, c2 = DIFF_WIDTH, DIFF_WIDTH + GROUP_WIDTH
    mix = (_dot(od_ref[...], wo_ref[0:c1, :]) + _dot(ol_ref[...], wo_ref[c1:c2, :])
           + _dot(om_ref[...], wo_ref[c2:, :]))
    x2 = x_ref[...] + mix
    x2_ref[...] = x2
    h2 = _rms(x2, g_ref[...])
    h2_ref[...] = h2

    h_hi = h2.astype(BF16)
    h_lo = (h2 - h_hi.astype(F32)).astype(BF16)
    logits = (_dot(h_hi, wr_ref[0]) + (_dot(h_hi, wr_ref[1]) + _dot(h_lo, wr_ref[0]))) + br_ref[...]
    work = logits.T[:n_exp]
    erow = lax.broadcasted_iota(I32, work.shape, 0)
    erow_rev = (n_exp - 1 - erow).astype(F32)
    sel = jnp.zeros(work.shape, F32)
    vals, idxs = [], []
    for _ in range(TOP_K):
        mx = jnp.max(work, axis=0, keepdims=True)
        key = jnp.max(jnp.where(work == mx, erow_rev, -1.0), axis=0, keepdims=True)
        hit = erow_rev == key
        sel = jnp.where(hit, 1.0, sel)
        work = jnp.where(hit, -jnp.inf, work)
        vals.append(mx)
        idxs.append(n_exp - 1 - key.astype(I32))

    r_io = lax.broadcasted_iota(I32, (tm, tm), 0)
    c_io = lax.broadcasted_iota(I32, (tm, tm), 1)
    earlier = jnp.where(r_io < c_io, 1.0, 0.0).astype(BF16)
    before = _dot(sel.astype(BF16), earlier) + run_ref[...]
    run_ref[...] = run_ref[...] + jnp.sum(sel, axis=1, keepdims=True)
    cnt_ref[...] = jnp.broadcast_to(run_ref[...], cnt_ref.shape).astype(I32)

    exps = [jnp.exp(v - vals[0]) for v in vals]
    inv = 1.0 / (exps[0] + exps[1] + exps[2] + exps[3])
    krow = lax.broadcasted_iota(I32, (2 * TOP_K, tm), 0)
    route = jnp.zeros((2 * TOP_K, tm), I32)
    gate_t = jnp.zeros((V7X_LANES, tm), F32)
    grow = lax.broadcasted_iota(I32, gate_t.shape, 0)
    for k in range(TOP_K):
        rk = jnp.sum(jnp.where(erow == idxs[k], before, 0.0), axis=0, keepdims=True)
        route = jnp.where(krow == k, idxs[k], route)
        route = jnp.where(krow == TOP_K + k, rk.astype(I32), route)
        gate_t = jnp.where(grow == k, exps[k] * inv, gate_t)
    route_ref[...] = route
    gate_ref[...] = gate_t.T


def _out_router(x2d, o_diff, o_dil, o_mem, w_out, g, w_router_p, b_router_p, n_exp):
    t, d = x2d.shape
    tm = min(TOK_TILE, t)
    row = lambda n: pl.BlockSpec((tm, n), lambda i: (i, 0))
    full = lambda a: pl.BlockSpec(a.shape, lambda i: (0,) * a.ndim)
    return pl.pallas_call(
        functools.partial(_outrouter_kernel, tm=tm, n_exp=n_exp),
        grid=(t // tm,),
        in_specs=[row(d), row(DIFF_WIDTH), row(GROUP_WIDTH), row(GROUP_WIDTH),
                  full(w_out), full(g), full(w_router_p), full(b_router_p)],
        out_specs=[row(d), row(d), pl.BlockSpec((2 * TOP_K, tm), lambda i: (0, i)), row(V7X_LANES),
                   pl.BlockSpec((n_exp, V7X_LANES), lambda i: (0, 0))],
        out_shape=[jax.ShapeDtypeStruct((t, d), F32), jax.ShapeDtypeStruct((t, d), F32),
                   jax.ShapeDtypeStruct((2 * TOP_K, t), I32), jax.ShapeDtypeStruct((t, V7X_LANES), F32),
                   jax.ShapeDtypeStruct((n_exp, V7X_LANES), I32)],
        scratch_shapes=[pltpu.VMEM((n_exp, 1), F32)],
        compiler_params=_cparams("arbitrary"),
        name="out_router",
    )(x2d, o_diff, o_dil, o_mem, w_out, g, w_router_p, b_router_p)


def _row_copy(src, dst, sem):
    return pltpu.make_async_copy(src, dst, sem)


def _dispatch_kernel(dest_ref, h_ref, xs_hbm, sem, *, tm):
    def issue(t, carry):
        for k in range(TOP_K):
            d = dest_ref[k, t]
            _row_copy(h_ref.at[pl.ds(t, 1), :], xs_hbm.at[pl.ds(d, 1), :], sem).start()
        return carry

    lax.fori_loop(0, tm, issue, 0, unroll=DMA_UNROLL)
    for _ in range(TOP_K):
        _row_copy(h_ref, xs_hbm.at[pl.ds(0, tm), :], sem).wait()


def _dispatch(h2, dest):
    t, d = h2.shape
    tm = min(DISPATCH_TILE, t)
    return pl.pallas_call(
        functools.partial(_dispatch_kernel, tm=tm),
        grid=(t // tm,),
        in_specs=[pl.BlockSpec((TOP_K, tm), lambda i: (0, i), memory_space=pltpu.SMEM),
                  pl.BlockSpec((tm, d), lambda i: (i, 0))],
        out_specs=pl.BlockSpec(memory_space=pl.ANY),
        out_shape=jax.ShapeDtypeStruct((t * TOP_K, d), h2.dtype),
        scratch_shapes=[pltpu.SemaphoreType.DMA],
        compiler_params=_cparams("arbitrary"),
        name="dispatch",
    )(dest, h2)


def _split_gate_up_kernel(w_ref, sel_ref, wg_ref, wu_ref):
    x = w_ref[0].astype(BF16)
    k, n = sel_ref.shape[1], sel_ref.shape[2]
    for j in range(wg_ref.shape[2] // n):
        blk = x[:, j * k:(j + 1) * k]
        wg_ref[0, :, j * n:(j + 1) * n] = _dot(blk, sel_ref[0]).astype(BF16)
        wu_ref[0, :, j * n:(j + 1) * n] = _dot(blk, sel_ref[1]).astype(BF16)


def _split_gate_up(w):
    e, d, f2 = w.shape
    f = f2 // 2
    rows, n = min(TOK_TILE, d), V7X_MXU_DIM
    src = jnp.arange(2 * n, dtype=I32)[:, None]
    dst = jnp.arange(n, dtype=I32)[None, :]
    sel = jnp.stack([src == 2 * dst, src == 2 * dst + 1]).astype(BF16)
    out = jax.ShapeDtypeStruct((e, d, f), BF16)
    return pl.pallas_call(
        _split_gate_up_kernel,
        grid=(e, d // rows),
        in_specs=[pl.BlockSpec((1, rows, f2), lambda i, j: (i, j, 0)),
                  pl.BlockSpec(sel.shape, lambda i, j: (0, 0, 0))],
        out_specs=[pl.BlockSpec((1, rows, f), lambda i, j: (i, j, 0))] * 2,
        out_shape=[out, out],
        compiler_params=_cparams("arbitrary", "arbitrary"),
        name="split_gate_up",
    )(w, sel)


def _experts_kernel(tile_ref, exp_ref, lo_ref, hi_ref, act_ref, fresh_ref,
                    xs_ref, wg_ref, wu_ref, bg_ref, bu_ref, wd_ref, bd_ref, y_ref, wd_bf16, *, tm):
    w = pl.program_id(0)

    @pl.when(fresh_ref[w] == 1)
    def _():
        wd_bf16[...] = wd_ref[0].astype(BF16)

    lo, hi, half = lo_ref[w], hi_ref[w], tm // 2
    active = act_ref[w] == 1
    top_only = active & (hi <= half)
    bottom_only = active & (lo >= half)

    def mlp(r0, nr):
        x = xs_ref[r0:r0 + nr, :].astype(BF16)
        gate = jnp.minimum(_dot(x, wg_ref[0]) + bg_ref[0], SWIGLU_LIMIT)
        up = jnp.clip(_dot(x, wu_ref[0]) + bu_ref[0], -SWIGLU_LIMIT, SWIGLU_LIMIT)
        act = (up + 1.0) * (gate * jax.nn.sigmoid(gate * SWIGLU_ALPHA))
        y = _dot(act.astype(BF16), wd_bf16[...]) + bd_ref[0]
        row = r0 + lax.broadcasted_iota(I32, (nr, V7X_LANES), 0)
        mine = (row >= lo) & (row < hi)
        for s in range(ROW_SUBLANES):
            y_ref[pl.ds(r0 * ROW_SUBLANES + s, nr, stride=ROW_SUBLANES), :] = jnp.where(
                mine, y[:, s * V7X_LANES:(s + 1) * V7X_LANES], 0.0)

    def zero(r0, nr):
        y_ref[r0 * ROW_SUBLANES:(r0 + nr) * ROW_SUBLANES, :] = jnp.zeros((nr * ROW_SUBLANES, V7X_LANES), F32)

    @pl.when(active & jnp.logical_not(top_only | bottom_only))
    def _():
        mlp(0, tm)

    @pl.when(top_only)
    def _():
        mlp(0, half)
        zero(half, half)

    @pl.when(bottom_only)
    def _():
        zero(0, half)
        mlp(half, half)

    @pl.when(jnp.logical_not(active))
    def _():
        zero(0, tm)


def _experts(xs, meta, wg, wu, bg, bu, wd, bd):
    n, d = xs.shape
    e, _, f = wg.shape
    tm = min(ROW_TILE, n)
    n_items = meta[0].shape[0]
    xmap = lambda w, tile, ex, *_: (tile[w], 0)
    wmap = lambda w, tile, ex, *_: (ex[w], 0, 0)
    grid_spec = pltpu.PrefetchScalarGridSpec(
        num_scalar_prefetch=len(meta),
        grid=(n_items,),
        in_specs=[pl.BlockSpec((tm, d), xmap),
                  pl.BlockSpec((1, d, f), wmap), pl.BlockSpec((1, d, f), wmap),
                  pl.BlockSpec((1, 1, f), wmap), pl.BlockSpec((1, 1, f), wmap),
                  pl.BlockSpec((1, f, d), wmap), pl.BlockSpec((1, 1, d), wmap)],
        out_specs=pl.BlockSpec((tm * ROW_SUBLANES, V7X_LANES), lambda w, *_: (w, 0)),
        scratch_shapes=[pltpu.VMEM((f, d), BF16)],
    )
    assert d == ROW_SUBLANES * V7X_LANES
    return pl.pallas_call(
        functools.partial(_experts_kernel, tm=tm),
        grid_spec=grid_spec,
        out_shape=jax.ShapeDtypeStruct((n_items * tm * ROW_SUBLANES, V7X_LANES), F32),
        compiler_params=_cparams("arbitrary"),
        name="experts",
    )(*meta, xs, wg, wu, bg, bu, wd, bd)


def _expert_work_items(counts, n_rows, tm):
    e = counts.shape[0]
    n_tiles = n_rows // tm
    n_items = n_tiles + e - 1
    ends = jnp.cumsum(counts)
    starts = ends - counts
    first_tile = starts // tm
    n_t = jnp.where(counts > 0, (ends - 1) // tm - first_tile + 1, 0)
    item_end = jnp.cumsum(n_t)
    item_start = item_end - n_t
    total = item_end[-1]
    w = jnp.arange(n_items, dtype=I32)
    active = w < total
    w_eff = jnp.minimum(w, total - 1)
    ex = jnp.minimum(jnp.sum((item_end[None, :] <= w_eff[:, None]).astype(I32), axis=1), e - 1)
    mine = ex[:, None] == jnp.arange(e, dtype=I32)[None, :]
    pick = lambda v: jnp.sum(jnp.where(mine, v[None, :], 0), axis=1)
    tile = (pick(first_tile) + (w_eff - pick(item_start))).astype(I32)
    lo = jnp.clip(pick(starts) - tile * tm, 0, tm).astype(I32)
    hi = jnp.clip(pick(ends) - tile * tm, 0, tm).astype(I32)
    fresh = jnp.concatenate([jnp.ones((1,), I32), (ex[1:] != ex[:-1]).astype(I32)])
    return (tile, ex, lo, hi, active.astype(I32), fresh), first_tile, item_start


def _combine_kernel(dest_ref, dest_next_ref, x2_ref, gate_ref, g_ref, y_hbm, o_ref, ybuf, sems, *, tm, n_steps):
    i = pl.program_id(0)
    slot = i % 2

    def gather(dref, into):
        def issue(t, carry):
            for k in range(TOP_K):
                src = pl.multiple_of(dref[k, t] * ROW_SUBLANES, ROW_SUBLANES)
                dst = pl.multiple_of(t * ROW_SUBLANES, ROW_SUBLANES)
                _row_copy(y_hbm.at[pl.ds(src, ROW_SUBLANES), :], ybuf.at[into, k, pl.ds(dst, ROW_SUBLANES), :],
                          sems.at[into]).start()
            return carry
        lax.fori_loop(0, tm, issue, 0, unroll=DMA_UNROLL)

    @pl.when(i == 0)
    def _():
        gather(dest_ref, 0)

    @pl.when(i + 1 < n_steps)
    def _():
        gather(dest_next_ref, 1 - slot)

    for k in range(TOP_K):
        _row_copy(y_hbm.at[pl.ds(0, tm * ROW_SUBLANES), :], ybuf.at[slot, k], sems.at[slot]).wait()

    gates = gate_ref[...]
    cols = []
    for s in range(ROW_SUBLANES):
        acc = x2_ref[:, s * V7X_LANES:(s + 1) * V7X_LANES]
        for k in range(TOP_K):
            acc = acc + ybuf[slot, k, pl.ds(s, tm, stride=ROW_SUBLANES), :] * gates[:, k:k + 1]
        cols.append(acc)
    o_ref[...] = _rms(jnp.concatenate(cols, axis=1), g_ref[...])


def _combine(y, y_row, x2, gates, g_final):
    t, d = x2.shape
    tm = min(DMA_TILE, t)
    n_steps = t // tm
    rows_spec = lambda f: pl.BlockSpec((TOP_K, tm), f, memory_space=pltpu.SMEM)
    return pl.pallas_call(
        functools.partial(_combine_kernel, tm=tm, n_steps=n_steps),
        grid=(n_steps,),
        in_specs=[rows_spec(lambda i: (0, i)),
                  rows_spec(lambda i: (0, jnp.minimum(i + 1, n_steps - 1))),
                  pl.BlockSpec((tm, d), lambda i: (i, 0)),
                  pl.BlockSpec((tm, V7X_LANES), lambda i: (i, 0)),
                  pl.BlockSpec((1, d), lambda i: (0, 0)),
                  pl.BlockSpec(memory_space=pl.ANY)],
        out_specs=pl.BlockSpec((tm, d), lambda i: (i, 0)),
        out_shape=jax.ShapeDtypeStruct((t, d), F32),
        scratch_shapes=[pltpu.VMEM((2, TOP_K, tm * ROW_SUBLANES, V7X_LANES), F32), pltpu.SemaphoreType.DMA((2,))],
        compiler_params=_cparams("arbitrary"),
        name="combine",
    )(y_row, y_row, x2, gates, g_final, y)


def _alibi_slopes():
    n = 2 * N_HEADS
    s = 2.0 ** (-8.0 * (jnp.arange(n, dtype=F32) + 1.0) / n)
    return s[0::2], s[1::2]


def kernel(x, mem, attn_norm_g, mem_norm_g, w_in, diff_lambda_qk, diff_subln_g, w_mem_kv, w_out,
           ffn_norm_g, w_router, b_router, w_gate_up, b_gate_up, w_down, b_down, final_norm_g):
    b, s, d = x.shape
    t = b * s
    assert w_in.shape[0] == 1, "the final norm is fused into the layer's combine step: one layer only"
    assert s % (DIL_DILATIONS[-1] * DIL_TILE) == 0 and t % TOK_TILE == 0
    n_exp = w_router.shape[-1]
    diff_slopes, dil_slopes = _alibi_slopes()
    x2d = x.reshape(t, d)

    for l in range(1):
        lam_init = 0.8 - 0.6 * math.exp(-0.3 * l)
        mk, mv = _mem_kv(mem, mem_norm_g[l][None], w_mem_kv[l].astype(BF16))
        qd, kd, vd, ql, kl, vl, mq = _in_proj(x2d, attn_norm_g[l][None], w_in[l])
        sh = lambda a: a.reshape(b, s, a.shape[-1])
        o_diff = _diff_attention(sh(qd), sh(kd), sh(vd), diff_slopes, diff_lambda_qk[l],
                                 diff_subln_g[l][None], lam_init)
        o_dil = _dil_attention(sh(ql), sh(kl), sh(vl), dil_slopes)
        o_mem = _mem_attention(sh(mq), mk, mv)

        w_router_f = jnp.zeros((d, V7X_LANES), F32).at[:, :n_exp].set(w_router[l])
        w_router_hi = w_router_f.astype(BF16)
        w_router_p = jnp.stack([w_router_hi, (w_router_f - w_router_hi.astype(F32)).astype(BF16)])
        b_router_p = jnp.full((1, V7X_LANES), NEG, F32).at[0, :n_exp].set(b_router[l])
        x2, h2, route, gates, cnt = _out_router(
            x2d, o_diff.reshape(t, DIFF_WIDTH), o_dil.reshape(t, GROUP_WIDTH), o_mem.reshape(t, GROUP_WIDTH),
            w_out[l].astype(BF16), ffn_norm_g[l][None], w_router_p, b_router_p, n_exp)

        counts = cnt[:, 0]
        starts = jnp.cumsum(counts) - counts
        onehot = route[:TOP_K, :, None] == jnp.arange(n_exp, dtype=I32)[None, None, :]
        pick = lambda v: jnp.sum(jnp.where(onehot, v[None, None, :], 0), axis=-1)
        dest = (route[TOP_K:] + pick(starts)).astype(I32)

        xs = _dispatch(h2, dest)

        f = w_down.shape[2]
        tm_rows = min(ROW_TILE, t * TOP_K)
        w_gate, w_up = _split_gate_up(w_gate_up[l])
        bgu = b_gate_up[l].reshape(n_exp, 1, f, 2)
        meta, first_tile, item_start = _expert_work_items(counts, t * TOP_K, tm_rows)
        y = _experts(xs, meta, w_gate, w_up, bgu[..., 0], bgu[..., 1], w_down[l], b_down[l][:, None, :])

        item = pick(item_start) + dest // tm_rows - pick(first_tile)
        y_row = (item * tm_rows + dest % tm_rows).astype(I32)
        out = _combine(y, y_row, x2, gates, final_norm_g[None])
    return out.reshape(b, s, d)
```

```python
import functools
import math

import jax
import jax.numpy as jnp
from jax import lax
from jax.experimental import pallas as pl
from jax.experimental.pallas import tpu as pltpu

F32 = jnp.float32
BF16 = jnp.bfloat16
I32 = jnp.int32

NORM_EPS = 1e-5
HEAD_DIM = 64
N_HEADS = 4
DIFF_VDIM = 2 * HEAD_DIM
DIFF_WIDTH = N_HEADS * DIFF_VDIM
GROUP_WIDTH = N_HEADS * HEAD_DIM
DIL_DILATIONS = (1, 4, 16)
DIL_WINDOW_KEYS = 128
TOP_K = 4
SWIGLU_ALPHA = 1.702
SWIGLU_LIMIT = 7.0
NEG = -1e30

V7X_LANES = 128
V7X_SUBLANES = 8
V7X_MXU_DIM = 256
V7X_VMEM_LIMIT = 56 * 1024 * 1024

TOK_TILE = 512
PROJ_TILE = 1024
ATT_TILE = 512
POS_SPLIT = 64
DIFF_HEADS_PER_STEP = 1
DIL_TILE = DIL_WINDOW_KEYS
DIL_CHUNK = 256
DIL_UNROLL = 16
ROW_TILE = 512
ROW_SUBLANES = V7X_SUBLANES
DMA_TILE = 256
DISPATCH_TILE = 1024
DMA_UNROLL = 4


def _cparams(*sem):
    return pltpu.CompilerParams(dimension_semantics=tuple(sem), vmem_limit_bytes=V7X_VMEM_LIMIT)


def _rms(x, g):
    return x * lax.rsqrt(jnp.mean(x * x, axis=-1, keepdims=True) + NORM_EPS) * g


def _dot_nt(a, b):
    return lax.dot_general(a, b, (((1,), (1,)), ((), ())), preferred_element_type=F32)


def _dot(a, b):
    return jnp.dot(a, b, preferred_element_type=F32)


def _memkv_kernel(mem_ref, g_ref, w_ref, mk_ref, mv_ref):
    h = _rms(mem_ref[0], g_ref[...]).astype(BF16)
    kv = _dot(h, w_ref[...])
    mk_ref[0] = kv[:, :GROUP_WIDTH].astype(BF16)
    mv_ref[0] = kv[:, GROUP_WIDTH:].astype(BF16)


def _mem_kv(mem, g, w):
    b, m, d = mem.shape
    out = jax.ShapeDtypeStruct((b, m, GROUP_WIDTH), BF16)
    return pl.pallas_call(
        _memkv_kernel,
        grid=(b,),
        in_specs=[pl.BlockSpec((1, m, d), lambda i: (i, 0, 0)),
                  pl.BlockSpec((1, d), lambda i: (0, 0)),
                  pl.BlockSpec((d, 2 * GROUP_WIDTH), lambda i: (0, 0))],
        out_specs=[pl.BlockSpec((1, m, GROUP_WIDTH), lambda i: (i, 0, 0))] * 2,
        out_shape=[out, out],
        compiler_params=_cparams("arbitrary"),
        name="mem_kv",
    )(mem, g, w)


_IN_SEGMENTS = ((DIFF_WIDTH, HEAD_DIM ** -0.5), (DIFF_WIDTH, 1.0), (DIFF_WIDTH, 1.0),
                (GROUP_WIDTH, HEAD_DIM ** -0.5), (GROUP_WIDTH, 1.0), (GROUP_WIDTH, 1.0),
                (GROUP_WIDTH, HEAD_DIM ** -0.5))


def _inproj_kernel(x_ref, g_ref, w_ref, *refs):
    out_refs, w_bf16 = refs[:-1], refs[-1]

    @pl.when(pl.program_id(0) == 0)
    def _():
        w_bf16[...] = w_ref[...].astype(BF16)

    h = _rms(x_ref[...], g_ref[...]).astype(BF16)
    c0 = 0
    for ref, (n, scale) in zip(out_refs, _IN_SEGMENTS):
        p = _dot(h, w_bf16[:, c0:c0 + n])
        if scale != 1.0:
            p = p * scale
        ref[...] = p.astype(BF16)
        c0 += n


def _in_proj(x2d, g, w):
    t, d = x2d.shape
    tm = min(PROJ_TILE, t)
    return pl.pallas_call(
        _inproj_kernel,
        grid=(t // tm,),
        in_specs=[pl.BlockSpec((tm, d), lambda i: (i, 0)),
                  pl.BlockSpec((1, d), lambda i: (0, 0)),
                  pl.BlockSpec(w.shape, lambda i: (0, 0))],
        out_specs=[pl.BlockSpec((tm, n), lambda i: (i, 0)) for n, _ in _IN_SEGMENTS],
        out_shape=[jax.ShapeDtypeStruct((t, n), BF16) for n, _ in _IN_SEGMENTS],
        scratch_shapes=[pltpu.VMEM(w.shape, BF16)],
        compiler_params=_cparams("arbitrary"),
        name="in_proj",
    )(x2d, g, w)


def _diff_kernel(slopes_ref, lam_ref, g_ref, q_ref, k_ref, v_ref, kpos_ref, ones_ref, o_ref,
                 m_ref, a_ref, s_ref, *, lam_init, tile):
    n_full = 2 * pl.program_id(2)
    lane = lax.broadcasted_iota(I32, (tile, DIFF_VDIM), 1)
    heads = range(DIFF_HEADS_PER_STEP)
    qs = []
    for hd in heads:
        slope = slopes_ref[DIFF_HEADS_PER_STEP * pl.program_id(1) + hd]
        qpos = jnp.where(lane == 0, slope * float(POS_SPLIT), jnp.where(lane == 1, slope, 0.0)).astype(BF16)
        for sub in range(2):
            q = q_ref[0, sub * tile:(sub + 1) * tile, hd * DIFF_VDIM:(hd + 1) * DIFF_VDIM]
            zero = jnp.zeros_like(q)
            qs.append(jnp.concatenate([jnp.where(lane < HEAD_DIM, q, zero), qpos], axis=1))
            qs.append(jnp.concatenate([jnp.where(lane >= HEAD_DIM, q, zero), qpos], axis=1))
    chains = lambda subs: [(hd, 4 * hd + 2 * sub + mp) for hd in heads for sub in subs for mp in range(2)]
    sub_a, sub_b, both = chains((0,)), chains((1,)), chains((0, 1))

    m_ref[...] = jnp.full(m_ref.shape, NEG, F32)
    a_ref[...] = jnp.zeros(a_ref.shape, F32)

    def scores(j, slot, which):
        start = pl.multiple_of(j * tile, tile)
        kk = [jnp.concatenate([k_ref[0, pl.ds(start, tile), hd * DIFF_VDIM:(hd + 1) * DIFF_VDIM],
                               kpos_ref[pl.ds(start, tile), :]], axis=1) for hd in heads]
        for hd, c in which:
            s_ref[slot, c] = _dot_nt(qs[c], kk[hd])

    def consume(j, slot, which, diagonal):
        start = pl.multiple_of(j * tile, tile)
        vv = [jnp.concatenate([v_ref[0, pl.ds(start, tile), hd * DIFF_VDIM:(hd + 1) * DIFF_VDIM], ones_ref[...]],
                              axis=1) for hd in heads]
        for hd, c in which:
            s = s_ref[slot, c]
            if diagonal:
                row = lax.broadcasted_iota(I32, s.shape, 0)
                col = lax.broadcasted_iota(I32, s.shape, 1)
                s = jnp.where(col <= row, s, NEG)
            m_old = m_ref[c]
            m_new = jnp.maximum(m_old, jnp.max(s, axis=1, keepdims=True))
            p = jnp.exp(s - m_new)
            a_ref[c] = jnp.exp(m_old - m_new) * a_ref[c] + _dot(p.astype(BF16), vv[hd])
            m_ref[c] = m_new

    scores(0, 0, both)

    def pair(p, carry):
        j = 2 * p
        scores(j + 1, 1, both)
        consume(j, 0, both, False)
        scores(j + 2, 0, both)
        consume(j + 1, 1, both, False)
        return carry

    lax.fori_loop(0, n_full // 2, pair, 0)
    scores(n_full + 1, 1, sub_b)
    consume(n_full, 0, sub_a, True)
    consume(n_full, 0, sub_b, False)
    consume(n_full + 1, 1, sub_b, True)

    lv = lam_ref[...]
    lam = (jnp.exp(jnp.sum(lv[0:1] * lv[1:2], axis=1, keepdims=True))
           - jnp.exp(jnp.sum(lv[2:3] * lv[3:4], axis=1, keepdims=True)) + lam_init)
    n = DIFF_VDIM
    for hd in heads:
        for sub in range(2):
            a1, a2 = a_ref[4 * hd + 2 * sub], a_ref[4 * hd + 2 * sub + 1]
            o = (a1[:, :n] * (1.0 / a1[:, n:n + 1])) - lam * (a2[:, :n] * (1.0 / a2[:, n:n + 1]))
            o = _rms(o, g_ref[...]) * (1.0 - lam_init)
            o_ref[0, sub * tile:(sub + 1) * tile, hd * n:(hd + 1) * n] = o.astype(o_ref.dtype)


def _diff_attention(qd, kd, vd, slopes, lam_qk, subln_g, lam_init):
    b, s, _ = qd.shape
    tile = ATT_TILE
    assert s % (2 * tile) == 0
    pos = jnp.arange(s, dtype=I32)[:, None]
    lane = jnp.arange(V7X_LANES, dtype=I32)[None, :]
    kpos = jnp.where(lane == 0, pos // POS_SPLIT, jnp.where(lane == 1, pos % POS_SPLIT, 0)).astype(BF16)
    ones = jnp.broadcast_to(lane == 0, (tile, V7X_LANES)).astype(BF16)
    vec = lambda shape: pltpu.VMEM(shape, F32)
    const = lambda a: pl.BlockSpec(a.shape, lambda bb, hh, ii: (0, 0))
    width = DIFF_HEADS_PER_STEP * DIFF_VDIM
    n_chains = 4 * DIFF_HEADS_PER_STEP
    return pl.pallas_call(
        functools.partial(_diff_kernel, lam_init=lam_init, tile=tile),
        grid=(b, N_HEADS // DIFF_HEADS_PER_STEP, s // (2 * tile)),
        in_specs=[pl.BlockSpec(memory_space=pltpu.SMEM), const(lam_qk), const(subln_g),
                  pl.BlockSpec((1, 2 * tile, width), lambda bb, hh, ii: (bb, ii, hh)),
                  pl.BlockSpec((1, s, width), lambda bb, hh, ii: (bb, 0, hh)),
                  pl.BlockSpec((1, s, width), lambda bb, hh, ii: (bb, 0, hh)),
                  const(kpos), const(ones)],
        out_specs=pl.BlockSpec((1, 2 * tile, width), lambda bb, hh, ii: (bb, ii, hh)),
        out_shape=jax.ShapeDtypeStruct((b, s, DIFF_WIDTH), BF16),
        scratch_shapes=[vec((n_chains, tile, 1)), vec((n_chains, tile, 2 * DIFF_VDIM)),
                        vec((2, n_chains, tile, tile))],
        compiler_params=_cparams("arbitrary", "arbitrary", "arbitrary"),
        name="diff_attn",
    )(slopes, lam_qk, subln_g, qd, kd, vd, kpos, ones)


def _head_mask(lane, h):
    return (lane >= h * HEAD_DIM) & (lane < (h + 1) * HEAD_DIM)


def _split3(x):
    hi = x.astype(BF16)
    rem = x - hi.astype(F32)
    mid = rem.astype(BF16)
    return hi, mid, (rem - mid.astype(F32)).astype(BF16)


def _dil_kernel(q_ref, k_ref, v_ref, perm_ref, bias_ref, spread_ref, o_ref, qc, kc, vc, res_o, res_l, *, seq):
    tile, chunk = DIL_TILE, DIL_CHUNK
    n_blocks, n_chunks = seq // tile, seq // chunk
    lane = lax.broadcasted_iota(I32, (tile, GROUP_WIDTH), 1)
    tail_lane = lax.broadcasted_iota(I32, (tile, V7X_LANES), 1)

    def to_class_order(bi, r):
        n, sr = chunk // r, seq // r

        def one_chunk(t, carry):
            src0 = pl.multiple_of(t * chunk, chunk)
            for src, dst in ((q_ref, qc), (k_ref, kc), (v_ref, vc)):
                y = _dot(perm_ref[bi - 1, 0], src[0, pl.ds(src0, chunk), :]).astype(BF16)
                for rho in range(r):
                    dst[pl.ds(pl.multiple_of(rho * sr + t * n, n), n), :] = y[rho * n:(rho + 1) * n, :]
            return carry

        lax.fori_loop(0, n_chunks, one_chunk, 0, unroll=4)

    def attend(bi, r, load):
        blocks_per_class = (seq // r) // tile

        def one_block(g, in_class):
            cur = pl.multiple_of(g * tile, tile)
            prev = pl.multiple_of(jnp.maximum(g - 1, 0) * tile, tile)
            q = load(0, cur)
            qs = jnp.concatenate([jnp.where(_head_mask(lane, h), q, jnp.zeros_like(q)) for h in range(N_HEADS)], axis=0)
            kk = jnp.concatenate([load(1, prev), load(1, cur)], axis=0)
            vv = jnp.concatenate([load(2, prev), load(2, cur)], axis=0)
            if in_class is None:
                bias = jnp.where((g % blocks_per_class) != 0, bias_ref[bi, 1], bias_ref[bi, 0])
            else:
                bias = bias_ref[bi, 1 if in_class else 0]
            s = _dot_nt(qs, kk) + bias
            m = jnp.max(s, axis=1, keepdims=True)
            p = jnp.exp(s - m)
            l = jnp.sum(p, axis=1, keepdims=True)
            pv = _dot(p.astype(BF16), vv) * (1.0 / l)
            lse = m + jnp.log(l)
            out = jnp.zeros((tile, GROUP_WIDTH), F32)
            tail = jnp.zeros((tile, V7X_LANES), F32)
            for h in range(N_HEADS):
                rows = slice(h * tile, (h + 1) * tile)
                out = jnp.where(_head_mask(lane, h), pv[rows], out)
                tail = jnp.where(tail_lane == h, lse[rows], tail)
            res_o[bi, pl.ds(cur, tile), :] = out.astype(BF16)
            res_l[bi, pl.ds(cur, tile), :] = tail

        def block_group(outer, carry):
            for u in range(DIL_UNROLL):
                if blocks_per_class <= DIL_UNROLL:
                    in_class = u % blocks_per_class
                else:
                    in_class = u if u else None
                one_block(outer * DIL_UNROLL + u, in_class)
            return carry

        assert n_blocks % DIL_UNROLL == 0 and max(blocks_per_class, DIL_UNROLL) % min(blocks_per_class, DIL_UNROLL) == 0
        lax.fori_loop(0, n_blocks // DIL_UNROLL, block_group, 0)

    refs = (q_ref, k_ref, v_ref)
    scratch = (qc, kc, vc)
    attend(0, 1, lambda which, start: refs[which][0, pl.ds(start, tile), :])
    for bi, r in enumerate(DIL_DILATIONS[1:], start=1):
        to_class_order(bi, r)
        attend(bi, r, lambda which, start: scratch[which][pl.ds(start, tile), :])

    def merge_chunk(t, carry):
        base = pl.multiple_of(t * chunk, chunk)
        def pack3(lse):
            hi, mid, lo = _split3(lse)
            return (hi.astype(F32) + pltpu.roll(mid.astype(F32), N_HEADS, 1)
                    + pltpu.roll(lo.astype(F32), 2 * N_HEADS, 1)).astype(BF16)

        outs, lses = [], []
        for bi, r in enumerate(DIL_DILATIONS):
            if r == 1:
                out = res_o[0, pl.ds(base, chunk), :].astype(F32)
                packed = pack3(res_l[0, pl.ds(base, chunk), :])
            else:
                n, sr = chunk // r, seq // r
                rows = [pl.ds(pl.multiple_of(rho * sr + t * n, n), n) for rho in range(r)]
                back = perm_ref[bi - 1, 1]
                out = _dot(back, jnp.concatenate([res_o[bi, rw, :] for rw in rows], axis=0))
                packed = _dot(back, pack3(jnp.concatenate([res_l[bi, rw, :] for rw in rows], axis=0))).astype(BF16)
            outs.append(out)
            lses.append(_dot(packed, spread_ref[...]))
        big = jnp.maximum(jnp.maximum(lses[0], lses[1]), lses[2])
        wts = [jnp.exp(z - big) for z in lses]
        o = (wts[0] * outs[0] + wts[1] * outs[1] + wts[2] * outs[2]) * (1.0 / (wts[0] + wts[1] + wts[2]))
        o_ref[0, pl.ds(base, chunk), :] = o.astype(o_ref.dtype)
        return carry

    lax.fori_loop(0, n_chunks, merge_chunk, 0, unroll=4)


def _dil_constants(slopes, seq):
    chunk, tile = DIL_CHUNK, DIL_TILE
    perms = []
    for r in DIL_DILATIONS[1:]:
        n = chunk // r
        dst = jnp.arange(chunk, dtype=I32)
        src = r * (dst % n) + dst // n
        fwd = (src[:, None] == jnp.arange(chunk, dtype=I32)[None, :]).astype(BF16)
        perms.append(jnp.stack([fwd, fwd.T]))
    m = jnp.arange(tile, dtype=I32)[:, None]
    j = jnp.arange(2 * tile, dtype=I32)[None, :]
    dist = jnp.where(j < tile, m - j + tile, m - (j - tile))
    ok_cur = (j >= tile) & (dist >= 0)
    ok_prev = (j < tile) & (dist <= DIL_WINDOW_KEYS)
    biases = []
    for r in DIL_DILATIONS:
        per_head = [-(slopes[h] * r) * dist.astype(F32) for h in range(N_HEADS)]
        with_prev = jnp.concatenate([jnp.where(ok_cur | ok_prev, b, NEG) for b in per_head], axis=0)
        no_prev = jnp.concatenate([jnp.where(ok_cur, b, NEG) for b in per_head], axis=0)
        biases.append(jnp.stack([no_prev, with_prev]))
    src_lane = jnp.arange(V7X_LANES, dtype=I32)[:, None]
    spread = ((src_lane < 3 * N_HEADS)
              & (src_lane % N_HEADS == jnp.arange(GROUP_WIDTH, dtype=I32)[None, :] // HEAD_DIM)).astype(BF16)
    return jnp.stack(perms), jnp.stack(biases), spread


def _dil_attention(ql, kl, vl, slopes):
    b, s, w = ql.shape
    perms, biases, spread = _dil_constants(slopes, s)
    row = pl.BlockSpec((1, s, w), lambda i: (i, 0, 0))
    const = lambda a: pl.BlockSpec(a.shape, lambda i: (0,) * a.ndim)
    return pl.pallas_call(
        functools.partial(_dil_kernel, seq=s),
        grid=(b,),
        in_specs=[row, row, row, const(perms), const(biases), const(spread)],
        out_specs=row,
        out_shape=jax.ShapeDtypeStruct((b, s, w), BF16),
        scratch_shapes=[pltpu.VMEM((s, w), BF16)] * 3
                       + [pltpu.VMEM((len(DIL_DILATIONS), s, w), BF16),
                          pltpu.VMEM((len(DIL_DILATIONS), s, V7X_LANES), F32)],
        compiler_params=_cparams("arbitrary"),
        name="dil_attn",
    )(ql, kl, vl, perms, biases, spread)


def _memattn_kernel(q_ref, mk_ref, mv_ref, o_ref):
    q = q_ref[0]
    tile = q.shape[0]
    lane = lax.broadcasted_iota(I32, q.shape, 1)
    qs = jnp.concatenate([jnp.where(_head_mask(lane, h), q, jnp.zeros_like(q)) for h in range(N_HEADS)], axis=0)
    s = _dot_nt(qs, mk_ref[0])
    m = jnp.max(s, axis=1, keepdims=True)
    p = jnp.exp(s - m)
    l = jnp.sum(p, axis=1, keepdims=True)
    pv = _dot(p.astype(BF16), mv_ref[0]) * (1.0 / l)
    out = jnp.zeros(q.shape, F32)
    for h in range(N_HEADS):
        out = jnp.where(_head_mask(lane, h), pv[h * tile:(h + 1) * tile], out)
    o_ref[0] = out.astype(o_ref.dtype)


def _mem_attention(mq, mk, mv):
    b, s, w = mq.shape
    m = mk.shape[1]
    tile = min(PROJ_TILE, s)
    return pl.pallas_call(
        _memattn_kernel,
        grid=(b, s // tile),
        in_specs=[pl.BlockSpec((1, tile, w), lambda bb, ii: (bb, ii, 0)),
                  pl.BlockSpec((1, m, w), lambda bb, ii: (bb, 0, 0)),
                  pl.BlockSpec((1, m, w), lambda bb, ii: (bb, 0, 0))],
        out_specs=pl.BlockSpec((1, tile, w), lambda bb, ii: (bb, ii, 0)),
        out_shape=jax.ShapeDtypeStruct((b, s, w), BF16),
        compiler_params=_cparams("arbitrary", "arbitrary"),
        name="mem_attn",
    )(mq, mk, mv)


def _outrouter_kernel(x_ref, od_ref, ol_ref, om_ref, wo_ref, g_ref, wr_ref, br_ref,
                      x2_ref, h2_ref, route_ref, gate_ref, cnt_ref, run_ref, *, tm, n_exp):
    step = pl.program_id(0)

    @pl.when(step == 0)
    def _():
        run_ref[...] = jnp.zeros(run_ref.shape, F32)

    c1, c2 = DIFF_WIDTH, DIFF_WIDTH + GROUP_WIDTH
    mix = (_dot(od_ref[...], wo_ref[0:c1, :]) + _dot(ol_ref[...], wo_ref[c1:c2, :])
           + _dot(om_ref[...], wo_ref[c2:, :]))
    x2 = x_ref[...] + mix
    x2_ref[...] = x2
    h2 = _rms(x2, g_ref[...])
    h2_ref[...] = h2

    h_hi = h2.astype(BF16)
    h_lo = (h2 - h_hi.astype(F32)).astype(BF16)
    logits = (_dot(h_hi, wr_ref[0]) + (_dot(h_hi, wr_ref[1]) + _dot(h_lo, wr_ref[0]))) + br_ref[...]
    work = logits.T[:n_exp]
    erow = lax.broadcasted_iota(I32, work.shape, 0)
    erow_rev = (n_exp - 1 - erow).astype(F32)
    sel = jnp.zeros(work.shape, F32)
    vals, idxs = [], []
    for _ in range(TOP_K):
        mx = jnp.max(work, axis=0, keepdims=True)
        key = jnp.max(jnp.where(work == mx, erow_rev, -1.0), axis=0, keepdims=True)
        hit = erow_rev == key
        sel = jnp.where(hit, 1.0, sel)
        work = jnp.where(hit, -jnp.inf, work)
        vals.append(mx)
        idxs.append(n_exp - 1 - key.astype(I32))

    r_io = lax.broadcasted_iota(I32, (tm, tm), 0)
    c_io = lax.broadcasted_iota(I32, (tm, tm), 1)
    earlier = jnp.where(r_io < c_io, 1.0, 0.0).astype(BF16)
    before = _dot(sel.astype(BF16), earlier) + run_ref[...]
    run_ref[...] = run_ref[...] + jnp.sum(sel, axis=1, keepdims=True)
    cnt_ref[...] = jnp.broadcast_to(run_ref[...], cnt_ref.shape).astype(I32)

    exps = [jnp.exp(v - vals[0]) for v in vals]
    inv = 1.0 / (exps[0] + exps[1] + exps[2] + exps[3])
    krow = lax.broadcasted_iota(I32, (2 * TOP_K, tm), 0)
    route = jnp.zeros((2 * TOP_K, tm), I32)
    gate_t = jnp.zeros((V7X_LANES, tm), F32)
    grow = lax.broadcasted_iota(I32, gate_t.shape, 0)
    for k in range(TOP_K):
        rk = jnp.sum(jnp.where(erow == idxs[k], before, 0.0), axis=0, keepdims=True)
        route = jnp.where(krow == k, idxs[k], route)
        route = jnp.where(krow == TOP_K + k, rk.astype(I32), route)
        gate_t = jnp.where(grow == k, exps[k] * inv, gate_t)
    route_ref[...] = route
    gate_ref[...] = gate_t.T


def _out_router(x2d, o_diff, o_dil, o_mem, w_out, g, w_router_p, b_router_p, n_exp):
    t, d = x2d.shape
    tm = min(TOK_TILE, t)
    row = lambda n: pl.BlockSpec((tm, n), lambda i: (i, 0))
    full = lambda a: pl.BlockSpec(a.shape, lambda i: (0,) * a.ndim)
    return pl.pallas_call(
        functools.partial(_outrouter_kernel, tm=tm, n_exp=n_exp),
        grid=(t // tm,),
        in_specs=[row(d), row(DIFF_WIDTH), row(GROUP_WIDTH), row(GROUP_WIDTH),
                  full(w_out), full(g), full(w_router_p), full(b_router_p)],
        out_specs=[row(d), row(d), pl.BlockSpec((2 * TOP_K, tm), lambda i: (0, i)), row(V7X_LANES),
                   pl.BlockSpec((n_exp, V7X_LANES), lambda i: (0, 0))],
        out_shape=[jax.ShapeDtypeStruct((t, d), F32), jax.ShapeDtypeStruct((t, d), F32),
                   jax.ShapeDtypeStruct((2 * TOP_K, t), I32), jax.ShapeDtypeStruct((t, V7X_LANES), F32),
                   jax.ShapeDtypeStruct((n_exp, V7X_LANES), I32)],
        scratch_shapes=[pltpu.VMEM((n_exp, 1), F32)],
        compiler_params=_cparams("arbitrary"),
        name="out_router",
    )(x2d, o_diff, o_dil, o_mem, w_out, g, w_router_p, b_router_p)


def _row_copy(src, dst, sem):
    return pltpu.make_async_copy(src, dst, sem)


def _dispatch_kernel(dest_ref, h_ref, xs_hbm, sem, *, tm):
    def issue(t, carry):
        for k in range(TOP_K):
            d = dest_ref[k, t]
            _row_copy(h_ref.at[pl.ds(t, 1), :], xs_hbm.at[pl.ds(d, 1), :], sem).start()
        return carry

    lax.fori_loop(0, tm, issue, 0, unroll=DMA_UNROLL)
    for _ in range(TOP_K):
        _row_copy(h_ref, xs_hbm.at[pl.ds(0, tm), :], sem).wait()


def _dispatch(h2, dest):
    t, d = h2.shape
    tm = min(DISPATCH_TILE, t)
    return pl.pallas_call(
        functools.partial(_dispatch_kernel, tm=tm),
        grid=(t // tm,),
        in_specs=[pl.BlockSpec((TOP_K, tm), lambda i: (0, i), memory_space=pltpu.SMEM),
                  pl.BlockSpec((tm, d), lambda i: (i, 0))],
        out_specs=pl.BlockSpec(memory_space=pl.ANY),
        out_shape=jax.ShapeDtypeStruct((t * TOP_K, d), h2.dtype),
        scratch_shapes=[pltpu.SemaphoreType.DMA],
        compiler_params=_cparams("arbitrary"),
        name="dispatch",
    )(dest, h2)


def _split_gate_up_kernel(w_ref, sel_ref, wg_ref, wu_ref):
    x = w_ref[0].astype(BF16)
    k, n = sel_ref.shape[1], sel_ref.shape[2]
    for j in range(wg_ref.shape[2] // n):
        blk = x[:, j * k:(j + 1) * k]
        wg_ref[0, :, j * n:(j + 1) * n] = _dot(blk, sel_ref[0]).astype(BF16)
        wu_ref[0, :, j * n:(j + 1) * n] = _dot(blk, sel_ref[1]).astype(BF16)


def _split_gate_up(w):
    e, d, f2 = w.shape
    f = f2 // 2
    rows, n = min(PROJ_TILE, d), V7X_MXU_DIM
    src = jnp.arange(2 * n, dtype=I32)[:, None]
    dst = jnp.arange(n, dtype=I32)[None, :]
    sel = jnp.stack([src == 2 * dst, src == 2 * dst + 1]).astype(BF16)
    out = jax.ShapeDtypeStruct((e, d, f), BF16)
    return pl.pallas_call(
        _split_gate_up_kernel,
        grid=(e, d // rows),
        in_specs=[pl.BlockSpec((1, rows, f2), lambda i, j: (i, j, 0)),
                  pl.BlockSpec(sel.shape, lambda i, j: (0, 0, 0))],
        out_specs=[pl.BlockSpec((1, rows, f), lambda i, j: (i, j, 0))] * 2,
        out_shape=[out, out],
        compiler_params=_cparams("arbitrary", "arbitrary"),
        name="split_gate_up",
    )(w, sel)


def _experts_kernel(tile_ref, exp_ref, lo_ref, hi_ref, act_ref, fresh_ref,
                    xs_ref, wg_ref, wu_ref, bg_ref, bu_ref, wd_ref, bd_ref, y_ref, wd_bf16, *, tm):
    w = pl.program_id(0)

    @pl.when(fresh_ref[w] == 1)
    def _():
        wd_bf16[...] = wd_ref[0].astype(BF16)

    lo, hi, half = lo_ref[w], hi_ref[w], tm // 2
    active = act_ref[w] == 1
    top_only = active & (hi <= half)
    bottom_only = active & (lo >= half)

    def mlp(r0, nr):
        x = xs_ref[r0:r0 + nr, :].astype(BF16)
        gate = jnp.minimum(_dot(x, wg_ref[0]) + bg_ref[0], SWIGLU_LIMIT)
        up = jnp.clip(_dot(x, wu_ref[0]) + bu_ref[0], -SWIGLU_LIMIT, SWIGLU_LIMIT)
        act = (up + 1.0) * (gate * jax.nn.sigmoid(gate * SWIGLU_ALPHA))
        y = _dot(act.astype(BF16), wd_bf16[...]) + bd_ref[0]
        row = r0 + lax.broadcasted_iota(I32, (nr, V7X_LANES), 0)
        mine = (row >= lo) & (row < hi)
        for s in range(ROW_SUBLANES):
            y_ref[pl.ds(r0 * ROW_SUBLANES + s, nr, stride=ROW_SUBLANES), :] = jnp.where(
                mine, y[:, s * V7X_LANES:(s + 1) * V7X_LANES], 0.0)

    def zero(r0, nr):
        y_ref[r0 * ROW_SUBLANES:(r0 + nr) * ROW_SUBLANES, :] = jnp.zeros((nr * ROW_SUBLANES, V7X_LANES), F32)

    @pl.when(active & jnp.logical_not(top_only | bottom_only))
    def _():
        mlp(0, tm)

    @pl.when(top_only)
    def _():
        mlp(0, half)
        zero(half, half)

    @pl.when(bottom_only)
    def _():
        zero(0, half)
        mlp(half, half)

    @pl.when(jnp.logical_not(active))
    def _():
        zero(0, tm)


def _experts(xs, meta, wg, wu, bg, bu, wd, bd):
    n, d = xs.shape
    e, _, f = wg.shape
    tm = min(ROW_TILE, n)
    n_items = meta[0].shape[0]
    xmap = lambda w, tile, ex, *_: (tile[w], 0)
    wmap = lambda w, tile, ex, *_: (ex[w], 0, 0)
    grid_spec = pltpu.PrefetchScalarGridSpec(
        num_scalar_prefetch=len(meta),
        grid=(n_items,),
        in_specs=[pl.BlockSpec((tm, d), xmap),
                  pl.BlockSpec((1, d, f), wmap), pl.BlockSpec((1, d, f), wmap),
                  pl.BlockSpec((1, 1, f), wmap), pl.BlockSpec((1, 1, f), wmap),
                  pl.BlockSpec((1, f, d), wmap), pl.BlockSpec((1, 1, d), wmap)],
        out_specs=pl.BlockSpec((tm * ROW_SUBLANES, V7X_LANES), lambda w, *_: (w, 0)),
        scratch_shapes=[pltpu.VMEM((f, d), BF16)],
    )
    assert d == ROW_SUBLANES * V7X_LANES
    return pl.pallas_call(
        functools.partial(_experts_kernel, tm=tm),
        grid_spec=grid_spec,
        out_shape=jax.ShapeDtypeStruct((n_items * tm * ROW_SUBLANES, V7X_LANES), F32),
        compiler_params=_cparams("arbitrary"),
        name="experts",
    )(*meta, xs, wg, wu, bg, bu, wd, bd)


def _expert_work_items(counts, n_rows, tm):
    e = counts.shape[0]
    n_tiles = n_rows // tm
    n_items = n_tiles + e - 1
    ends = jnp.cumsum(counts)
    starts = ends - counts
    first_tile = starts // tm
    n_t = jnp.where(counts > 0, (ends - 1) // tm - first_tile + 1, 0)
    item_end = jnp.cumsum(n_t)
    item_start = item_end - n_t
    total = item_end[-1]
    w = jnp.arange(n_items, dtype=I32)
    active = w < total
    w_eff = jnp.minimum(w, total - 1)
    ex = jnp.minimum(jnp.sum((item_end[None, :] <= w_eff[:, None]).astype(I32), axis=1), e - 1)
    mine = ex[:, None] == jnp.arange(e, dtype=I32)[None, :]
    pick = lambda v: jnp.sum(jnp.where(mine, v[None, :], 0), axis=1)
    tile = (pick(first_tile) + (w_eff - pick(item_start))).astype(I32)
    lo = jnp.clip(pick(starts) - tile * tm, 0, tm).astype(I32)
    hi = jnp.clip(pick(ends) - tile * tm, 0, tm).astype(I32)
    fresh = jnp.concatenate([jnp.ones((1,), I32), (ex[1:] != ex[:-1]).astype(I32)])
    return (tile, ex, lo, hi, active.astype(I32), fresh), first_tile, item_start


def _combine_kernel(dest_ref, dest_next_ref, x2_ref, gate_ref, g_ref, y_hbm, o_ref, ybuf, sems, *, tm, n_steps):
    i = pl.program_id(0)
    slot = i % 2

    def gather(dref, into):
        def issue(t, carry):
            for k in range(TOP_K):
                src = pl.multiple_of(dref[k, t] * ROW_SUBLANES, ROW_SUBLANES)
                dst = pl.multiple_of(t * ROW_SUBLANES, ROW_SUBLANES)
                _row_copy(y_hbm.at[pl.ds(src, ROW_SUBLANES), :], ybuf.at[into, k, pl.ds(dst, ROW_SUBLANES), :],
                          sems.at[into]).start()
            return carry
        lax.fori_loop(0, tm, issue, 0, unroll=DMA_UNROLL)

    @pl.when(i == 0)
    def _():
        gather(dest_ref, 0)

    @pl.when(i + 1 < n_steps)
    def _():
        gather(dest_next_ref, 1 - slot)

    for k in range(TOP_K):
        _row_copy(y_hbm.at[pl.ds(0, tm * ROW_SUBLANES), :], ybuf.at[slot, k], sems.at[slot]).wait()

    gates = gate_ref[...]
    cols = []
    for s in range(ROW_SUBLANES):
        acc = x2_ref[:, s * V7X_LANES:(s + 1) * V7X_LANES]
        for k in range(TOP_K):
            acc = acc + ybuf[slot, k, pl.ds(s, tm, stride=ROW_SUBLANES), :] * gates[:, k:k + 1]
        cols.append(acc)
    o_ref[...] = _rms(jnp.concatenate(cols, axis=1), g_ref[...])


def _combine(y, y_row, x2, gates, g_final):
    t, d = x2.shape
    tm = min(DMA_TILE, t)
    n_steps = t // tm
    rows_spec = lambda f: pl.BlockSpec((TOP_K, tm), f, memory_space=pltpu.SMEM)
    return pl.pallas_call(
        functools.partial(_combine_kernel, tm=tm, n_steps=n_steps),
        grid=(n_steps,),
        in_specs=[rows_spec(lambda i: (0, i)),
                  rows_spec(lambda i: (0, jnp.minimum(i + 1, n_steps - 1))),
                  pl.BlockSpec((tm, d), lambda i: (i, 0)),
                  pl.BlockSpec((tm, V7X_LANES), lambda i: (i, 0)),
                  pl.BlockSpec((1, d), lambda i: (0, 0)),
                  pl.BlockSpec(memory_space=pl.ANY)],
        out_specs=pl.BlockSpec((tm, d), lambda i: (i, 0)),
        out_shape=jax.ShapeDtypeStruct((t, d), F32),
        scratch_shapes=[pltpu.VMEM((2, TOP_K, tm * ROW_SUBLANES, V7X_LANES), F32), pltpu.SemaphoreType.DMA((2,))],
        compiler_params=_cparams("arbitrary"),
        name="combine",
    )(y_row, y_row, x2, gates, g_final, y)


def _alibi_slopes():
    n = 2 * N_HEADS
    s = 2.0 ** (-8.0 * (jnp.arange(n, dtype=F32) + 1.0) / n)
    return s[0::2], s[1::2]


def kernel(x, mem, attn_norm_g, mem_norm_g, w_in, diff_lambda_qk, diff_subln_g, w_mem_kv, w_out,
           ffn_norm_g, w_router, b_router, w_gate_up, b_gate_up, w_down, b_down, final_norm_g):
    b, s, d = x.shape
    t = b * s
    assert w_in.shape[0] == 1, "the final norm is fused into the layer's combine step: one layer only"
    assert s % (DIL_DILATIONS[-1] * DIL_TILE) == 0 and t % PROJ_TILE == 0 and s % PROJ_TILE == 0
    n_exp = w_router.shape[-1]
    diff_slopes, dil_slopes = _alibi_slopes()
    x2d = x.reshape(t, d)

    for l in range(1):
        lam_init = 0.8 - 0.6 * math.exp(-0.3 * l)
        mk, mv = _mem_kv(mem, mem_norm_g[l][None], w_mem_kv[l].astype(BF16))
        qd, kd, vd, ql, kl, vl, mq = _in_proj(x2d, attn_norm_g[l][None], w_in[l])
        sh = lambda a: a.reshape(b, s, a.shape[-1])
        o_diff = _diff_attention(sh(qd), sh(kd), sh(vd), diff_slopes, diff_lambda_qk[l],
                                 diff_subln_g[l][None], lam_init)
        o_dil = _dil_attention(sh(ql), sh(kl), sh(vl), dil_slopes)
        o_mem = _mem_attention(sh(mq), mk, mv)

        w_router_f = jnp.zeros((d, V7X_LANES), F32).at[:, :n_exp].set(w_router[l])
        w_router_hi = w_router_f.astype(BF16)
        w_router_p = jnp.stack([w_router_hi, (w_router_f - w_router_hi.astype(F32)).astype(BF16)])
        b_router_p = jnp.full((1, V7X_LANES), NEG, F32).at[0, :n_exp].set(b_router[l])
        x2, h2, route, gates, cnt = _out_router(
            x2d, o_diff.reshape(t, DIFF_WIDTH), o_dil.reshape(t, GROUP_WIDTH), o_mem.reshape(t, GROUP_WIDTH),
            w_out[l].astype(BF16), ffn_norm_g[l][None], w_router_p, b_router_p, n_exp)

        counts = cnt[:, 0]
        starts = jnp.cumsum(counts) - counts
        onehot = route[:TOP_K, :, None] == jnp.arange(n_exp, dtype=I32)[None, None, :]
        pick = lambda v: jnp.sum(jnp.where(onehot, v[None, None, :], 0), axis=-1)
        dest = (route[TOP_K:] + pick(starts)).astype(I32)

        xs = _dispatch(h2, dest)

        f = w_down.shape[2]
        tm_rows = min(ROW_TILE, t * TOP_K)
        w_gate, w_up = _split_gate_up(w_gate_up[l])
        bgu = b_gate_up[l].reshape(n_exp, 1, f, 2)
        meta, first_tile, item_start = _expert_work_items(counts, t * TOP_K, tm_rows)
        y = _experts(xs, meta, w_gate, w_up, bgu[..., 0], bgu[..., 1], w_down[l], b_down[l][:, None, :])

        item = pick(item_start) + dest // tm_rows - pick(first_tile)
        y_row = (item * tm_rows + dest % tm_rows).astype(I32)
        out = _combine(y, y_row, x2, gates, final_norm_g[None])
    return out.reshape(b, s, d)
```
